```python
import jax, jax.numpy as jnp
from jax import lax
import numpy as np

D_MODEL = 1024
BATCH = 2
SEQ = 8192
DEPTH = 4
DEC_BATCH = 128
DEC_SEQ = 1
PAST_LEN = 2048
PAGE_SIZE = 128

N_A_LAYERS = DEPTH // 2
N_B_LAYERS = DEPTH - N_A_LAYERS
D_FF = 2816
GMLP_EXPAND = 6
D_V = GMLP_EXPAND * D_MODEL // 2
CHUNK = 128
N_GROUPS_A = 8
CG = D_V // N_GROUPS_A
N_HEADS = 16
HEAD_DIM = D_MODEL // N_HEADS
N_KV = 4
HPG = N_HEADS // N_KV
CMP_LEN = 32
CMP_STRIDE = 16
CMP_HID = 256
SEL_BLOCK = 64
N_SEL = 16
WINDOW = 512
Q_BLOCK = 128
N_CACHE_SLOTS = 4
N_KV_PROJ = 6
EPS = 1e-6
NEG = -1e30
FORCE_BONUS = 1e4

kernel_name = 'yoco_gmlp_nsa_macaron_step'


def rms_norm(x, g):
    xf = x.astype(jnp.float32)
    y = xf * lax.rsqrt(jnp.mean(xf * xf, axis=-1, keepdims=True) + EPS)
    return (y * g.astype(jnp.float32)).astype(x.dtype)


def layer_norm(x, g, b):
    xf = x.astype(jnp.float32)
    mu = jnp.mean(xf, axis=-1, keepdims=True)
    var = jnp.mean(jnp.square(xf - mu), axis=-1, keepdims=True)
    return ((xf - mu) * lax.rsqrt(var + EPS) * g + b).astype(x.dtype)


def modulate(h, shift, scale):
    return h * (1 + scale[:, None, :]) + shift[:, None, :]


def ada_norm(x, g, m):
    return modulate(rms_norm(x, g), m[:, 0], m[:, 1])


def swiglu(h, w_gate, w_up, w_down):
    return (jax.nn.silu(h @ w_gate) * (h @ w_up)) @ w_down


def pad_seq(a, mult):
    p = (-a.shape[1]) % mult
    return jnp.pad(a, ((0, 0), (0, p)) + ((0, 0),) * (a.ndim - 2))


def alibi_slopes():
    h = jnp.arange(1, N_HEADS + 1, dtype=jnp.float32)
    return jnp.exp2(-8.0 * h / N_HEADS).reshape(N_KV, HPG)


def masked_softmax(s, mask):
    s = jnp.where(mask, s, NEG)
    p = jax.nn.softmax(s, axis=-1)
    return jnp.where(mask, p, 0.0)


def gmlp_mix(h, w_uv, ln_g, ln_b, w_sp, b_sp, w_out):
    B, T, _ = h.shape
    z = jax.nn.gelu(h @ w_uv)
    u, v = z[..., :D_V], z[..., D_V:]
    v = layer_norm(v, ln_g, ln_b)
    L = min(T, CHUNK)
    Tp = -(-T // L) * L
    vp = jnp.pad(v, ((0, 0), (0, Tp - T), (0, 0))).reshape(B, Tp // L, L, N_GROUPS_A, CG)
    w = jnp.where(jnp.tril(jnp.ones((L, L), bool)), w_sp[:, :L, :L], 0.0)
    s = jnp.einsum('gts,bnsgc->bntgc', w, vp) + b_sp[:, :L].T[None, None, :, :, None]
    s = s.reshape(B, Tp, D_V)[:, :T]
    return (u * s) @ w_out, v


def compress(k, w1, w2, pe):
    B, T = k.shape[:2]
    r_n = CMP_LEN // CMP_STRIDE
    n_chunk = T // CMP_STRIDE
    nc = n_chunk - r_n + 1
    chunks = k[:, :n_chunk * CMP_STRIDE].reshape(B, n_chunk, CMP_STRIDE, N_KV, HEAD_DIM)
    w1r = w1.reshape(r_n, CMP_STRIDE, HEAD_DIM, CMP_HID)
    hid = jnp.einsum('ld,ldh->h', pe, w1.reshape(CMP_LEN, HEAD_DIM, CMP_HID))
    for r in range(r_n):
        hid = hid + jnp.einsum('bnsgd,sdh->bngh', chunks[:, r:r + nc], w1r[r])
    out = jax.nn.gelu(hid) @ w2
    cpos = jnp.arange(nc) * CMP_STRIDE + CMP_LEN - 1
    return out, cpos


def nsa_core(q, gates, qpos, kc, vc, cpos, ks, vs, kw, vw, wpos):
    f32 = jnp.float32
    B = q.shape[0]
    sl = alibi_slopes()[None, :, :, None, None]
    d_c = (qpos[:, None] - cpos[None, :]).astype(f32)
    s_c = jnp.einsum('btghd,bngd->bghtn', q, kc).astype(f32) - sl * d_c
    p_c = masked_softmax(s_c, d_c >= 0)
    o_c = jnp.einsum('bghtn,bngd->btghd', p_c.astype(vc.dtype), vc)
    nc = kc.shape[1]
    ns = ks.shape[1] // SEL_BLOCK
    c_start = jnp.arange(nc)[:, None] * CMP_STRIDE
    s_start = jnp.arange(ns)[None, :] * SEL_BLOCK
    cover = ((c_start < s_start + SEL_BLOCK) & (c_start + CMP_LEN > s_start)).astype(f32)
    imp = jnp.einsum('bghtn,nj->bgtj', p_c, cover)
    q_blk = (qpos // SEL_BLOCK)[:, None]
    blk = jnp.arange(ns)[None, :]
    forced = (blk == 0) | (blk == q_blk) | (blk == q_blk - 1)
    imp = jnp.where(blk <= q_blk, imp + jnp.where(forced, FORCE_BONUS, 0.0), NEG)
    top_s, top_i = lax.top_k(imp, min(N_SEL, ns))
    bi = jnp.arange(B)[:, None, None, None]
    gi = jnp.arange(N_KV)[None, :, None, None]
    ks_b = ks.reshape(B, ns, SEL_BLOCK, N_KV, HEAD_DIM).transpose(0, 3, 1, 2, 4)
    vs_b = vs.reshape(B, ns, SEL_BLOCK, N_KV, HEAD_DIM).transpose(0, 3, 1, 2, 4)
    k_sel = ks_b[bi, gi, top_i]
    v_sel = vs_b[bi, gi, top_i]
    kpos = top_i[..., None] * SEL_BLOCK + jnp.arange(SEL_BLOCK)
    d_s = (qpos[None, None, :, None, None] - kpos).astype(f32)[:, :, None]
    m_s = (top_s > NEG / 2)[:, :, None, :, :, None] & (d_s >= 0)
    s_s = jnp.einsum('btghd,bgtjsd->bghtjs', q, k_sel).astype(f32) - sl[..., None] * d_s
    sh = s_s.shape
    p_s = masked_softmax(s_s.reshape(sh[:4] + (-1,)), m_s.reshape(m_s.shape[:4] + (-1,))).reshape(sh)
    o_s = jnp.einsum('bghtjs,bgtjsd->btghd', p_s.astype(v_sel.dtype), v_sel)
    d_w = (qpos[:, None] - wpos[None, :]).astype(f32)
    m_w = (d_w >= 0) & (d_w <= WINDOW) & (wpos[None, :] >= 0)
    s_w = jnp.einsum('btghd,blgd->bghtl', q, kw).astype(f32) - sl * d_w
    p_w = masked_softmax(s_w, m_w)
    o_w = jnp.einsum('bghtl,blgd->btghd', p_w.astype(vw.dtype), vw)
    return gates[..., 0:1] * o_c + gates[..., 1:2] * o_s + gates[..., 2:3] * o_w


def prompt_ctx(kv, cmp_w1, cmp_w2, cmp_pe):
    T = kv.shape[1]
    kc, cpos = compress(kv[:, :, 0], cmp_w1[0], cmp_w2[0], cmp_pe[0])
    vc, _ = compress(kv[:, :, 1], cmp_w1[1], cmp_w2[1], cmp_pe[1])
    ks = pad_seq(kv[:, :, 2], SEL_BLOCK)
    vs = pad_seq(kv[:, :, 3], SEL_BLOCK)
    win = jnp.pad(kv[:, :, 4:6], ((0, 0), (WINDOW, 0), (0, 0), (0, 0), (0, 0)))
    ctx = (kc, vc, cpos, ks, vs, win)
    new_state = (kv[:, :, :N_CACHE_SLOTS], kv[:, T - min(WINDOW, T):, 4:6])
    return ctx, new_state


def prompt_attend(q, gates, ctx):
    kc, vc, cpos, ks, vs, win = ctx
    B, T = q.shape[:2]
    qb = min(Q_BLOCK, T)
    nb = T // qb
    qs = q.reshape(B, nb, qb, N_KV, HPG, HEAD_DIM).swapaxes(0, 1)
    gs = gates.reshape(B, nb, qb, N_KV, HPG, 3).swapaxes(0, 1)

    def block(args):
        i, q_i, g_i = args
        start = i * qb
        w_i = lax.dynamic_slice_in_dim(win, start, WINDOW + qb, axis=1)
        qpos = start + jnp.arange(qb)
        wpos = start - WINDOW + jnp.arange(WINDOW + qb)
        return nsa_core(q_i, g_i, qpos, kc, vc, cpos, ks, vs, w_i[:, :, 0], w_i[:, :, 1], wpos)

    out = lax.map(block, (jnp.arange(nb), qs, gs))
    return out.swapaxes(0, 1).reshape(B, T, N_KV, HPG, HEAD_DIM)


def make_sample_ctx(cache_kv, page_table, state_win_kv):
    def ctx_fn(kv, cmp_w1, cmp_w2, cmp_pe):
        B, S = kv.shape[:2]
        past_len = page_table.shape[1] * PAGE_SIZE
        past = cache_kv[page_table].reshape(B, past_len, N_CACHE_SLOTS, N_KV, HEAD_DIM)
        full = jnp.concatenate([past, kv[:, :, :N_CACHE_SLOTS]], axis=1)
        kc, cpos = compress(full[:, :, 0], cmp_w1[0], cmp_w2[0], cmp_pe[0])
        vc, _ = compress(full[:, :, 1], cmp_w1[1], cmp_w2[1], cmp_pe[1])
        ks = pad_seq(full[:, :, 2], SEL_BLOCK)
        vs = pad_seq(full[:, :, 3], SEL_BLOCK)
        buf = state_win_kv.shape[1]
        win = jnp.concatenate([state_win_kv, kv[:, :, 4:6]], axis=1)
        wpos = past_len - buf + jnp.arange(buf + S)
        qpos = past_len + jnp.arange(S)
        ctx = (kc, vc, cpos, ks, vs, win, wpos, qpos)
        new_state = (kv[:, :, :N_CACHE_SLOTS], win[:, S:])
        return ctx, new_state
    return ctx_fn


def sample_attend(q, gates, ctx):
    kc, vc, cpos, ks, vs, win, wpos, qpos = ctx
    return nsa_core(q, gates, qpos, kc, vc, cpos, ks, vs, win[:, :, 0], win[:, :, 1], wpos)


def trunk(x, c, ctx_fn, attend_fn, ada_w, ada_b, norm_g, ffn_w_gate, ffn_w_up, ffn_w_down,
          gmlp_w_uv, gmlp_ln_g, gmlp_ln_b, gmlp_w_sp, gmlp_b_sp, gmlp_w_out,
          nsa_w_qg, nsa_w_o, kv_norm_g, kv_ada_w, kv_ada_b, kv_w,
          cmp_w1, cmp_w2, cmp_pe, final_g, final_ada_w, final_ada_b):
    B, T, _ = x.shape
    sc = jax.nn.silu(c)
    v_rows = []
    ctx = None
    kv_state = None
    for l in range(DEPTH):
        if l == N_A_LAYERS:
            kv_mod = (sc @ kv_ada_w + kv_ada_b).reshape(B, 2, D_MODEL)
            h_kv = modulate(rms_norm(x, kv_norm_g), kv_mod[:, 0], kv_mod[:, 1])
            kv = (h_kv @ kv_w).reshape(B, T, N_KV_PROJ, N_KV, HEAD_DIM)
            ctx, kv_state = ctx_fn(kv, cmp_w1, cmp_w2, cmp_pe)
        mod = (sc @ ada_w[l] + ada_b[l]).reshape(B, 3, 3, D_MODEL)
        h = ada_norm(x, norm_g[l, 0], mod[:, 0])
        x = x + 0.5 * mod[:, 0, 2, None, :] * swiglu(h, ffn_w_gate[l, 0], ffn_w_up[l, 0], ffn_w_down[l, 0])
        h = ada_norm(x, norm_g[l, 1], mod[:, 1])
        if l < N_A_LAYERS:
            y, v = gmlp_mix(h, gmlp_w_uv[l], gmlp_ln_g[l], gmlp_ln_b[l], gmlp_w_sp[l], gmlp_b_sp[l], gmlp_w_out[l])
            v_rows.append(v)
        else:
            j = l - N_A_LAYERS
            qg = h @ nsa_w_qg[j]
            q = qg[..., :N_HEADS * HEAD_DIM].reshape(B, T, N_KV, HPG, HEAD_DIM) * HEAD_DIM ** -0.5
            gates = jax.nn.sigmoid(qg[..., N_HEADS * HEAD_DIM:].reshape(B, T, N_KV, HPG, 3))
            y = attend_fn(q, gates, ctx).reshape(B, T, N_HEADS * HEAD_DIM) @ nsa_w_o[j]
        x = x + mod[:, 1, 2, None, :] * y
        h = ada_norm(x, norm_g[l, 2], mod[:, 2])
        x = x + 0.5 * mod[:, 2, 2, None, :] * swiglu(h, ffn_w_gate[l, 1], ffn_w_up[l, 1], ffn_w_down[l, 1])
    fmod = (sc @ final_ada_w + final_ada_b).reshape(B, 2, D_MODEL)
    y = modulate(rms_norm(x, final_g), fmod[:, 0], fmod[:, 1])
    return y, v_rows, kv_state


def setup_inputs(seed: int = 0) -> dict:
    key = jax.random.key(seed)
    ks = jax.random.split(key, 32)

    def nrm(k, shape, scale):
        return jax.random.normal(k, shape, jnp.float32) * scale

    D = D_MODEL
    n_pages = PAST_LEN // PAGE_SIZE
    n_used = DEC_BATCH * n_pages
    n_pool = n_used + max(1, n_used // 4)
    win_buf = min(WINDOW, PAST_LEN)
    page_table = jax.random.permutation(ks[4], n_pool)[:n_used].reshape(DEC_BATCH, n_pages).astype(jnp.int32)
    return {
        'x_prompt': nrm(ks[0], (BATCH, SEQ, D), 1.0),
        'x_sample': nrm(ks[1], (DEC_BATCH, DEC_SEQ, D), 1.0),
        'cache_kv': nrm(ks[2], (n_pool, PAGE_SIZE, N_CACHE_SLOTS, N_KV, HEAD_DIM), 1.0),
        'state_win_kv': nrm(ks[3], (DEC_BATCH, win_buf, 2, N_KV, HEAD_DIM), 1.0),
        'page_table': page_table,
        'c_prompt': nrm(ks[5], (BATCH, D), 1.0),
        'c_sample': nrm(ks[6], (DEC_BATCH, D), 1.0),
        'ada_w': nrm(ks[7], (DEPTH, D, 9 * D), 0.5 * D ** -0.5),
        'ada_b': nrm(ks[8], (DEPTH, 9 * D), 0.01),
        'norm_g': 1.0 + nrm(ks[9], (DEPTH, 3, D), 0.05),
        'ffn_w_gate': nrm(ks[10], (DEPTH, 2, D, D_FF), D ** -0.5),
        'ffn_w_up': nrm(ks[11], (DEPTH, 2, D, D_FF), D ** -0.5),
        'ffn_w_down': nrm(ks[12], (DEPTH, 2, D_FF, D), D_FF ** -0.5),
        'gmlp_w_uv': nrm(ks[13], (N_A_LAYERS, D, 2 * D_V), D ** -0.5),
        'gmlp_ln_g': 1.0 + nrm(ks[14], (N_A_LAYERS, D_V), 0.05),
        'gmlp_ln_b': nrm(ks[15], (N_A_LAYERS, D_V), 0.02),
        'gmlp_w_sp': nrm(ks[16], (N_A_LAYERS, N_GROUPS_A, CHUNK, CHUNK), CHUNK ** -0.5),
        'gmlp_b_sp': 1.0 + nrm(ks[17], (N_A_LAYERS, N_GROUPS_A, CHUNK), 0.1),
        'gmlp_w_out': nrm(ks[18], (N_A_LAYERS, D_V, D), D_V ** -0.5),
        'nsa_w_qg': nrm(ks[19], (N_B_LAYERS, D, N_HEADS * HEAD_DIM + 3 * N_HEADS), D ** -0.5),
        'nsa_w_o': nrm(ks[20], (N_B_LAYERS, N_HEADS * HEAD_DIM, D), (N_HEADS * HEAD_DIM) ** -0.5),
        'kv_norm_g': 1.0 + nrm(ks[21], (D,), 0.05),
        'kv_ada_w': nrm(ks[22], (D, 2 * D), 0.5 * D ** -0.5),
        'kv_ada_b': nrm(ks[23], (2 * D,), 0.01),
        'kv_w': nrm(ks[24], (D, N_KV_PROJ * N_KV * HEAD_DIM), D ** -0.5),
        'cmp_w1': nrm(ks[25], (2, CMP_LEN * HEAD_DIM, CMP_HID), (CMP_LEN * HEAD_DIM) ** -0.5),
        'cmp_w2': nrm(ks[26], (2, CMP_HID, HEAD_DIM), CMP_HID ** -0.5),
        'cmp_pe': nrm(ks[27], (2, CMP_LEN, HEAD_DIM), 0.5),
        'final_g': 1.0 + nrm(ks[28], (D,), 0.05),
        'final_ada_w': nrm(ks[29], (D, 2 * D), 0.5 * D ** -0.5),
        'final_ada_b': nrm(ks[30], (2 * D,), 0.01),
    }


def reference(x_prompt, x_sample, cache_kv, state_win_kv, page_table, c_prompt, c_sample,
              ada_w, ada_b, norm_g, ffn_w_gate, ffn_w_up, ffn_w_down,
              gmlp_w_uv, gmlp_ln_g, gmlp_ln_b, gmlp_w_sp, gmlp_b_sp, gmlp_w_out,
              nsa_w_qg, nsa_w_o, kv_norm_g, kv_ada_w, kv_ada_b, kv_w,
              cmp_w1, cmp_w2, cmp_pe, final_g, final_ada_w, final_ada_b):
    weights = (ada_w, ada_b, norm_g, ffn_w_gate, ffn_w_up, ffn_w_down,
               gmlp_w_uv, gmlp_ln_g, gmlp_ln_b, gmlp_w_sp, gmlp_b_sp, gmlp_w_out,
               nsa_w_qg, nsa_w_o, kv_norm_g, kv_ada_w, kv_ada_b, kv_w,
               cmp_w1, cmp_w2, cmp_pe, final_g, final_ada_w, final_ada_b)
    y_prompt, _, p_state = trunk(x_prompt, c_prompt, prompt_ctx, prompt_attend, *weights)
    kv_prompt, win_prompt = p_state
    sample_ctx = make_sample_ctx(cache_kv, page_table, state_win_kv)
    y_sample, v_rows_s, s_state = trunk(x_sample, c_sample, sample_ctx, sample_attend, *weights)
    kv_sample, win_sample = s_state
    gmlp_v_sample = jnp.stack(v_rows_s)
    return (y_prompt, y_sample, kv_prompt, kv_sample, win_prompt, win_sample, gmlp_v_sample)
```

```python
import functools

import jax
import jax.numpy as jnp
from jax import lax
from jax.experimental import pallas as pl
from jax.experimental.pallas import tpu as pltpu

F32 = jnp.float32
BF16 = jnp.bfloat16

D_MODEL = 1024
D_FF = 2816
D_V = 3072
CHUNK = 128
N_GROUPS_A = 8
CG = D_V // N_GROUPS_A
N_HEADS = 16
HEAD_DIM = 64
N_KV = 4
HPG = N_HEADS // N_KV
CMP_LEN = 32
CMP_STRIDE = 16
CMP_HID = 256
SEL_BLOCK = 64
N_SEL = 16
WINDOW = 512
PAGE = 128
N_KV_PROJ = 6
EPS = 1e-6
NEG = -1e30
M_FLOOR = -1e29
PICKED = -3e38
FORCE_BONUS = 1e4

LANE = 128
VMEM_LIMIT = 56 * 1024 * 1024
FF_CHUNK = 256
UV_CHUNK = 512
PAGES_PER_SEG = 16
XROWS = PAGES_PER_SEG * 8 + 8


def _dot(a, b):
    return jnp.dot(a, b, preferred_element_type=F32)


def _dot_nt(a, b):
    return lax.dot_general(a, b, (((1,), (1,)), ((), ())), preferred_element_type=F32)


def _split_bf16(x):
    hi = x.astype(BF16)
    lo = (x - hi.astype(F32)).astype(BF16)
    return hi, lo


def _params(n_grid):
    return pltpu.CompilerParams(dimension_semantics=("arbitrary",) * n_grid,
                                vmem_limit_bytes=VMEM_LIMIT)


def _full_spec(shape):
    nd = len(shape)
    return pl.BlockSpec(shape, lambda *_: (0,) * nd, pipeline_mode=pl.Buffered(1))


def _ada_norm(x, g, shift, scale):
    ms = jnp.mean(x * x, axis=-1, keepdims=True)
    h = x * lax.rsqrt(ms + EPS) * g
    return h * (1.0 + scale) + shift


def _mod_kernel(c_ref, w_ref, b_ref, o_ref):
    c = c_ref[...]
    a = c * jax.nn.sigmoid(c)
    w = w_ref[0]
    a_hi, a_lo = _split_bf16(a)
    w_hi, w_lo = _split_bf16(w)
    o_ref[0] = _dot(a_hi, w_hi) + _dot(a_hi, w_lo) + _dot(a_lo, w_hi) + b_ref[0]


def _mod_linear(c, w, b):
    n_l, d, n = w.shape
    m = c.shape[0]
    tn = 1024
    return pl.pallas_call(
        _mod_kernel,
        grid=(n_l, n // tn),
        in_specs=[pl.BlockSpec((m, d), lambda l, j: (0, 0)),
                  pl.BlockSpec((1, d, tn), lambda l, j: (l, 0, j)),
                  pl.BlockSpec((1, 1, tn), lambda l, j: (l, 0, j))],
        out_specs=pl.BlockSpec((1, m, tn), lambda l, j: (l, 0, j)),
        out_shape=jax.ShapeDtypeStruct((n_l, m, n), F32),
        compiler_params=_params(2),
    )(c, w, b.reshape(n_l, 1, n))


class _Tok:
    def __init__(self, n_seq, seq_len, per_row):
        self.n_seq, self.seq_len = n_seq, seq_len
        self.n_tok = n_seq * seq_len
        self.per_row = per_row

    def tiles(self, tm):
        if self.per_row:
            return self.n_tok, 1
        assert self.seq_len % tm == 0
        return tm, self.n_tok // tm

    def x_spec(self, tm, width=D_MODEL):
        return pl.BlockSpec((tm, width), lambda i: (i, 0))

    def mod_spec(self, tm):
        if self.per_row:
            return pl.BlockSpec((1, tm, D_MODEL), lambda i: (0, 0, 0))
        per_seq = self.seq_len // tm
        return pl.BlockSpec((1, 1, D_MODEL), lambda i: (i // per_seq, 0, 0))

    def mod_arr(self, m):
        if self.per_row:
            return m.reshape(1, self.n_tok, D_MODEL)
        return m.reshape(self.n_seq, 1, D_MODEL)


def _ffn_kernel(x_ref, sh_ref, sc_ref, gt_ref, g_ref, wg_ref, wu_ref, wd_ref, o_ref, a_scr):
    x = x_ref[...]
    hb = _ada_norm(x, g_ref[...], sh_ref[0], sc_ref[0]).astype(BF16)
    for c in range(D_FF // FF_CHUNK):
        cs = slice(c * FF_CHUNK, (c + 1) * FF_CHUNK)
        gate = _dot(hb, wg_ref[:, cs])
        up = _dot(hb, wu_ref[:, cs])
        a_scr[:, cs] = (gate * jax.nn.sigmoid(gate) * up).astype(BF16)
    y = _dot(a_scr[...], wd_ref[...])
    o_ref[...] = x + (0.5 * gt_ref[0]) * y


def _ffn(tok, x, shift, scale, gate, g, wg, wu, wd):
    tm, nt = tok.tiles(512)
    return pl.pallas_call(
        _ffn_kernel,
        grid=(nt,),
        in_specs=[tok.x_spec(tm), tok.mod_spec(tm), tok.mod_spec(tm), tok.mod_spec(tm),
                  _full_spec((1, D_MODEL)), _full_spec((D_MODEL, D_FF)),
                  _full_spec((D_MODEL, D_FF)), _full_spec((D_FF, D_MODEL))],
        out_specs=tok.x_spec(tm),
        out_shape=jax.ShapeDtypeStruct((tok.n_tok, D_MODEL), F32),
        scratch_shapes=[pltpu.VMEM((tm, D_FF), BF16)],
        compiler_params=_params(1),
    )(x, tok.mod_arr(shift), tok.mod_arr(scale), tok.mod_arr(gate), g.reshape(1, D_MODEL), wg, wu, wd)


def _gmlp_uv(x_ref, sh_ref, sc_ref, g_ref, wuv_ref, lng_ref, lnb_ref, u_scr, v_scr):
    hb = _ada_norm(x_ref[...], g_ref[...], sh_ref[0], sc_ref[0]).astype(BF16)
    for c in range(D_V // UV_CHUNK):
        cs = slice(c * UV_CHUNK, (c + 1) * UV_CHUNK)
        cv = slice(D_V + c * UV_CHUNK, D_V + (c + 1) * UV_CHUNK)
        u_scr[:, cs] = jax.nn.gelu(_dot(hb, wuv_ref[:, cs]))
        v_scr[:, cs] = jax.nn.gelu(_dot(hb, wuv_ref[:, cv]))
    v = v_scr[...]
    mu = jnp.mean(v, axis=-1, keepdims=True)
    vc = v - mu
    var = jnp.mean(vc * vc, axis=-1, keepdims=True)
    return vc * lax.rsqrt(var + EPS) * lng_ref[...] + lnb_ref[...]


def _gmlp_prompt_kernel(x_ref, sh_ref, sc_ref, gt_ref, g_ref, wuv_ref, lng_ref, lnb_ref,
                        wsp_ref, bsp_ref, wout_ref, o_ref, u_scr, v_scr, a_scr):
    v_scr[...] = _gmlp_uv(x_ref, sh_ref, sc_ref, g_ref, wuv_ref, lng_ref, lnb_ref, u_scr, v_scr)
    row = lax.broadcasted_iota(jnp.int32, (CHUNK, CHUNK), 0)
    col = lax.broadcasted_iota(jnp.int32, (CHUNK, CHUNK), 1)
    tm = x_ref.shape[0]
    for grp in range(N_GROUPS_A):
        w = jnp.where(row >= col, wsp_ref[grp], 0.0).astype(BF16)
        cs = slice(grp * CG, (grp + 1) * CG)
        for n in range(tm // CHUNK):
            rs = slice(n * CHUNK, (n + 1) * CHUNK)
            s = _dot(w, v_scr[rs, cs].astype(BF16)) + bsp_ref[:, cs]
            a_scr[rs, cs] = (u_scr[rs, cs] * s).astype(BF16)
    y = _dot(a_scr[...], wout_ref[...])
    o_ref[...] = x_ref[...] + gt_ref[0] * y


def _gmlp_sample_kernel(x_ref, sh_ref, sc_ref, gt_ref, g_ref, wuv_ref, lng_ref, lnb_ref,
                        wrow_ref, brow_ref, wout_ref, o_ref, vout_ref, u_scr, v_scr):
    vn = _gmlp_uv(x_ref, sh_ref, sc_ref, g_ref, wuv_ref, lng_ref, lnb_ref, u_scr, v_scr)
    vout_ref[...] = vn
    s = vn * wrow_ref[...] + brow_ref[...]
    y = _dot((u_scr[...] * s).astype(BF16), wout_ref[...])
    o_ref[...] = x_ref[...] + gt_ref[0] * y


def _gmlp(tok, x, shift, scale, gate, g, wuv, ln_g, ln_b, w_sp, b_sp, wout):
    tm, nt = tok.tiles(256)
    common = [tok.x_spec(tm), tok.mod_spec(tm), tok.mod_spec(tm), tok.mod_spec(tm),
              _full_spec((1, D_MODEL)), _full_spec((D_MODEL, 2 * D_V)),
              _full_spec((1, D_V)), _full_spec((1, D_V))]
    args = [x, tok.mod_arr(shift), tok.mod_arr(scale), tok.mod_arr(gate), g.reshape(1, D_MODEL), wuv,
            ln_g.reshape(1, D_V), ln_b.reshape(1, D_V)]
    x_shape = jax.ShapeDtypeStruct((tok.n_tok, D_MODEL), F32)
    if tok.per_row:
        wrow = jnp.repeat(w_sp[:, 0, 0], CG).reshape(1, D_V)
        brow = jnp.repeat(b_sp[:, 0], CG).reshape(1, D_V)
        return pl.pallas_call(
            _gmlp_sample_kernel,
            grid=(nt,),
            in_specs=common + [_full_spec((1, D_V)), _full_spec((1, D_V)), _full_spec((D_V, D_MODEL))],
            out_specs=[tok.x_spec(tm), tok.x_spec(tm, D_V)],
            out_shape=[x_shape, jax.ShapeDtypeStruct((tok.n_tok, D_V), F32)],
            scratch_shapes=[pltpu.VMEM((tm, D_V), F32), pltpu.VMEM((tm, D_V), F32)],
            compiler_params=_params(1),
        )(*args, wrow, brow, wout)
    bias = jnp.repeat(b_sp.T, CG, axis=1)
    out = pl.pallas_call(
        _gmlp_prompt_kernel,
        grid=(nt,),
        in_specs=common + [_full_spec((N_GROUPS_A, CHUNK, CHUNK)), _full_spec((CHUNK, D_V)),
                           _full_spec((D_V, D_MODEL))],
        out_specs=tok.x_spec(tm),
        out_shape=x_shape,
        scratch_shapes=[pltpu.VMEM((tm, D_V), F32), pltpu.VMEM((tm, D_V), F32),
                        pltpu.VMEM((tm, D_V), BF16)],
        compiler_params=_params(1),
    )(*args, w_sp, bias, wout)
    return out, None


def _kv_prompt_kernel(x_ref, sh_ref, sc_ref, g_ref, w_ref, wvt_ref, kv_ref, k_ref, vt_ref):
    hb = _ada_norm(x_ref[...], g_ref[...], sh_ref[0], sc_ref[0]).astype(BF16)
    kv = _dot(hb, w_ref[...])
    kv_ref[...] = kv
    vt = _dot_nt(wvt_ref[...], hb).astype(BF16)
    tm = x_ref.shape[0]
    gd = N_KV * HEAD_DIM
    for c in range(tm // LANE):
        rs = slice(c * LANE, (c + 1) * LANE)
        vt_ref[0, c] = vt[:, rs]
        for j, slot in enumerate((2, 4)):
            for grp in range(N_KV):
                col = slot * gd + grp * HEAD_DIM
                k_ref[0, j * N_KV + grp, c] = kv[rs, col:col + HEAD_DIM].astype(BF16)


def _kv_sample_kernel(x_ref, sh_ref, sc_ref, g_ref, w_ref, kv_ref):
    hb = _ada_norm(x_ref[...], g_ref[...], sh_ref[0], sc_ref[0]).astype(BF16)
    kv_ref[...] = _dot(hb, w_ref[...])


def _kv_proj(tok, x, shift, scale, g, w, wvt):
    tm, nt = tok.tiles(512)
    n_kv = N_KV_PROJ * N_KV * HEAD_DIM
    in_specs = [tok.x_spec(tm), tok.mod_spec(tm), tok.mod_spec(tm), _full_spec((1, D_MODEL)),
                _full_spec((D_MODEL, n_kv))]
    args = [x, tok.mod_arr(shift), tok.mod_arr(scale), g.reshape(1, D_MODEL), w]
    kv_shape = jax.ShapeDtypeStruct((tok.n_tok, n_kv), F32)
    if tok.per_row:
        return pl.pallas_call(
            _kv_sample_kernel, grid=(nt,), in_specs=in_specs, out_specs=tok.x_spec(tm, n_kv),
            out_shape=kv_shape, compiler_params=_params(1))(*args)
    per_seq = tok.seq_len // tm
    n_qt = tok.seq_len // LANE
    sub = tm // LANE
    return pl.pallas_call(
        _kv_prompt_kernel,
        grid=(nt,),
        in_specs=in_specs + [_full_spec((2 * N_KV * HEAD_DIM, D_MODEL))],
        out_specs=[tok.x_spec(tm, n_kv),
                   pl.BlockSpec((1, 2 * N_KV, sub, LANE, HEAD_DIM),
                                lambda i: (i // per_seq, 0, i % per_seq, 0, 0)),
                   pl.BlockSpec((1, sub, 2 * N_KV * HEAD_DIM, LANE),
                                lambda i: (i // per_seq, i % per_seq, 0, 0))],
        out_shape=[kv_shape,
                   jax.ShapeDtypeStruct((tok.n_seq, 2 * N_KV, n_qt, LANE, HEAD_DIM), BF16),
                   jax.ShapeDtypeStruct((tok.n_seq, n_qt, 2 * N_KV * HEAD_DIM, LANE), BF16)],
        compiler_params=_params(1),
    )(*args, wvt)


def _compress_kernel(tbl_ref, *refs):
    pages = refs[:PAGES_PER_SEG + 1]
    perm_ref, w1_ref, w2_ref, w2t_ref, pe_ref, kc_ref, vct_ref, xs_scr, acca, accb = refs[PAGES_PER_SEG + 1:]
    del tbl_ref
    perm = perm_ref[...]
    for p in range(PAGES_PER_SEG + 1):
        xp = _dot(perm, pages[p][0].astype(BF16))
        for s in range(CMP_STRIDE):
            xs_scr[s, p * 8:(p + 1) * 8, :] = xp[s * 8:(s + 1) * 8, :]
    gd = N_KV * HEAD_DIM
    n_blk = PAGES_PER_SEG * 8
    for slot in range(2):
        acca[...] = jnp.zeros_like(acca)
        accb[...] = jnp.zeros_like(accb)

        def body(s, pacc, slot=slot):
            rows = [xs_scr[s, :, slot * gd + grp * HEAD_DIM: slot * gd + (grp + 1) * HEAD_DIM]
                    for grp in range(N_KV)]
            a = jnp.concatenate(rows, axis=0).astype(BF16)
            wa = w1_ref[slot, s]
            wb = w1_ref[slot, CMP_STRIDE + s]
            acca[...] += _dot(a, wa)
            accb[...] += _dot(a, wb)
            return pacc + _dot(pe_ref[slot, s], wa) + _dot(pe_ref[slot, CMP_STRIDE + s], wb)

        pacc = lax.fori_loop(0, CMP_STRIDE, body, jnp.zeros((16, CMP_HID), F32))
        pe_hid = pacc[0:1]
        for grp in range(N_KV):
            hid = (acca[pl.ds(grp * XROWS, n_blk), :] + accb[pl.ds(grp * XROWS + 1, n_blk), :] + pe_hid)
            act = jax.nn.gelu(hid).astype(BF16)
            if slot == 0:
                kc_ref[0, grp] = _dot(act, w2_ref[0]).astype(BF16)
            else:
                vct_ref[0, grp] = _dot_nt(w2t_ref[1], act).astype(BF16)


def _compress(src, table, w1, w2, w2t, pe, n_seq, n_seg):
    n_blk = PAGES_PER_SEG * 8
    gd2 = 2 * N_KV * HEAD_DIM
    dst = jnp.arange(PAGE)
    src_row = (dst % 8) * CMP_STRIDE + dst // 8
    perm = (jnp.arange(PAGE)[None, :] == src_row[:, None]).astype(BF16)

    def page_spec(p):
        return pl.BlockSpec((1, PAGE, gd2), lambda i, tbl, p=p: (tbl[i, p], 0, 0))

    grid_spec = pltpu.PrefetchScalarGridSpec(
        num_scalar_prefetch=1,
        grid=(n_seq * n_seg,),
        in_specs=[page_spec(p) for p in range(PAGES_PER_SEG + 1)] + [
            pl.BlockSpec((PAGE, PAGE), lambda i, tbl: (0, 0)),
            pl.BlockSpec(w1.shape, lambda i, tbl: (0, 0, 0, 0)),
            pl.BlockSpec(w2.shape, lambda i, tbl: (0, 0, 0)),
            pl.BlockSpec(w2t.shape, lambda i, tbl: (0, 0, 0)),
            pl.BlockSpec(pe.shape, lambda i, tbl: (0, 0, 0, 0))],
        out_specs=[pl.BlockSpec((1, N_KV, n_blk, HEAD_DIM), lambda i, tbl: (i // n_seg, 0, i % n_seg, 0)),
                   pl.BlockSpec((1, N_KV, HEAD_DIM, n_blk), lambda i, tbl: (i // n_seg, 0, 0, i % n_seg))],
        scratch_shapes=[pltpu.VMEM((CMP_STRIDE, XROWS, gd2), F32),
                        pltpu.VMEM((N_KV * XROWS, CMP_HID), F32),
                        pltpu.VMEM((N_KV * XROWS, CMP_HID), F32)])
    return pl.pallas_call(
        _compress_kernel,
        grid_spec=grid_spec,
        out_shape=[jax.ShapeDtypeStruct((n_seq, N_KV, n_seg * n_blk, HEAD_DIM), BF16),
                   jax.ShapeDtypeStruct((n_seq, N_KV, HEAD_DIM, n_seg * n_blk), BF16)],
        compiler_params=_params(1),
    )(table, *([src] * (PAGES_PER_SEG + 1)), perm, w1, w2, w2t, pe)


def _qg_prompt_kernel(x_ref, sh_ref, sc_ref, g_ref, wq_ref, wgt_ref, q_ref, gt_ref):
    hb = _ada_norm(x_ref[...], g_ref[...], sh_ref[0], sc_ref[0]).astype(BF16)
    q = _dot(hb, wq_ref[...]) * (HEAD_DIM ** -0.5)
    for h in range(N_HEADS):
        q_ref[0, h] = q[:, h * HEAD_DIM:(h + 1) * HEAD_DIM].astype(BF16)
    gates = jax.nn.sigmoid(_dot_nt(wgt_ref[...], hb))
    for grp in range(N_KV):
        gt_ref[0, grp] = gates[grp * 16:(grp + 1) * 16, :]


def _qg_sample_kernel(x_ref, sh_ref, sc_ref, g_ref, wq_ref, wg_ref, q_ref, gt_ref):
    hb = _ada_norm(x_ref[...], g_ref[...], sh_ref[0], sc_ref[0]).astype(BF16)
    q_ref[...] = _dot(hb, wq_ref[...]) * (HEAD_DIM ** -0.5)
    gt_ref[...] = jax.nn.sigmoid(_dot(hb, wg_ref[...]))


def _qg_proj(tok, x, shift, scale, g, wq, wg_t, wg_nat):
    tm, nt = tok.tiles(512)
    in_specs = [tok.x_spec(tm), tok.mod_spec(tm), tok.mod_spec(tm), _full_spec((1, D_MODEL)),
                _full_spec((D_MODEL, D_MODEL))]
    args = [x, tok.mod_arr(shift), tok.mod_arr(scale), g.reshape(1, D_MODEL), wq]
    if tok.per_row:
        return pl.pallas_call(
            _qg_sample_kernel, grid=(nt,),
            in_specs=in_specs + [_full_spec((D_MODEL, LANE))],
            out_specs=[tok.x_spec(tm), tok.x_spec(tm, LANE)],
            out_shape=[jax.ShapeDtypeStruct((tok.n_tok, D_MODEL), F32),
                       jax.ShapeDtypeStruct((tok.n_tok, LANE), F32)],
            compiler_params=_params(1))(*args, wg_nat)
    per_seq = tok.seq_len // tm
    return pl.pallas_call(
        _qg_prompt_kernel, grid=(nt,),
        in_specs=in_specs + [_full_spec((N_KV * 16, D_MODEL))],
        out_specs=[pl.BlockSpec((1, N_HEADS, tm, HEAD_DIM), lambda i: (i // per_seq, 0, i % per_seq, 0)),
                   pl.BlockSpec((1, N_KV, 16, tm), lambda i: (i // per_seq, 0, 0, i % per_seq))],
        out_shape=[jax.ShapeDtypeStruct((tok.n_seq, N_HEADS, tok.seq_len, HEAD_DIM), BF16),
                   jax.ShapeDtypeStruct((tok.n_seq, N_KV, 16, tok.seq_len), F32)],
        compiler_params=_params(1))(*args, wg_t)


def _select_blocks(work, idx_f, axis):
    sel = jnp.zeros_like(work)
    for _ in range(N_SEL):
        mx = jnp.max(work, axis=axis, keepdims=True)
        first = jnp.min(jnp.where(work == mx, idx_f, 1e9), axis=axis, keepdims=True)
        chosen = idx_f == first
        ok = jnp.where(mx > NEG / 2, 1.0, 0.0)
        sel = jnp.maximum(sel, jnp.where(chosen, ok, 0.0))
        work = jnp.where(chosen, PICKED, work)
    return sel


def _online_update(s, vt, m_ref, l_ref, acc_ref, h):
    m_old = m_ref[h]
    m_new = jnp.maximum(m_old, jnp.max(s, axis=0, keepdims=True))
    alpha = jnp.exp(m_old - m_new)
    p = jnp.exp(s - m_new)
    l_ref[h] = alpha * l_ref[h] + jnp.sum(p, axis=0, keepdims=True)
    acc_ref[h] = alpha * acc_ref[h] + _dot(vt, p.astype(BF16))
    m_ref[h] = m_new


def _nsa_prompt_kernel(q_ref, g_ref, kc_ref, vct_ref, ks_ref, vst_ref, kw_ref, vwt_ref, cov_ref, slope_ref,
                       o_ref, sel_scr, m_scr, l_scr, acc_scr, bits_ref, *, ncp):
    i = pl.program_id(2)
    key_io = lax.broadcasted_iota(jnp.int32, (LANE, LANE), 0)
    tok_io = lax.broadcasted_iota(jnp.int32, (LANE, LANE), 1)
    rel = (tok_io - key_io).astype(F32)

    n_io = lax.broadcasted_iota(jnp.int32, (ncp, LANE), 0)
    t_io = lax.broadcasted_iota(jnp.int32, (ncp, LANE), 1)
    d_c = (t_io - CMP_STRIDE * n_io + (LANE * i - (CMP_LEN - 1))).astype(F32)
    valid_c = d_c >= 0.0
    kc = kc_ref[0, 0]
    vct = vct_ref[0, 0]
    psum = jnp.zeros((ncp, LANE), F32)
    o_cmp = []
    for h in range(HPG):
        s = _dot_nt(kc, q_ref[0, h]) - slope_ref[h] * d_c
        s = jnp.where(valid_c, s, NEG)
        m = jnp.max(s, axis=0, keepdims=True)
        p = jnp.where(valid_c, jnp.exp(s - m), 0.0)
        l = jnp.sum(p, axis=0, keepdims=True)
        inv = jnp.where(l > 0.0, 1.0 / l, 0.0)
        psum = psum + p * inv
        o_cmp.append(_dot(vct, p.astype(BF16)) * inv)

    p_hi, p_lo = _split_bf16(psum)
    cov = cov_ref[...]
    imp = _dot(cov, p_hi) + _dot(cov, p_lo)
    q_blk = 2 * i + tok_io // SEL_BLOCK
    forced = (key_io == 0) | (key_io == q_blk) | (key_io == q_blk - 1)
    work = jnp.where(key_io <= q_blk, imp + jnp.where(forced, FORCE_BONUS, 0.0), NEG)
    sel = _select_blocks(work, key_io.astype(F32), 0)
    sel_b = sel.astype(BF16)
    sel_scr[...] = sel_b

    cnt = _dot_nt(jnp.ones((8, LANE), BF16), sel_b)[0:1]
    lane = lax.broadcasted_iota(jnp.int32, (1, LANE), 1)
    for w in range(LANE // 32):
        bit = jnp.where((cnt > 0.0) & (lane // 32 == w), jnp.left_shift(1, lane % 32), 0)
        bits_ref[w] = jnp.sum(bit)

    def reset_state():
        m_scr[...] = jnp.full(m_scr.shape, M_FLOOR, F32)
        l_scr[...] = jnp.zeros_like(l_scr)
        acc_scr[...] = jnp.zeros_like(acc_scr)

    def finish(h):
        l = l_scr[h]
        return acc_scr[h] * jnp.where(l > 0.0, 1.0 / l, 0.0)

    reset_state()

    def pair_body(jp, carry):
        word = bits_ref[jp // 16]
        used = jnp.right_shift(word, (2 * jp) % 32) & 3

        @pl.when(used != 0)
        def _():
            expand = jnp.where(tok_io == 2 * jp + key_io // SEL_BLOCK, 1.0, 0.0).astype(BF16)
            chosen = _dot(expand, sel_scr[...])
            d = rel + (LANE * (i - jp)).astype(F32)
            bias = jnp.where(chosen > 0.5, jnp.where(d >= 0.0, 0.0, NEG), NEG)
            kt = ks_ref[0, 0, jp]
            vt = vst_ref[0, jp]
            for h in range(HPG):
                s = _dot_nt(kt, q_ref[0, h]) - slope_ref[h] * d + bias
                _online_update(s, vt, m_scr, l_scr, acc_scr, h)
        return carry

    lax.fori_loop(0, i + 1, pair_body, 0)
    o_sel = [finish(h) for h in range(HPG)]

    reset_state()
    n_wt = WINDOW // LANE
    for kt_i in range(n_wt + 1):
        jt = i - n_wt + kt_i
        jt_c = jnp.maximum(jt, 0)
        d = rel + float(LANE * (n_wt - kt_i))
        tile_bias = jnp.where(jt >= 0, 0.0, NEG).astype(F32)
        if kt_i == 0:
            bias = jnp.where(d <= float(WINDOW), 0.0, NEG) + tile_bias
        elif kt_i == n_wt:
            bias = jnp.where(d >= 0.0, 0.0, NEG) + tile_bias
        else:
            bias = tile_bias
        kt = kw_ref[0, 0, jt_c]
        vt = vwt_ref[0, jt_c]
        for h in range(HPG):
            s = _dot_nt(kt, q_ref[0, h]) - slope_ref[h] * d + bias
            _online_update(s, vt, m_scr, l_scr, acc_scr, h)

    for h in range(HPG):
        g_c = g_ref[0, 0, 3 * h:3 * h + 1, :]
        g_s = g_ref[0, 0, 3 * h + 1:3 * h + 2, :]
        g_w = g_ref[0, 0, 3 * h + 2:3 * h + 3, :]
        o_ref[0, h * HEAD_DIM:(h + 1) * HEAD_DIM, :] = g_c * o_cmp[h] + g_s * o_sel[h] + g_w * finish(h)


def _nsa_prompt(q, gates_t, kc, vct, k_nat, v_t, cov_t, slopes, n_seq, seq_len):
    n_qt = seq_len // LANE
    ncp = kc.shape[2]
    kern = functools.partial(_nsa_prompt_kernel, ncp=ncp)
    gd = N_KV * HEAD_DIM
    return pl.pallas_call(
        kern,
        grid=(n_seq, N_KV, n_qt),
        in_specs=[pl.BlockSpec((1, HPG, LANE, HEAD_DIM), lambda b, g, i: (b, g, i, 0)),
                  pl.BlockSpec((1, 1, 16, LANE), lambda b, g, i: (b, g, 0, i)),
                  pl.BlockSpec((1, 1, ncp, HEAD_DIM), lambda b, g, i: (b, g, 0, 0)),
                  pl.BlockSpec((1, 1, HEAD_DIM, ncp), lambda b, g, i: (b, g, 0, 0)),
                  pl.BlockSpec((1, 1, n_qt, LANE, HEAD_DIM), lambda b, g, i: (b, g, 0, 0, 0)),
                  pl.BlockSpec((1, n_qt, HEAD_DIM, LANE), lambda b, g, i: (b, 0, g, 0)),
                  pl.BlockSpec((1, 1, n_qt, LANE, HEAD_DIM), lambda b, g, i: (b, N_KV + g, 0, 0, 0)),
                  pl.BlockSpec((1, n_qt, HEAD_DIM, LANE), lambda b, g, i: (b, 0, N_KV + g, 0)),
                  pl.BlockSpec((LANE, ncp), lambda b, g, i: (0, 0)),
                  pl.BlockSpec((HPG, 1, LANE), lambda b, g, i: (g, 0, 0))],
        out_specs=pl.BlockSpec((1, HPG * HEAD_DIM, LANE), lambda b, g, i: (b, g, i)),
        out_shape=jax.ShapeDtypeStruct((n_seq, N_HEADS * HEAD_DIM, seq_len), F32),
        scratch_shapes=[pltpu.VMEM((LANE, LANE), BF16),
                        pltpu.VMEM((HPG, 1, LANE), F32),
                        pltpu.VMEM((HPG, 1, LANE), F32),
                        pltpu.VMEM((HPG, HEAD_DIM, LANE), F32),
                        pltpu.SMEM((LANE // 32,), jnp.int32)],
        compiler_params=_params(3),
    )(q, gates_t, kc, vct, k_nat, v_t, k_nat, v_t, cov_t, slopes)


def _nsa_sample_kernel(tbl_ref, *refs, n_pages):
    pages = refs[:n_pages]
    (q_ref, g_ref, kc_ref, vct_ref, win_ref, new_ref, cov_ref, exp_ref, slope_ref,
     o_ref, s_scr) = refs[n_pages:]
    del tbl_ref
    gd = N_KV * HEAD_DIM
    past = n_pages * PAGE
    q_pos = float(past)
    qb = q_ref[0].astype(BF16)
    slope = slope_ref[...]
    row_grp = lax.broadcasted_iota(jnp.int32, (N_HEADS, LANE), 0) // HPG
    lane_f = lax.broadcasted_iota(jnp.int32, (N_HEADS, LANE), 1).astype(F32)
    new = new_ref[0]
    first_row = lax.broadcasted_iota(jnp.int32, (LANE, HEAD_DIM), 0) == 0

    def own_rows(g_sel, pick):
        out = pick(0)
        for grp in range(1, N_KV):
            out = jnp.where(g_sel == grp, pick(grp), out)
        return out

    def new_tile(slot, grp):
        col = slot * gd + grp * HEAD_DIM
        return jnp.where(first_row, new[:, col:col + HEAD_DIM], 0.0).astype(BF16)

    def softmax_rows(s, valid):
        s = jnp.where(valid, s, NEG)
        m = jnp.max(s, axis=1, keepdims=True)
        p = jnp.where(valid, jnp.exp(s - m), 0.0)
        l = jnp.sum(p, axis=1, keepdims=True)
        return p * jnp.where(l > 0.0, 1.0 / l, 0.0)

    d_c = q_pos - (CMP_STRIDE * lane_f + (CMP_LEN - 1))
    s_c = own_rows(row_grp, lambda grp: _dot_nt(qb, kc_ref[0, grp])) - slope * d_c
    p_c = softmax_rows(s_c, d_c >= 0.0)
    p_cb = p_c.astype(BF16)
    row_grp_o = lax.broadcasted_iota(jnp.int32, (N_HEADS, HEAD_DIM), 0) // HPG
    o_c = own_rows(row_grp_o, lambda grp: _dot_nt(p_cb, vct_ref[0, grp]))

    p_grp = own_rows(row_grp, lambda grp: jnp.broadcast_to(
        jnp.sum(p_c[grp * HPG:(grp + 1) * HPG], axis=0, keepdims=True), (N_HEADS, LANE)))
    p_hi, p_lo = _split_bf16(p_grp)
    cov = cov_ref[...]
    imp = _dot(p_hi, cov) + _dot(p_lo, cov)
    q_blk = float(past // SEL_BLOCK)
    forced = (lane_f == 0.0) | (lane_f == q_blk) | (lane_f == q_blk - 1.0)
    work = jnp.where(lane_f <= q_blk, imp + jnp.where(forced, FORCE_BONUS, 0.0), NEG)
    sel = _select_blocks(work, lane_f, 1)

    n_t = n_pages + 1
    for t in range(n_t):
        if t < n_pages:
            tile = lambda grp, t=t: pages[t][0, :, grp * HEAD_DIM:(grp + 1) * HEAD_DIM].astype(BF16)
        else:
            tile = lambda grp: new_tile(2, grp)
        s_scr[:, t * LANE:(t + 1) * LANE] = own_rows(row_grp, lambda grp: _dot_nt(qb, tile(grp)))
    width = n_t * LANE
    kpos = lax.broadcasted_iota(jnp.int32, (N_HEADS, width), 1).astype(F32)
    d_s = q_pos - kpos
    chosen = _dot(sel.astype(BF16), exp_ref[...])
    slope_w = jnp.concatenate([slope] * n_t, axis=1)
    p_s = softmax_rows(s_scr[...] - slope_w * d_s, (chosen > 0.5) & (d_s >= 0.0)).astype(BF16)
    o_s = jnp.zeros((N_HEADS, HEAD_DIM), F32)
    for t in range(n_t):
        pt = p_s[:, t * LANE:(t + 1) * LANE]
        if t < n_pages:
            vtile = lambda grp, t=t: pages[t][0, :, gd + grp * HEAD_DIM: gd + (grp + 1) * HEAD_DIM].astype(BF16)
        else:
            vtile = lambda grp: new_tile(3, grp)
        o_s = o_s + own_rows(row_grp_o, lambda grp: _dot(pt, vtile(grp)))

    buf = win_ref.shape[1]
    n_w = buf // LANE + 1
    w_parts = []
    for t in range(n_w):
        if t < n_w - 1:
            tile = lambda grp, t=t: win_ref[0, t * LANE:(t + 1) * LANE,
                                            grp * HEAD_DIM:(grp + 1) * HEAD_DIM].astype(BF16)
        else:
            tile = lambda grp: new_tile(4, grp)
        w_parts.append(own_rows(row_grp, lambda grp: _dot_nt(qb, tile(grp))))
    s_w = jnp.concatenate(w_parts, axis=1)
    wpos = lax.broadcasted_iota(jnp.int32, (N_HEADS, n_w * LANE), 1).astype(F32) + float(past - buf)
    d_w = q_pos - wpos
    slope_ww = jnp.concatenate([slope] * n_w, axis=1)
    p_w = softmax_rows(s_w - slope_ww * d_w, (d_w >= 0.0) & (d_w <= float(WINDOW))).astype(BF16)
    o_w = jnp.zeros((N_HEADS, HEAD_DIM), F32)
    for t in range(n_w):
        pt = p_w[:, t * LANE:(t + 1) * LANE]
        if t < n_w - 1:
            vtile = lambda grp, t=t: win_ref[0, t * LANE:(t + 1) * LANE,
                                             gd + grp * HEAD_DIM: gd + (grp + 1) * HEAD_DIM].astype(BF16)
        else:
            vtile = lambda grp: new_tile(5, grp)
        o_w = o_w + own_rows(row_grp_o, lambda grp: _dot(pt, vtile(grp)))

    gt = g_ref[0]
    o_ref[0] = gt[:, 0:1] * o_c + gt[:, 1:2] * o_s + gt[:, 2:3] * o_w


def _nsa_sample(q, gates, kc, vct, cache_pages, page_table, win, kv_new, cov, expand, slope16):
    n_seq, n_pages = page_table.shape
    gd = N_KV * HEAD_DIM
    buf = win.shape[1]
    kern = functools.partial(_nsa_sample_kernel, n_pages=n_pages)

    def page_spec(p):
        return pl.BlockSpec((1, PAGE, 2 * gd), lambda b, tbl, p=p: (tbl[b, p], 0, 1))

    def const_spec(shape):
        nd = len(shape)
        return pl.BlockSpec(shape, lambda b, tbl: (0,) * nd)

    grid_spec = pltpu.PrefetchScalarGridSpec(
        num_scalar_prefetch=1,
        grid=(n_seq,),
        in_specs=[page_spec(p) for p in range(n_pages)] + [
            pl.BlockSpec((1, N_HEADS, HEAD_DIM), lambda b, tbl: (b, 0, 0)),
            pl.BlockSpec((1, N_HEADS, 3), lambda b, tbl: (b, 0, 0)),
            pl.BlockSpec((1, N_KV, LANE, HEAD_DIM), lambda b, tbl: (b, 0, 0, 0)),
            pl.BlockSpec((1, N_KV, HEAD_DIM, LANE), lambda b, tbl: (b, 0, 0, 0)),
            pl.BlockSpec((1, buf, 2 * gd), lambda b, tbl: (b, 0, 0)),
            pl.BlockSpec((1, 1, N_KV_PROJ * gd), lambda b, tbl: (b, 0, 0)),
            const_spec(cov.shape), const_spec(expand.shape), const_spec(slope16.shape)],
        out_specs=pl.BlockSpec((1, N_HEADS, HEAD_DIM), lambda b, tbl: (b, 0, 0)),
        scratch_shapes=[pltpu.VMEM((N_HEADS, (n_pages + 1) * LANE), F32)])
    return pl.pallas_call(
        kern,
        grid_spec=grid_spec,
        out_shape=jax.ShapeDtypeStruct((n_seq, N_HEADS, HEAD_DIM), F32),
        compiler_params=_params(1),
    )(page_table, *([cache_pages] * n_pages), q, gates, kc, vct, win, kv_new, cov, expand, slope16)


def _oproj_prompt_kernel(ot_ref, x_ref, gt_ref, wo_ref, o_ref):
    o = ot_ref[0].T.astype(BF16)
    o_ref[...] = x_ref[...] + gt_ref[0] * _dot(o, wo_ref[...])


def _oproj_sample_kernel(a_ref, x_ref, gt_ref, wo_ref, o_ref):
    o_ref[...] = x_ref[...] + gt_ref[0] * _dot(a_ref[...].astype(BF16), wo_ref[...])


def _out_proj(tok, attn, x, gate, wo):
    tm, nt = tok.tiles(512)
    if tok.per_row:
        kern, a_spec = _oproj_sample_kernel, tok.x_spec(tm)
    else:
        per_seq = tok.seq_len // tm
        kern = _oproj_prompt_kernel
        a_spec = pl.BlockSpec((1, D_MODEL, tm), lambda i: (i // per_seq, 0, i % per_seq))
    return pl.pallas_call(
        kern, grid=(nt,),
        in_specs=[a_spec, tok.x_spec(tm), tok.mod_spec(tm), _full_spec((D_MODEL, D_MODEL))],
        out_specs=tok.x_spec(tm),
        out_shape=jax.ShapeDtypeStruct((tok.n_tok, D_MODEL), F32),
        compiler_params=_params(1))(attn, x, tok.mod_arr(gate), wo)


def _final_kernel(x_ref, sh_ref, sc_ref, g_ref, o_ref):
    o_ref[...] = _ada_norm(x_ref[...], g_ref[...], sh_ref[0], sc_ref[0])


def _final(tok, x, shift, scale, g):
    tm, nt = tok.tiles(512)
    return pl.pallas_call(
        _final_kernel, grid=(nt,),
        in_specs=[tok.x_spec(tm), tok.mod_spec(tm), tok.mod_spec(tm), _full_spec((1, D_MODEL))],
        out_specs=tok.x_spec(tm),
        out_shape=jax.ShapeDtypeStruct((tok.n_tok, D_MODEL), F32),
        compiler_params=_params(1))(x, tok.mod_arr(shift), tok.mod_arr(scale), g.reshape(1, D_MODEL))


def _alibi_slopes():
    h = jnp.arange(1, N_HEADS + 1, dtype=F32)
    return jnp.exp2(-8.0 * h / N_HEADS)


def _cover(n_cmp, n_sel):
    c_start = jnp.arange(n_cmp)[:, None] * CMP_STRIDE
    s_start = jnp.arange(n_sel)[None, :] * SEL_BLOCK
    return ((c_start < s_start + SEL_BLOCK) & (c_start + CMP_LEN > s_start)).astype(BF16)


def _trunk(tok, x, mods, kv_mod, f_mod, wts, ctx):
    depth = wts['norm_g'].shape[0]
    n_a = depth // 2
    v_rows = []
    kv = None
    attn_ctx = None
    for l in range(depth):
        m = mods[l]
        if l == n_a:
            kv_out = _kv_proj(tok, x, kv_mod[:, 0], kv_mod[:, 1], wts['kv_norm_g'], wts['kv_w'], wts['kv_wvt'])
            kv, attn_ctx = ctx['prepare'](kv_out)
        x = _ffn(tok, x, m[:, 0], m[:, 1], m[:, 2], wts['norm_g'][l, 0],
                 wts['ffn_w_gate'][l, 0], wts['ffn_w_up'][l, 0], wts['ffn_w_down'][l, 0])
        if l < n_a:
            x, v = _gmlp(tok, x, m[:, 3], m[:, 4], m[:, 5], wts['norm_g'][l, 1], wts['gmlp_w_uv'][l],
                         wts['gmlp_ln_g'][l], wts['gmlp_ln_b'][l], wts['gmlp_w_sp'][l], wts['gmlp_b_sp'][l],
                         wts['gmlp_w_out'][l])
            v_rows.append(v)
        else:
            j = l - n_a
            q, gates = _qg_proj(tok, x, m[:, 3], m[:, 4], wts['norm_g'][l, 1], wts['nsa_wq'][j],
                                wts['nsa_wg_t'][j], wts['nsa_wg'][j])
            attn = ctx['attend'](q, gates, attn_ctx)
            x = _out_proj(tok, attn, x, m[:, 5], wts['nsa_w_o'][j])
        x = _ffn(tok, x, m[:, 6], m[:, 7], m[:, 8], wts['norm_g'][l, 2],
                 wts['ffn_w_gate'][l, 1], wts['ffn_w_up'][l, 1], wts['ffn_w_down'][l, 1])
    y = _final(tok, x, f_mod[:, 0], f_mod[:, 1], wts['final_g'])
    return y, kv, v_rows


def kernel(x_prompt, x_sample, cache_kv, state_win_kv, page_table, c_prompt, c_sample, ada_w, ada_b, norm_g, ffn_w_gate, ffn_w_up, ffn_w_down, gmlp_w_uv, gmlp_ln_g, gmlp_ln_b, gmlp_w_sp, gmlp_b_sp, gmlp_w_out, nsa_w_qg, nsa_w_o, kv_norm_g, kv_ada_w, kv_ada_b, kv_w, cmp_w1, cmp_w2, cmp_pe, final_g, final_ada_w, final_ada_b):
    n_p, seq, _ = x_prompt.shape
    n_s, dec_seq, _ = x_sample.shape
    assert dec_seq == 1 and seq % (PAGES_PER_SEG * PAGE) == 0
    depth = ada_w.shape[0]
    n_b = nsa_w_qg.shape[0]
    gd = N_KV * HEAD_DIM
    n_pages = page_table.shape[1]
    assert n_pages == PAGES_PER_SEG
    past = n_pages * PAGE

    nq = N_HEADS * HEAD_DIM
    wg_cols = nsa_w_qg[:, :, nq:]
    wg_pad = jnp.pad(wg_cols.reshape(n_b, D_MODEL, N_KV, HPG * 3), ((0, 0), (0, 0), (0, 0), (0, 16 - HPG * 3)))
    wts = dict(
        norm_g=norm_g, kv_norm_g=kv_norm_g, final_g=final_g,
        ffn_w_gate=ffn_w_gate.astype(BF16), ffn_w_up=ffn_w_up.astype(BF16), ffn_w_down=ffn_w_down.astype(BF16),
        gmlp_w_uv=gmlp_w_uv.astype(BF16), gmlp_ln_g=gmlp_ln_g, gmlp_ln_b=gmlp_ln_b,
        gmlp_w_sp=gmlp_w_sp, gmlp_b_sp=gmlp_b_sp, gmlp_w_out=gmlp_w_out.astype(BF16),
        nsa_wq=nsa_w_qg[:, :, :nq].astype(BF16),
        nsa_wg_t=jnp.swapaxes(wg_pad.reshape(n_b, D_MODEL, N_KV * 16), 1, 2).astype(BF16),
        nsa_wg=jnp.pad(wg_cols, ((0, 0), (0, 0), (0, LANE - N_HEADS * 3))).astype(BF16),
        nsa_w_o=nsa_w_o.astype(BF16),
        kv_w=kv_w.astype(BF16),
        kv_wvt=jnp.concatenate([kv_w[:, 3 * gd:4 * gd], kv_w[:, 5 * gd:6 * gd]], axis=1).T.astype(BF16),
    )
    w1 = cmp_w1.astype(BF16).reshape(2, CMP_LEN, HEAD_DIM, CMP_HID)
    w2 = cmp_w2.astype(BF16)
    w2t = jnp.swapaxes(cmp_w2, 1, 2).astype(BF16)
    pe = jnp.broadcast_to(cmp_pe.astype(BF16)[:, :, None, :], (2, CMP_LEN, 16, HEAD_DIM))
    slopes = _alibi_slopes()

    n_c = n_p + n_s
    c_all = jnp.pad(jnp.concatenate([c_prompt, c_sample], axis=0), ((0, (-n_c) % 8), (0, 0)))
    mod_all = _mod_linear(c_all, ada_w, ada_b)
    kv_mod_all = _mod_linear(c_all, kv_ada_w[None], kv_ada_b[None])[0]
    f_mod_all = _mod_linear(c_all, final_ada_w[None], final_ada_b[None])[0]

    def rows(a, lo, hi, k):
        return a[..., lo:hi, :].reshape(a.shape[:-2] + (hi - lo, k, D_MODEL))

    tok_p = _Tok(n_p, seq, per_row=False)
    n_seg = seq // (PAGES_PER_SEG * PAGE)
    n_qt = seq // LANE
    ncp = n_seg * PAGES_PER_SEG * 8
    n_sel_p = seq // SEL_BLOCK
    assert n_sel_p <= LANE
    cov_t = jnp.pad(_cover(ncp, n_sel_p).T, ((0, LANE - n_sel_p), (0, 0)))
    slopes_p = jnp.broadcast_to(slopes[:, None, None], (N_HEADS, 1, LANE))

    def prepare_p(kv_out):
        kv, k_nat, v_t = kv_out
        pages_per_seq = seq // PAGE
        base = jnp.arange(n_p * n_seg, dtype=jnp.int32)[:, None] * PAGES_PER_SEG
        table = jnp.minimum(base + jnp.arange(PAGES_PER_SEG + 1, dtype=jnp.int32)[None, :],
                            n_p * pages_per_seq - 1)
        kc, vct = _compress(kv.reshape(n_p * pages_per_seq, PAGE, N_KV_PROJ * gd), table, w1, w2, w2t, pe,
                            n_p, n_seg)
        return kv, (kc, vct, k_nat, v_t)

    def attend_p(q, gates, c):
        kc, vct, k_nat, v_t = c
        return _nsa_prompt(q, gates, kc, vct, k_nat, v_t, cov_t, slopes_p, n_p, seq)

    y_p, kv_p, _ = _trunk(tok_p, x_prompt.reshape(n_p * seq, D_MODEL),
                          rows(mod_all, 0, n_p, 9), rows(kv_mod_all, 0, n_p, 2), rows(f_mod_all, 0, n_p, 2),
                          wts, dict(prepare=prepare_p, attend=attend_p))
    kv_p = kv_p.reshape(n_p, seq, N_KV_PROJ, N_KV, HEAD_DIM)
    kv_prompt = kv_p[:, :, :4]
    win_prompt = kv_p[:, seq - min(WINDOW, seq):, 4:6]

    tok_s = _Tok(n_s, 1, per_row=True)
    cache_pages = cache_kv.reshape(cache_kv.shape[0], PAGE, 4 * gd)
    buf = state_win_kv.shape[1]
    win_flat = state_win_kv.reshape(n_s, buf, 2 * gd)
    n_sel_s = (past + 1 + SEL_BLOCK - 1) // SEL_BLOCK
    cov_s = jnp.pad(_cover(LANE, n_sel_s), ((0, 0), (0, LANE - n_sel_s)))
    n_keys = (n_pages + 1) * LANE
    expand_s = (jnp.arange(LANE)[:, None] == (jnp.arange(n_keys)[None, :] // SEL_BLOCK)).astype(BF16)
    slope16 = jnp.broadcast_to(slopes[:, None], (N_HEADS, LANE))

    def prepare_s(kv):
        table = jnp.concatenate([page_table, page_table[:, -1:]], axis=1)
        kc, vct = _compress(cache_pages, table, w1, w2, w2t, pe, n_s, 1)
        return kv, (kc, vct, kv)

    def attend_s(q, gates, c):
        kc, vct, kv = c
        o = _nsa_sample(q.reshape(n_s, N_HEADS, HEAD_DIM), gates[:, :N_HEADS * 3].reshape(n_s, N_HEADS, 3),
                        kc, vct, cache_pages, page_table, win_flat, kv.reshape(n_s, 1, N_KV_PROJ * gd),
                        cov_s, expand_s, slope16)
        return o.reshape(n_s, N_HEADS * HEAD_DIM)

    y_s, kv_s, v_rows = _trunk(tok_s, x_sample.reshape(n_s, D_MODEL),
                               rows(mod_all, n_p, n_c, 9), rows(kv_mod_all, n_p, n_c, 2),
                               rows(f_mod_all, n_p, n_c, 2), wts, dict(prepare=prepare_s, attend=attend_s))
    kv_s = kv_s.reshape(n_s, 1, N_KV_PROJ, N_KV, HEAD_DIM)
    kv_sample = kv_s[:, :, :4]
    win_sample = jnp.concatenate([state_win_kv, kv_s[:, :, 4:6]], axis=1)[:, 1:]
    gmlp_v_sample = jnp.stack([v.reshape(n_s, 1, D_V) for v in v_rows])

    return (y_p.reshape(n_p, seq, D_MODEL), y_s.reshape(n_s, 1, D_MODEL), kv_prompt, kv_sample,
            win_prompt, win_sample, gmlp_v_sample)
```

```python
import functools

import jax
import jax.numpy as jnp
from jax import lax
from jax.experimental import pallas as pl
from jax.experimental.pallas import tpu as pltpu

F32 = jnp.float32
BF16 = jnp.bfloat16

D_MODEL = 1024
D_FF = 2816
D_V = 3072
CHUNK = 128
N_GROUPS_A = 8
CG = D_V // N_GROUPS_A
N_HEADS = 16
HEAD_DIM = 64
N_KV = 4
HPG = N_HEADS // N_KV
CMP_LEN = 32
CMP_STRIDE = 16
CMP_HID = 256
SEL_BLOCK = 64
N_SEL = 16
WINDOW = 512
PAGE = 128
N_KV_PROJ = 6
EPS = 1e-6
NEG = -1e30
M_FLOOR = -1e29
PICKED = -3e38
FORCE_BONUS = 1e4
LOG2E = 1.4426950408889634
MASK_OFF = 1e30

LANE = 128
VMEM_LIMIT = 56 * 1024 * 1024
FF_CHUNK = 256
UV_CHUNK = 512
PAGES_PER_SEG = 16
XROWS = PAGES_PER_SEG * 8 + 8


def _dot(a, b):
    return jnp.dot(a, b, preferred_element_type=F32)


def _dot_nt(a, b):
    return lax.dot_general(a, b, (((1,), (1,)), ((), ())), preferred_element_type=F32)


def _split_bf16(x):
    hi = x.astype(BF16)
    lo = (x - hi.astype(F32)).astype(BF16)
    return hi, lo


def _params(n_grid):
    return pltpu.CompilerParams(dimension_semantics=("arbitrary",) * n_grid,
                                vmem_limit_bytes=VMEM_LIMIT)


def _full_spec(shape):
    nd = len(shape)
    return pl.BlockSpec(shape, lambda *_: (0,) * nd, pipeline_mode=pl.Buffered(1))


def _ada_norm(x, g, shift, scale):
    ms = jnp.mean(x * x, axis=-1, keepdims=True)
    h = x * lax.rsqrt(ms + EPS) * g
    return h * (1.0 + scale) + shift


def _mod_kernel(c_ref, w_ref, b_ref, o_ref):
    c = c_ref[...]
    a = c * jax.nn.sigmoid(c)
    w = w_ref[0]
    a_hi, a_lo = _split_bf16(a)
    w_hi, w_lo = _split_bf16(w)
    o_ref[0] = _dot(a_hi, w_hi) + _dot(a_hi, w_lo) + _dot(a_lo, w_hi) + b_ref[0]


def _mod_linear(c, w, b):
    n_l, d, n = w.shape
    m = c.shape[0]
    tn = 1024
    return pl.pallas_call(
        _mod_kernel,
        grid=(n_l, n // tn),
        in_specs=[pl.BlockSpec((m, d), lambda l, j: (0, 0)),
                  pl.BlockSpec((1, d, tn), lambda l, j: (l, 0, j)),
                  pl.BlockSpec((1, 1, tn), lambda l, j: (l, 0, j))],
        out_specs=pl.BlockSpec((1, m, tn), lambda l, j: (l, 0, j)),
        out_shape=jax.ShapeDtypeStruct((n_l, m, n), F32),
        compiler_params=_params(2),
    )(c, w, b.reshape(n_l, 1, n))


class _Tok:
    def __init__(self, n_seq, seq_len, per_row):
        self.n_seq, self.seq_len = n_seq, seq_len
        self.n_tok = n_seq * seq_len
        self.per_row = per_row

    def tiles(self, tm):
        if self.per_row:
            return self.n_tok, 1
        assert self.seq_len % tm == 0
        return tm, self.n_tok // tm

    def x_spec(self, tm, width=D_MODEL):
        return pl.BlockSpec((tm, width), lambda i: (i, 0))

    def mod_spec(self, tm):
        if self.per_row:
            return pl.BlockSpec((1, tm, D_MODEL), lambda i: (0, 0, 0))
        per_seq = self.seq_len // tm
        return pl.BlockSpec((1, 1, D_MODEL), lambda i: (i // per_seq, 0, 0))

    def mod_arr(self, m):
        if self.per_row:
            return m.reshape(1, self.n_tok, D_MODEL)
        return m.reshape(self.n_seq, 1, D_MODEL)


def _ffn_kernel(x_ref, sh_ref, sc_ref, gt_ref, g_ref, wg_ref, wu_ref, wd_ref, o_ref, a_scr):
    x = x_ref[...]
    hb = _ada_norm(x, g_ref[...], sh_ref[0], sc_ref[0]).astype(BF16)
    for c in range(D_FF // FF_CHUNK):
        cs = slice(c * FF_CHUNK, (c + 1) * FF_CHUNK)
        gate = _dot(hb, wg_ref[:, cs])
        up = _dot(hb, wu_ref[:, cs])
        a_scr[:, cs] = (gate * jax.nn.sigmoid(gate) * up).astype(BF16)
    y = _dot(a_scr[...], wd_ref[...])
    o_ref[...] = x + (0.5 * gt_ref[0]) * y


def _ffn(tok, x, shift, scale, gate, g, wg, wu, wd):
    tm, nt = tok.tiles(512)
    return pl.pallas_call(
        _ffn_kernel,
        grid=(nt,),
        in_specs=[tok.x_spec(tm), tok.mod_spec(tm), tok.mod_spec(tm), tok.mod_spec(tm),
                  _full_spec((1, D_MODEL)), _full_spec((D_MODEL, D_FF)),
                  _full_spec((D_MODEL, D_FF)), _full_spec((D_FF, D_MODEL))],
        out_specs=tok.x_spec(tm),
        out_shape=jax.ShapeDtypeStruct((tok.n_tok, D_MODEL), F32),
        scratch_shapes=[pltpu.VMEM((tm, D_FF), BF16)],
        compiler_params=_params(1),
    )(x, tok.mod_arr(shift), tok.mod_arr(scale), tok.mod_arr(gate), g.reshape(1, D_MODEL), wg, wu, wd)


def _gmlp_uv(x_ref, sh_ref, sc_ref, g_ref, wuv_ref, lng_ref, lnb_ref, u_scr, v_scr):
    hb = _ada_norm(x_ref[...], g_ref[...], sh_ref[0], sc_ref[0]).astype(BF16)
    for c in range(D_V // UV_CHUNK):
        cs = slice(c * UV_CHUNK, (c + 1) * UV_CHUNK)
        cv = slice(D_V + c * UV_CHUNK, D_V + (c + 1) * UV_CHUNK)
        u_scr[:, cs] = jax.nn.gelu(_dot(hb, wuv_ref[:, cs]))
        v_scr[:, cs] = jax.nn.gelu(_dot(hb, wuv_ref[:, cv]))
    v = v_scr[...]
    mu = jnp.mean(v, axis=-1, keepdims=True)
    vc = v - mu
    var = jnp.mean(vc * vc, axis=-1, keepdims=True)
    return vc * lax.rsqrt(var + EPS) * lng_ref[...] + lnb_ref[...]


def _gmlp_prompt_kernel(x_ref, sh_ref, sc_ref, gt_ref, g_ref, wuv_ref, lng_ref, lnb_ref,
                        wsp_ref, bsp_ref, wout_ref, o_ref, u_scr, v_scr, a_scr):
    v_scr[...] = _gmlp_uv(x_ref, sh_ref, sc_ref, g_ref, wuv_ref, lng_ref, lnb_ref, u_scr, v_scr)
    row = lax.broadcasted_iota(jnp.int32, (CHUNK, CHUNK), 0)
    col = lax.broadcasted_iota(jnp.int32, (CHUNK, CHUNK), 1)
    tm = x_ref.shape[0]
    for grp in range(N_GROUPS_A):
        w = jnp.where(row >= col, wsp_ref[grp], 0.0).astype(BF16)
        cs = slice(grp * CG, (grp + 1) * CG)
        for n in range(tm // CHUNK):
            rs = slice(n * CHUNK, (n + 1) * CHUNK)
            s = _dot(w, v_scr[rs, cs].astype(BF16)) + bsp_ref[:, cs]
            a_scr[rs, cs] = (u_scr[rs, cs] * s).astype(BF16)
    y = _dot(a_scr[...], wout_ref[...])
    o_ref[...] = x_ref[...] + gt_ref[0] * y


def _gmlp_sample_kernel(x_ref, sh_ref, sc_ref, gt_ref, g_ref, wuv_ref, lng_ref, lnb_ref,
                        wrow_ref, brow_ref, wout_ref, o_ref, vout_ref, u_scr, v_scr):
    vn = _gmlp_uv(x_ref, sh_ref, sc_ref, g_ref, wuv_ref, lng_ref, lnb_ref, u_scr, v_scr)
    vout_ref[...] = vn
    s = vn * wrow_ref[...] + brow_ref[...]
    y = _dot((u_scr[...] * s).astype(BF16), wout_ref[...])
    o_ref[...] = x_ref[...] + gt_ref[0] * y


def _gmlp(tok, x, shift, scale, gate, g, wuv, ln_g, ln_b, w_sp, b_sp, wout):
    tm, nt = tok.tiles(256)
    common = [tok.x_spec(tm), tok.mod_spec(tm), tok.mod_spec(tm), tok.mod_spec(tm),
              _full_spec((1, D_MODEL)), _full_spec((D_MODEL, 2 * D_V)),
              _full_spec((1, D_V)), _full_spec((1, D_V))]
    args = [x, tok.mod_arr(shift), tok.mod_arr(scale), tok.mod_arr(gate), g.reshape(1, D_MODEL), wuv,
            ln_g.reshape(1, D_V), ln_b.reshape(1, D_V)]
    x_shape = jax.ShapeDtypeStruct((tok.n_tok, D_MODEL), F32)
    if tok.per_row:
        wrow = jnp.repeat(w_sp[:, 0, 0], CG).reshape(1, D_V)
        brow = jnp.repeat(b_sp[:, 0], CG).reshape(1, D_V)
        return pl.pallas_call(
            _gmlp_sample_kernel,
            grid=(nt,),
            in_specs=common + [_full_spec((1, D_V)), _full_spec((1, D_V)), _full_spec((D_V, D_MODEL))],
            out_specs=[tok.x_spec(tm), tok.x_spec(tm, D_V)],
            out_shape=[x_shape, jax.ShapeDtypeStruct((tok.n_tok, D_V), F32)],
            scratch_shapes=[pltpu.VMEM((tm, D_V), F32), pltpu.VMEM((tm, D_V), F32)],
            compiler_params=_params(1),
        )(*args, wrow, brow, wout)
    bias = jnp.repeat(b_sp.T, CG, axis=1)
    out = pl.pallas_call(
        _gmlp_prompt_kernel,
        grid=(nt,),
        in_specs=common + [_full_spec((N_GROUPS_A, CHUNK, CHUNK)), _full_spec((CHUNK, D_V)),
                           _full_spec((D_V, D_MODEL))],
        out_specs=tok.x_spec(tm),
        out_shape=x_shape,
        scratch_shapes=[pltpu.VMEM((tm, D_V), F32), pltpu.VMEM((tm, D_V), F32),
                        pltpu.VMEM((tm, D_V), BF16)],
        compiler_params=_params(1),
    )(*args, w_sp, bias, wout)
    return out, None


def _kv_prompt_kernel(x_ref, sh_ref, sc_ref, g_ref, w_ref, wvt_ref, ak_ref, kv_ref, k_ref, vt_ref):
    hb = _ada_norm(x_ref[...], g_ref[...], sh_ref[0], sc_ref[0]).astype(BF16)
    kv = _dot(hb, w_ref[...])
    kv_ref[...] = kv
    vt = _dot_nt(wvt_ref[...], hb).astype(BF16)
    tm = x_ref.shape[0]
    gd = N_KV * HEAD_DIM
    for c in range(tm // LANE):
        rs = slice(c * LANE, (c + 1) * LANE)
        vt_ref[0, c] = vt[:, rs]
        for j, slot in enumerate((2, 4)):
            for grp in range(N_KV):
                col = slot * gd + grp * HEAD_DIM
                k_ref[0, j * N_KV + grp, c, :, 0:HEAD_DIM] = kv[rs, col:col + HEAD_DIM].astype(BF16)
                k_ref[0, j * N_KV + grp, c, :, HEAD_DIM:LANE] = ak_ref[...]


def _kv_sample_kernel(x_ref, sh_ref, sc_ref, g_ref, w_ref, kv_ref):
    hb = _ada_norm(x_ref[...], g_ref[...], sh_ref[0], sc_ref[0]).astype(BF16)
    kv_ref[...] = _dot(hb, w_ref[...])


def _kv_proj(tok, x, shift, scale, g, w, wvt, alibi_k):
    tm, nt = tok.tiles(512)
    n_kv = N_KV_PROJ * N_KV * HEAD_DIM
    in_specs = [tok.x_spec(tm), tok.mod_spec(tm), tok.mod_spec(tm), _full_spec((1, D_MODEL)),
                _full_spec((D_MODEL, n_kv))]
    args = [x, tok.mod_arr(shift), tok.mod_arr(scale), g.reshape(1, D_MODEL), w]
    kv_shape = jax.ShapeDtypeStruct((tok.n_tok, n_kv), F32)
    if tok.per_row:
        return pl.pallas_call(
            _kv_sample_kernel, grid=(nt,), in_specs=in_specs, out_specs=tok.x_spec(tm, n_kv),
            out_shape=kv_shape, compiler_params=_params(1))(*args)
    per_seq = tok.seq_len // tm
    n_qt = tok.seq_len // LANE
    sub = tm // LANE
    return pl.pallas_call(
        _kv_prompt_kernel,
        grid=(nt,),
        in_specs=in_specs + [_full_spec((2 * N_KV * HEAD_DIM, D_MODEL)), _full_spec((LANE, HEAD_DIM))],
        out_specs=[tok.x_spec(tm, n_kv),
                   pl.BlockSpec((1, 2 * N_KV, sub, LANE, LANE),
                                lambda i: (i // per_seq, 0, i % per_seq, 0, 0)),
                   pl.BlockSpec((1, sub, 2 * N_KV * HEAD_DIM, LANE),
                                lambda i: (i // per_seq, i % per_seq, 0, 0))],
        out_shape=[kv_shape,
                   jax.ShapeDtypeStruct((tok.n_seq, 2 * N_KV, n_qt, LANE, LANE), BF16),
                   jax.ShapeDtypeStruct((tok.n_seq, n_qt, 2 * N_KV * HEAD_DIM, LANE), BF16)],
        compiler_params=_params(1),
    )(*args, wvt, alibi_k)


def _compress_kernel(tbl_ref, *refs):
    pages = refs[:PAGES_PER_SEG + 1]
    perm_ref, w1_ref, w2_ref, w2t_ref, pe_ref, akc_ref, kc_ref, vct_ref, xs_scr, acca, accb = refs[PAGES_PER_SEG + 1:]
    del tbl_ref
    perm = perm_ref[...]
    for p in range(PAGES_PER_SEG + 1):
        xp = _dot(perm, pages[p][0].astype(BF16))
        for s in range(CMP_STRIDE):
            xs_scr[s, p * 8:(p + 1) * 8, :] = xp[s * 8:(s + 1) * 8, :]
    gd = N_KV * HEAD_DIM
    n_blk = PAGES_PER_SEG * 8
    for slot in range(2):
        acca[...] = jnp.zeros_like(acca)
        accb[...] = jnp.zeros_like(accb)

        def body(s, pacc, slot=slot):
            rows = [xs_scr[s, :, slot * gd + grp * HEAD_DIM: slot * gd + (grp + 1) * HEAD_DIM]
                    for grp in range(N_KV)]
            a = jnp.concatenate(rows, axis=0).astype(BF16)
            wa = w1_ref[slot, s]
            wb = w1_ref[slot, CMP_STRIDE + s]
            acca[...] += _dot(a, wa)
            accb[...] += _dot(a, wb)
            return pacc + _dot(pe_ref[slot, s], wa) + _dot(pe_ref[slot, CMP_STRIDE + s], wb)

        pacc = lax.fori_loop(0, CMP_STRIDE, body, jnp.zeros((16, CMP_HID), F32))
        pe_hid = pacc[0:1]
        for grp in range(N_KV):
            hid = (acca[pl.ds(grp * XROWS, n_blk), :] + accb[pl.ds(grp * XROWS + 1, n_blk), :] + pe_hid)
            act = jax.nn.gelu(hid).astype(BF16)
            if slot == 0:
                kc_ref[0, grp, :, 0:HEAD_DIM] = _dot(act, w2_ref[0]).astype(BF16)
                kc_ref[0, grp, :, HEAD_DIM:LANE] = akc_ref[...]
            else:
                vct_ref[0, grp] = _dot_nt(w2t_ref[1], act).astype(BF16)


def _compress(src, table, w1, w2, w2t, pe, alibi_kc, n_seq, n_seg):
    n_blk = PAGES_PER_SEG * 8
    gd2 = 2 * N_KV * HEAD_DIM
    dst = jnp.arange(PAGE)
    src_row = (dst % 8) * CMP_STRIDE + dst // 8
    perm = (jnp.arange(PAGE)[None, :] == src_row[:, None]).astype(BF16)

    def page_spec(p):
        return pl.BlockSpec((1, PAGE, gd2), lambda i, tbl, p=p: (tbl[i, p], 0, 0))

    grid_spec = pltpu.PrefetchScalarGridSpec(
        num_scalar_prefetch=1,
        grid=(n_seq * n_seg,),
        in_specs=[page_spec(p) for p in range(PAGES_PER_SEG + 1)] + [
            pl.BlockSpec((PAGE, PAGE), lambda i, tbl: (0, 0)),
            pl.BlockSpec(w1.shape, lambda i, tbl: (0, 0, 0, 0)),
            pl.BlockSpec(w2.shape, lambda i, tbl: (0, 0, 0)),
            pl.BlockSpec(w2t.shape, lambda i, tbl: (0, 0, 0)),
            pl.BlockSpec(pe.shape, lambda i, tbl: (0, 0, 0, 0)),
            pl.BlockSpec((LANE, HEAD_DIM), lambda i, tbl: (0, 0))],
        out_specs=[pl.BlockSpec((1, N_KV, n_blk, LANE), lambda i, tbl: (i // n_seg, 0, i % n_seg, 0)),
                   pl.BlockSpec((1, N_KV, HEAD_DIM, n_blk), lambda i, tbl: (i // n_seg, 0, 0, i % n_seg))],
        scratch_shapes=[pltpu.VMEM((CMP_STRIDE, XROWS, gd2), F32),
                        pltpu.VMEM((N_KV * XROWS, CMP_HID), F32),
                        pltpu.VMEM((N_KV * XROWS, CMP_HID), F32)])
    return pl.pallas_call(
        _compress_kernel,
        grid_spec=grid_spec,
        out_shape=[jax.ShapeDtypeStruct((n_seq, N_KV, n_seg * n_blk, LANE), BF16),
                   jax.ShapeDtypeStruct((n_seq, N_KV, HEAD_DIM, n_seg * n_blk), BF16)],
        compiler_params=_params(1),
    )(table, *([src] * (PAGES_PER_SEG + 1)), perm, w1, w2, w2t, pe, alibi_kc)


def _qg_prompt_kernel(x_ref, sh_ref, sc_ref, g_ref, wq_ref, wgt_ref, aq_ref, q_ref, gt_ref):
    hb = _ada_norm(x_ref[...], g_ref[...], sh_ref[0], sc_ref[0]).astype(BF16)
    q = _dot(hb, wq_ref[...]) * (HEAD_DIM ** -0.5 * LOG2E)
    tm = x_ref.shape[0]
    for h in range(N_HEADS):
        q_ref[0, h, :, 0:HEAD_DIM] = q[:, h * HEAD_DIM:(h + 1) * HEAD_DIM].astype(BF16)
        for c in range(tm // LANE):
            q_ref[0, h, c * LANE:(c + 1) * LANE, HEAD_DIM:LANE] = aq_ref[h]
    gates = jax.nn.sigmoid(_dot_nt(wgt_ref[...], hb))
    for grp in range(N_KV):
        gt_ref[0, grp] = gates[grp * 16:(grp + 1) * 16, :]


def _qg_sample_kernel(x_ref, sh_ref, sc_ref, g_ref, wq_ref, wg_ref, q_ref, gt_ref):
    hb = _ada_norm(x_ref[...], g_ref[...], sh_ref[0], sc_ref[0]).astype(BF16)
    q_ref[...] = _dot(hb, wq_ref[...]) * (HEAD_DIM ** -0.5)
    gt_ref[...] = jax.nn.sigmoid(_dot(hb, wg_ref[...]))


def _qg_proj(tok, x, shift, scale, g, wq, wg_t, wg_nat, alibi_q):
    tm, nt = tok.tiles(512)
    in_specs = [tok.x_spec(tm), tok.mod_spec(tm), tok.mod_spec(tm), _full_spec((1, D_MODEL)),
                _full_spec((D_MODEL, D_MODEL))]
    args = [x, tok.mod_arr(shift), tok.mod_arr(scale), g.reshape(1, D_MODEL), wq]
    if tok.per_row:
        return pl.pallas_call(
            _qg_sample_kernel, grid=(nt,),
            in_specs=in_specs + [_full_spec((D_MODEL, LANE))],
            out_specs=[tok.x_spec(tm), tok.x_spec(tm, LANE)],
            out_shape=[jax.ShapeDtypeStruct((tok.n_tok, D_MODEL), F32),
                       jax.ShapeDtypeStruct((tok.n_tok, LANE), F32)],
            compiler_params=_params(1))(*args, wg_nat)
    per_seq = tok.seq_len // tm
    return pl.pallas_call(
        _qg_prompt_kernel, grid=(nt,),
        in_specs=in_specs + [_full_spec((N_KV * 16, D_MODEL)), _full_spec((N_HEADS, LANE, HEAD_DIM))],
        out_specs=[pl.BlockSpec((1, N_HEADS, tm, LANE), lambda i: (i // per_seq, 0, i % per_seq, 0)),
                   pl.BlockSpec((1, N_KV, 16, tm), lambda i: (i // per_seq, 0, 0, i % per_seq))],
        out_shape=[jax.ShapeDtypeStruct((tok.n_seq, N_HEADS, tok.seq_len, LANE), BF16),
                   jax.ShapeDtypeStruct((tok.n_seq, N_KV, 16, tok.seq_len), F32)],
        compiler_params=_params(1))(*args, wg_t, alibi_q)


def _select_blocks(work, idx_f, axis, rounds):
    sel = jnp.zeros_like(work)
    for _ in range(rounds):
        mx = jnp.max(work, axis=axis, keepdims=True)
        first = jnp.min(jnp.where(work == mx, idx_f, 1e9), axis=axis, keepdims=True)
        chosen = idx_f == first
        ok = jnp.where(mx > NEG / 2, 1.0, 0.0)
        sel = jnp.maximum(sel, jnp.where(chosen, ok, 0.0))
        work = jnp.where(chosen, PICKED, work)
    return sel


def _softmax_group(scores, offsets, state):
    m_old, l_old = state
    m_new = m_old
    for s, off in zip(scores, offsets):
        m_new = jnp.maximum(m_new, jnp.max(s, axis=0, keepdims=True) - off)
    alpha = jnp.exp2(m_old - m_new)
    l_new = alpha * l_old
    probs = []
    for s, off in zip(scores, offsets):
        p = jnp.exp2(s - (m_new + off))
        l_new = l_new + jnp.sum(p, axis=0, keepdims=True)
        probs.append(p)
    return (m_new, l_new), alpha, probs


def _nsa_prompt_kernel(q_ref, g_ref, kc_ref, vct_ref, ks_ref, vst_ref, kw_ref, vwt_ref, cov_ref, hot_ref, sl_ref,
                       o_ref, qp_scr, oc_scr, imp_scr, m_scr, l_scr, acc_scr, bits_ref, *, ncp):
    i = pl.program_id(2)
    cols = HPG * LANE
    key_io = lax.broadcasted_iota(jnp.int32, (LANE, LANE), 0)
    tok_io = lax.broadcasted_iota(jnp.int32, (LANE, LANE), 1)
    sl = sl_ref[0]
    qx = q_ref[0].reshape(cols, LANE)
    fi = i.astype(F32)

    def tile4(a):
        return jnp.concatenate([a] * HPG, axis=1)

    def finish(l):
        return jnp.where(l > 0.0, 1.0 / l, 0.0)

    init = (jnp.full((1, cols), M_FLOOR, F32), jnp.zeros((1, cols), F32))

    n_chunks = ncp // LANE
    c_last = jnp.minimum((8 * i + 6) // LANE, n_chunks - 1)
    rel_c = tile4((tok_io - CMP_STRIDE * key_io).astype(F32))
    cov = cov_ref[...]
    for k in range(n_chunks):
        @pl.when(c_last == k)
        def _(k=k):
            scores, offsets = [], []
            for c in range(k + 1):
                base = float(CMP_STRIDE * LANE * c + CMP_LEN - 1)
                s = _dot_nt(kc_ref[0, 0, c * LANE:(c + 1) * LANE, :], qx)
                scores.append(jnp.where(rel_c >= base - LANE * fi, s, NEG))
                offsets.append(sl * (LANE * fi - base))
            (_, l), _, probs = _softmax_group(scores, offsets, init)
            p = jnp.concatenate(probs, axis=0)
            p_hi, p_lo = _split_bf16(p)
            inv = finish(l)
            width = (k + 1) * LANE
            oc_scr[...] = _dot(vct_ref[0, 0, :, 0:width], p_hi) * inv
            imp4 = (_dot(cov[:, 0:width], p_hi) + _dot(cov[:, 0:width], p_lo)) * inv
            imp = imp4[:, 0:LANE]
            for h in range(1, HPG):
                imp = imp + imp4[:, h * LANE:(h + 1) * LANE]
            imp_scr[...] = imp

    q_blk = 2 * i + tok_io // SEL_BLOCK
    forced = (key_io == 0) | (key_io == q_blk) | (key_io == q_blk - 1)
    valid = key_io <= q_blk
    work = jnp.where(valid & jnp.logical_not(forced), imp_scr[...], NEG)
    sel = _select_blocks(work, key_io.astype(F32), 0, N_SEL - 3)
    sel = jnp.where(valid & forced, 1.0, sel)
    sel_bias_t = jnp.where(sel.T > 0.5, 0.0, NEG).astype(BF16)
    for h in range(HPG):
        qp_scr[h * LANE:(h + 1) * LANE, 0:LANE] = sel_bias_t
        qp_scr[h * LANE:(h + 1) * LANE, LANE:2 * LANE] = q_ref[0, h]

    cnt = _dot_nt(jnp.ones((8, LANE), BF16), sel.astype(BF16))[0:1]
    lane = lax.broadcasted_iota(jnp.int32, (1, LANE), 1)
    live = (cnt > 0.0) & (lane >= 2) & (lane < 2 * i)
    for w in range(LANE // 32):
        bits_ref[w] = jnp.sum(jnp.where(live & (lane // 32 == w), jnp.left_shift(1, lane % 32), 0))

    n_wt = WINDOW // LANE
    rel = tile4((tok_io - key_io).astype(F32))
    scores, offsets, vts = [], [], []
    for kt_i in range(n_wt + 1):
        jt = i - n_wt + kt_i
        jt_c = jnp.maximum(jt, 0)
        s = _dot_nt(kw_ref[0, 0, jt_c], qx)
        if kt_i == 0:
            s = jnp.where(rel <= 0.0, s, NEG)
        elif kt_i == n_wt:
            s = jnp.where(rel >= 0.0, s, NEG)
        scores.append(s)
        offsets.append(sl * float(LANE * (n_wt - kt_i)) + jnp.where(jt >= 0, 0.0, MASK_OFF).astype(F32))
        vts.append(vwt_ref[0, jt_c])
    (_, l), _, probs = _softmax_group(scores, offsets, init)
    p = jnp.concatenate([x.astype(BF16) for x in probs], axis=0)
    o_win = _dot(jnp.concatenate(vts, axis=1), p) * finish(l)

    def sel_scores(jp):
        kx = jnp.concatenate([hot_ref[jp], ks_ref[0, 0, jp]], axis=1)
        return _dot_nt(kx, qp_scr[...])

    s0 = sel_scores(0)
    sd = jnp.where(rel >= 0.0, sel_scores(i), NEG)
    off0 = sl * (LANE * fi) + jnp.where(i > 0, 0.0, MASK_OFF).astype(F32)
    (m, l), _, probs = _softmax_group([s0, sd], [off0, jnp.zeros((1, cols), F32)], init)
    p = jnp.concatenate([x.astype(BF16) for x in probs], axis=0)
    m_scr[...] = m
    l_scr[...] = l
    acc_scr[...] = _dot(jnp.concatenate([vst_ref[0, 0], vst_ref[0, i]], axis=1), p)

    def group_body(k, carry):
        word = bits_ref[k // 4]
        used = jnp.right_shift(word, (8 * k) % 32) & 255

        @pl.when(used != 0)
        def _():
            scores, offsets, vts = [], [], []
            for r in range(4):
                jp = 4 * k + r
                scores.append(sel_scores(jp))
                dead = jnp.where((jp == 0) | (jp >= i), MASK_OFF, 0.0).astype(F32)
                offsets.append(sl * (LANE * (i - jp)).astype(F32) + dead)
                vts.append(vst_ref[0, jp])
            (m, l), alpha, probs = _softmax_group(scores, offsets, (m_scr[...], l_scr[...]))
            p = jnp.concatenate([x.astype(BF16) for x in probs], axis=0)
            m_scr[...] = m
            l_scr[...] = l
            acc_scr[...] = alpha * acc_scr[...] + _dot(jnp.concatenate(vts, axis=1), p)
        return carry

    lax.fori_loop(0, (i + 3) // 4, group_body, 0)
    o_sel = acc_scr[...] * finish(l_scr[...])

    o_cmp = oc_scr[...]
    for h in range(HPG):
        cs = slice(h * LANE, (h + 1) * LANE)
        g_c = g_ref[0, 0, 3 * h:3 * h + 1, :]
        g_s = g_ref[0, 0, 3 * h + 1:3 * h + 2, :]
        g_w = g_ref[0, 0, 3 * h + 2:3 * h + 3, :]
        o_ref[0, h * HEAD_DIM:(h + 1) * HEAD_DIM, :] = g_c * o_cmp[:, cs] + g_s * o_sel[:, cs] + g_w * o_win[:, cs]


def _nsa_prompt(q, gates_t, kc, vct, k_ext, v_t, cov_t, onehot, sl_rows, n_seq, seq_len):
    n_qt = seq_len // LANE
    ncp = kc.shape[2]
    assert n_qt % 4 == 0
    kern = functools.partial(_nsa_prompt_kernel, ncp=ncp)
    cols = HPG * LANE
    return pl.pallas_call(
        kern,
        grid=(n_seq, N_KV, n_qt),
        in_specs=[pl.BlockSpec((1, HPG, LANE, LANE), lambda b, g, i: (b, g, i, 0)),
                  pl.BlockSpec((1, 1, 16, LANE), lambda b, g, i: (b, g, 0, i)),
                  pl.BlockSpec((1, 1, ncp, LANE), lambda b, g, i: (b, g, 0, 0)),
                  pl.BlockSpec((1, 1, HEAD_DIM, ncp), lambda b, g, i: (b, g, 0, 0)),
                  pl.BlockSpec((1, 1, n_qt, LANE, LANE), lambda b, g, i: (b, g, 0, 0, 0)),
                  pl.BlockSpec((1, n_qt, HEAD_DIM, LANE), lambda b, g, i: (b, 0, g, 0)),
                  pl.BlockSpec((1, 1, n_qt, LANE, LANE), lambda b, g, i: (b, N_KV + g, 0, 0, 0)),
                  pl.BlockSpec((1, n_qt, HEAD_DIM, LANE), lambda b, g, i: (b, 0, N_KV + g, 0)),
                  pl.BlockSpec((LANE, ncp), lambda b, g, i: (0, 0)),
                  pl.BlockSpec((n_qt, LANE, LANE), lambda b, g, i: (0, 0, 0)),
                  pl.BlockSpec((1, 1, cols), lambda b, g, i: (g, 0, 0))],
        out_specs=pl.BlockSpec((1, HPG * HEAD_DIM, LANE), lambda b, g, i: (b, g, i)),
        out_shape=jax.ShapeDtypeStruct((n_seq, N_HEADS * HEAD_DIM, seq_len), F32),
        scratch_shapes=[pltpu.VMEM((cols, 2 * LANE), BF16),
                        pltpu.VMEM((HEAD_DIM, cols), F32),
                        pltpu.VMEM((LANE, LANE), F32),
                        pltpu.VMEM((1, cols), F32),
                        pltpu.VMEM((1, cols), F32),
                        pltpu.VMEM((HEAD_DIM, cols), F32),
                        pltpu.SMEM((LANE // 32,), jnp.int32)],
        compiler_params=_params(3),
    )(q, gates_t, kc, vct, k_ext, v_t, k_ext, v_t, cov_t, onehot, sl_rows)


def _nsa_sample_kernel(tbl_ref, *refs, n_pages):
    pages = refs[:n_pages]
    (q_ref, g_ref, kc_ref, vct_ref, win_ref, new_ref, cov_ref, exp_ref, slope_ref,
     o_ref, s_scr) = refs[n_pages:]
    del tbl_ref
    gd = N_KV * HEAD_DIM
    past = n_pages * PAGE
    q_pos = float(past)
    qb = q_ref[0].astype(BF16)
    slope = slope_ref[...]
    row_grp = lax.broadcasted_iota(jnp.int32, (N_HEADS, LANE), 0) // HPG
    lane_f = lax.broadcasted_iota(jnp.int32, (N_HEADS, LANE), 1).astype(F32)
    new = new_ref[0]
    first_row = lax.broadcasted_iota(jnp.int32, (LANE, HEAD_DIM), 0) == 0

    def own_rows(g_sel, pick):
        out = pick(0)
        for grp in range(1, N_KV):
            out = jnp.where(g_sel == grp, pick(grp), out)
        return out

    def new_tile(slot, grp):
        col = slot * gd + grp * HEAD_DIM
        return jnp.where(first_row, new[:, col:col + HEAD_DIM], 0.0).astype(BF16)

    def softmax_rows(s, valid):
        s = jnp.where(valid, s, NEG)
        m = jnp.max(s, axis=1, keepdims=True)
        p = jnp.where(valid, jnp.exp(s - m), 0.0)
        l = jnp.sum(p, axis=1, keepdims=True)
        return p * jnp.where(l > 0.0, 1.0 / l, 0.0)

    d_c = q_pos - (CMP_STRIDE * lane_f + (CMP_LEN - 1))
    s_c = own_rows(row_grp, lambda grp: _dot_nt(qb, kc_ref[0, grp][:, 0:HEAD_DIM])) - slope * d_c
    p_c = softmax_rows(s_c, d_c >= 0.0)
    p_cb = p_c.astype(BF16)
    row_grp_o = lax.broadcasted_iota(jnp.int32, (N_HEADS, HEAD_DIM), 0) // HPG
    o_c = own_rows(row_grp_o, lambda grp: _dot_nt(p_cb, vct_ref[0, grp]))

    p_grp = own_rows(row_grp, lambda grp: jnp.broadcast_to(
        jnp.sum(p_c[grp * HPG:(grp + 1) * HPG], axis=0, keepdims=True), (N_HEADS, LANE)))
    p_hi, p_lo = _split_bf16(p_grp)
    cov = cov_ref[...]
    imp = _dot(p_hi, cov) + _dot(p_lo, cov)
    q_blk = float(past // SEL_BLOCK)
    forced = (lane_f == 0.0) | (lane_f == q_blk) | (lane_f == q_blk - 1.0)
    work = jnp.where(lane_f <= q_blk, imp + jnp.where(forced, FORCE_BONUS, 0.0), NEG)
    sel = _select_blocks(work, lane_f, 1, N_SEL)

    n_t = n_pages + 1
    for t in range(n_t):
        if t < n_pages:
            tile = lambda grp, t=t: pages[t][0, :, grp * HEAD_DIM:(grp + 1) * HEAD_DIM].astype(BF16)
        else:
            tile = lambda grp: new_tile(2, grp)
        s_scr[:, t * LANE:(t + 1) * LANE] = own_rows(row_grp, lambda grp: _dot_nt(qb, tile(grp)))
    width = n_t * LANE
    kpos = lax.broadcasted_iota(jnp.int32, (N_HEADS, width), 1).astype(F32)
    d_s = q_pos - kpos
    chosen = _dot(sel.astype(BF16), exp_ref[...])
    slope_w = jnp.concatenate([slope] * n_t, axis=1)
    p_s = softmax_rows(s_scr[...] - slope_w * d_s, (chosen > 0.5) & (d_s >= 0.0)).astype(BF16)
    o_s = jnp.zeros((N_HEADS, HEAD_DIM), F32)
    for t in range(n_t):
        pt = p_s[:, t * LANE:(t + 1) * LANE]
        if t < n_pages:
            vtile = lambda grp, t=t: pages[t][0, :, gd + grp * HEAD_DIM: gd + (grp + 1) * HEAD_DIM].astype(BF16)
        else:
            vtile = lambda grp: new_tile(3, grp)
        o_s = o_s + own_rows(row_grp_o, lambda grp: _dot(pt, vtile(grp)))

    buf = win_ref.shape[1]
    n_w = buf // LANE + 1
    w_parts = []
    for t in range(n_w):
        if t < n_w - 1:
            tile = lambda grp, t=t: win_ref[0, t * LANE:(t + 1) * LANE,
                                            grp * HEAD_DIM:(grp + 1) * HEAD_DIM].astype(BF16)
        else:
            tile = lambda grp: new_tile(4, grp)
        w_parts.append(own_rows(row_grp, lambda grp: _dot_nt(qb, tile(grp))))
    s_w = jnp.concatenate(w_parts, axis=1)
    wpos = lax.broadcasted_iota(jnp.int32, (N_HEADS, n_w * LANE), 1).astype(F32) + float(past - buf)
    d_w = q_pos - wpos
    slope_ww = jnp.concatenate([slope] * n_w, axis=1)
    p_w = softmax_rows(s_w - slope_ww * d_w, (d_w >= 0.0) & (d_w <= float(WINDOW))).astype(BF16)
    o_w = jnp.zeros((N_HEADS, HEAD_DIM), F32)
    for t in range(n_w):
        pt = p_w[:, t * LANE:(t + 1) * LANE]
        if t < n_w - 1:
            vtile = lambda grp, t=t: win_ref[0, t * LANE:(t + 1) * LANE,
                                             gd + grp * HEAD_DIM: gd + (grp + 1) * HEAD_DIM].astype(BF16)
        else:
            vtile = lambda grp: new_tile(5, grp)
        o_w = o_w + own_rows(row_grp_o, lambda grp: _dot(pt, vtile(grp)))

    gt = g_ref[0]
    o_ref[0] = gt[:, 0:1] * o_c + gt[:, 1:2] * o_s + gt[:, 2:3] * o_w


def _nsa_sample(q, gates, kc, vct, cache_pages, page_table, win, kv_new, cov, expand, slope16):
    n_seq, n_pages = page_table.shape
    gd = N_KV * HEAD_DIM
    buf = win.shape[1]
    kern = functools.partial(_nsa_sample_kernel, n_pages=n_pages)

    def page_spec(p):
        return pl.BlockSpec((1, PAGE, 2 * gd), lambda b, tbl, p=p: (tbl[b, p], 0, 1))

    def const_spec(shape):
        nd = len(shape)
        return pl.BlockSpec(shape, lambda b, tbl: (0,) * nd)

    grid_spec = pltpu.PrefetchScalarGridSpec(
        num_scalar_prefetch=1,
        grid=(n_seq,),
        in_specs=[page_spec(p) for p in range(n_pages)] + [
            pl.BlockSpec((1, N_HEADS, HEAD_DIM), lambda b, tbl: (b, 0, 0)),
            pl.BlockSpec((1, N_HEADS, 3), lambda b, tbl: (b, 0, 0)),
            pl.BlockSpec((1, N_KV, LANE, LANE), lambda b, tbl: (b, 0, 0, 0)),
            pl.BlockSpec((1, N_KV, HEAD_DIM, LANE), lambda b, tbl: (b, 0, 0, 0)),
            pl.BlockSpec((1, buf, 2 * gd), lambda b, tbl: (b, 0, 0)),
            pl.BlockSpec((1, 1, N_KV_PROJ * gd), lambda b, tbl: (b, 0, 0)),
            const_spec(cov.shape), const_spec(expand.shape), const_spec(slope16.shape)],
        out_specs=pl.BlockSpec((1, N_HEADS, HEAD_DIM), lambda b, tbl: (b, 0, 0)),
        scratch_shapes=[pltpu.VMEM((N_HEADS, (n_pages + 1) * LANE), F32)])
    return pl.pallas_call(
        kern,
        grid_spec=grid_spec,
        out_shape=jax.ShapeDtypeStruct((n_seq, N_HEADS, HEAD_DIM), F32),
        compiler_params=_params(1),
    )(page_table, *([cache_pages] * n_pages), q, gates, kc, vct, win, kv_new, cov, expand, slope16)


def _oproj_prompt_kernel(ot_ref, x_ref, gt_ref, wo_ref, o_ref):
    o = ot_ref[0].T.astype(BF16)
    o_ref[...] = x_ref[...] + gt_ref[0] * _dot(o, wo_ref[...])


def _oproj_sample_kernel(a_ref, x_ref, gt_ref, wo_ref, o_ref):
    o_ref[...] = x_ref[...] + gt_ref[0] * _dot(a_ref[...].astype(BF16), wo_ref[...])


def _out_proj(tok, attn, x, gate, wo):
    tm, nt = tok.tiles(512)
    if tok.per_row:
        kern, a_spec = _oproj_sample_kernel, tok.x_spec(tm)
    else:
        per_seq = tok.seq_len // tm
        kern = _oproj_prompt_kernel
        a_spec = pl.BlockSpec((1, D_MODEL, tm), lambda i: (i // per_seq, 0, i % per_seq))
    return pl.pallas_call(
        kern, grid=(nt,),
        in_specs=[a_spec, tok.x_spec(tm), tok.mod_spec(tm), _full_spec((D_MODEL, D_MODEL))],
        out_specs=tok.x_spec(tm),
        out_shape=jax.ShapeDtypeStruct((tok.n_tok, D_MODEL), F32),
        compiler_params=_params(1))(attn, x, tok.mod_arr(gate), wo)


def _final_kernel(x_ref, sh_ref, sc_ref, g_ref, o_ref):
    o_ref[...] = _ada_norm(x_ref[...], g_ref[...], sh_ref[0], sc_ref[0])


def _final(tok, x, shift, scale, g):
    tm, nt = tok.tiles(512)
    return pl.pallas_call(
        _final_kernel, grid=(nt,),
        in_specs=[tok.x_spec(tm), tok.mod_spec(tm), tok.mod_spec(tm), _full_spec((1, D_MODEL))],
        out_specs=tok.x_spec(tm),
        out_shape=jax.ShapeDtypeStruct((tok.n_tok, D_MODEL), F32),
        compiler_params=_params(1))(x, tok.mod_arr(shift), tok.mod_arr(scale), g.reshape(1, D_MODEL))


def _alibi_slopes():
    h = jnp.arange(1, N_HEADS + 1, dtype=F32)
    return jnp.exp2(-8.0 * h / N_HEADS)


def _cover(n_cmp, n_sel):
    c_start = jnp.arange(n_cmp)[:, None] * CMP_STRIDE
    s_start = jnp.arange(n_sel)[None, :] * SEL_BLOCK
    return ((c_start < s_start + SEL_BLOCK) & (c_start + CMP_LEN > s_start)).astype(BF16)


def _trunk(tok, x, mods, kv_mod, f_mod, wts, ctx):
    depth = wts['norm_g'].shape[0]
    n_a = depth // 2
    v_rows = []
    kv = None
    attn_ctx = None
    for l in range(depth):
        m = mods[l]
        if l == n_a:
            kv_out = _kv_proj(tok, x, kv_mod[:, 0], kv_mod[:, 1], wts['kv_norm_g'], wts['kv_w'], wts['kv_wvt'],
                              wts['alibi_k'])
            kv, attn_ctx = ctx['prepare'](kv_out)
        x = _ffn(tok, x, m[:, 0], m[:, 1], m[:, 2], wts['norm_g'][l, 0],
                 wts['ffn_w_gate'][l, 0], wts['ffn_w_up'][l, 0], wts['ffn_w_down'][l, 0])
        if l < n_a:
            x, v = _gmlp(tok, x, m[:, 3], m[:, 4], m[:, 5], wts['norm_g'][l, 1], wts['gmlp_w_uv'][l],
                         wts['gmlp_ln_g'][l], wts['gmlp_ln_b'][l], wts['gmlp_w_sp'][l], wts['gmlp_b_sp'][l],
                         wts['gmlp_w_out'][l])
            v_rows.append(v)
        else:
            j = l - n_a
            q, gates = _qg_proj(tok, x, m[:, 3], m[:, 4], wts['norm_g'][l, 1], wts['nsa_wq'][j],
                                wts['nsa_wg_t'][j], wts['nsa_wg'][j], wts['alibi_q'])
            attn = ctx['attend'](q, gates, attn_ctx)
            x = _out_proj(tok, attn, x, m[:, 5], wts['nsa_w_o'][j])
        x = _ffn(tok, x, m[:, 6], m[:, 7], m[:, 8], wts['norm_g'][l, 2],
                 wts['ffn_w_gate'][l, 1], wts['ffn_w_up'][l, 1], wts['ffn_w_down'][l, 1])
    y = _final(tok, x, f_mod[:, 0], f_mod[:, 1], wts['final_g'])
    return y, kv, v_rows


def kernel(x_prompt, x_sample, cache_kv, state_win_kv, page_table, c_prompt, c_sample, ada_w, ada_b, norm_g, ffn_w_gate, ffn_w_up, ffn_w_down, gmlp_w_uv, gmlp_ln_g, gmlp_ln_b, gmlp_w_sp, gmlp_b_sp, gmlp_w_out, nsa_w_qg, nsa_w_o, kv_norm_g, kv_ada_w, kv_ada_b, kv_w, cmp_w1, cmp_w2, cmp_pe, final_g, final_ada_w, final_ada_b):
    n_p, seq, _ = x_prompt.shape
    n_s, dec_seq, _ = x_sample.shape
    assert dec_seq == 1 and seq % (PAGES_PER_SEG * PAGE) == 0
    depth = ada_w.shape[0]
    n_b = nsa_w_qg.shape[0]
    gd = N_KV * HEAD_DIM
    n_pages = page_table.shape[1]
    assert n_pages == PAGES_PER_SEG
    past = n_pages * PAGE

    nq = N_HEADS * HEAD_DIM
    wg_cols = nsa_w_qg[:, :, nq:]
    wg_pad = jnp.pad(wg_cols.reshape(n_b, D_MODEL, N_KV, HPG * 3), ((0, 0), (0, 0), (0, 0), (0, 16 - HPG * 3)))
    wts = dict(
        norm_g=norm_g, kv_norm_g=kv_norm_g, final_g=final_g,
        ffn_w_gate=ffn_w_gate.astype(BF16), ffn_w_up=ffn_w_up.astype(BF16), ffn_w_down=ffn_w_down.astype(BF16),
        gmlp_w_uv=gmlp_w_uv.astype(BF16), gmlp_ln_g=gmlp_ln_g, gmlp_ln_b=gmlp_ln_b,
        gmlp_w_sp=gmlp_w_sp, gmlp_b_sp=gmlp_b_sp, gmlp_w_out=gmlp_w_out.astype(BF16),
        nsa_wq=nsa_w_qg[:, :, :nq].astype(BF16),
        nsa_wg_t=jnp.swapaxes(wg_pad.reshape(n_b, D_MODEL, N_KV * 16), 1, 2).astype(BF16),
        nsa_wg=jnp.pad(wg_cols, ((0, 0), (0, 0), (0, LANE - N_HEADS * 3))).astype(BF16),
        nsa_w_o=nsa_w_o.astype(BF16),
        kv_w=kv_w.astype(BF16),
        kv_wvt=jnp.concatenate([kv_w[:, 3 * gd:4 * gd], kv_w[:, 5 * gd:6 * gd]], axis=1).T.astype(BF16),
    )
    w1 = cmp_w1.astype(BF16).reshape(2, CMP_LEN, HEAD_DIM, CMP_HID)
    w2 = cmp_w2.astype(BF16)
    w2t = jnp.swapaxes(cmp_w2, 1, 2).astype(BF16)
    pe = jnp.broadcast_to(cmp_pe.astype(BF16)[:, :, None, :], (2, CMP_LEN, 16, HEAD_DIM))
    slopes = _alibi_slopes()
    sl2 = slopes * LOG2E
    off_f = jnp.arange(LANE, dtype=F32)

    def split3(x):
        a = x.astype(BF16)
        b = (x - a.astype(F32)).astype(BF16)
        c = (x - a.astype(F32) - b.astype(F32)).astype(BF16)
        return [a, b, c]

    q_cols = split3(-sl2[:, None] * off_f[None, :]) + \
        [jnp.broadcast_to(c[:, None], (N_HEADS, LANE)) for c in split3(sl2)] + \
        [jnp.broadcast_to(c[:, None], (N_HEADS, LANE)) for c in split3(sl2 * CMP_STRIDE)]
    wts['alibi_q'] = jnp.pad(jnp.stack(q_cols, axis=-1), ((0, 0), (0, 0), (0, HEAD_DIM - 9)))
    ones, zeros, offs = jnp.ones((LANE,), BF16), jnp.zeros((LANE,), BF16), off_f.astype(BF16)
    wts['alibi_k'] = jnp.pad(jnp.stack([ones] * 3 + [offs] * 3 + [zeros] * 3, axis=-1), ((0, 0), (0, HEAD_DIM - 9)))
    alibi_kc = jnp.pad(jnp.stack([ones] * 3 + [zeros] * 3 + [offs] * 3, axis=-1), ((0, 0), (0, HEAD_DIM - 9)))

    n_c = n_p + n_s
    c_all = jnp.pad(jnp.concatenate([c_prompt, c_sample], axis=0), ((0, (-n_c) % 8), (0, 0)))
    mod_all = _mod_linear(c_all, ada_w, ada_b)
    kv_mod_all = _mod_linear(c_all, kv_ada_w[None], kv_ada_b[None])[0]
    f_mod_all = _mod_linear(c_all, final_ada_w[None], final_ada_b[None])[0]

    def rows(a, lo, hi, k):
        return a[..., lo:hi, :].reshape(a.shape[:-2] + (hi - lo, k, D_MODEL))

    tok_p = _Tok(n_p, seq, per_row=False)
    n_seg = seq // (PAGES_PER_SEG * PAGE)
    n_qt = seq // LANE
    ncp = n_seg * PAGES_PER_SEG * 8
    n_sel_p = seq // SEL_BLOCK
    assert n_sel_p <= LANE
    cov_t = jnp.pad(_cover(ncp, n_sel_p).T, ((0, LANE - n_sel_p), (0, 0)))
    sl_rows = jnp.repeat(sl2.reshape(N_KV, 1, HPG), LANE, axis=2)
    onehot = (jnp.arange(LANE)[None, None, :] ==
              (2 * jnp.arange(n_qt)[:, None, None] + jnp.arange(LANE)[None, :, None] // SEL_BLOCK)).astype(BF16)

    def prepare_p(kv_out):
        kv, k_nat, v_t = kv_out
        pages_per_seq = seq // PAGE
        base = jnp.arange(n_p * n_seg, dtype=jnp.int32)[:, None] * PAGES_PER_SEG
        table = jnp.minimum(base + jnp.arange(PAGES_PER_SEG + 1, dtype=jnp.int32)[None, :],
                            n_p * pages_per_seq - 1)
        kc, vct = _compress(kv.reshape(n_p * pages_per_seq, PAGE, N_KV_PROJ * gd), table, w1, w2, w2t, pe,
                            alibi_kc, n_p, n_seg)
        return kv, (kc, vct, k_nat, v_t)

    def attend_p(q, gates, c):
        kc, vct, k_nat, v_t = c
        return _nsa_prompt(q, gates, kc, vct, k_nat, v_t, cov_t, onehot, sl_rows, n_p, seq)

    y_p, kv_p, _ = _trunk(tok_p, x_prompt.reshape(n_p * seq, D_MODEL),
                          rows(mod_all, 0, n_p, 9), rows(kv_mod_all, 0, n_p, 2), rows(f_mod_all, 0, n_p, 2),
                          wts, dict(prepare=prepare_p, attend=attend_p))
    kv_p = kv_p.reshape(n_p, seq, N_KV_PROJ, N_KV, HEAD_DIM)
    kv_prompt = kv_p[:, :, :4]
    win_prompt = kv_p[:, seq - min(WINDOW, seq):, 4:6]

    tok_s = _Tok(n_s, 1, per_row=True)
    cache_pages = cache_kv.reshape(cache_kv.shape[0], PAGE, 4 * gd)
    buf = state_win_kv.shape[1]
    win_flat = state_win_kv.reshape(n_s, buf, 2 * gd)
    n_sel_s = (past + 1 + SEL_BLOCK - 1) // SEL_BLOCK
    cov_s = jnp.pad(_cover(LANE, n_sel_s), ((0, 0), (0, LANE - n_sel_s)))
    n_keys = (n_pages + 1) * LANE
    expand_s = (jnp.arange(LANE)[:, None] == (jnp.arange(n_keys)[None, :] // SEL_BLOCK)).astype(BF16)
    slope16 = jnp.broadcast_to(slopes[:, None], (N_HEADS, LANE))

    def prepare_s(kv):
        table = jnp.concatenate([page_table, page_table[:, -1:]], axis=1)
        kc, vct = _compress(cache_pages, table, w1, w2, w2t, pe, alibi_kc, n_s, 1)
        return kv, (kc, vct, kv)

    def attend_s(q, gates, c):
        kc, vct, kv = c
        o = _nsa_sample(q.reshape(n_s, N_HEADS, HEAD_DIM), gates[:, :N_HEADS * 3].reshape(n_s, N_HEADS, 3),
                        kc, vct, cache_pages, page_table, win_flat, kv.reshape(n_s, 1, N_KV_PROJ * gd),
                        cov_s, expand_s, slope16)
        return o.reshape(n_s, N_HEADS * HEAD_DIM)

    y_s, kv_s, v_rows = _trunk(tok_s, x_sample.reshape(n_s, D_MODEL),
                               rows(mod_all, n_p, n_c, 9), rows(kv_mod_all, n_p, n_c, 2),
                               rows(f_mod_all, n_p, n_c, 2), wts, dict(prepare=prepare_s, attend=attend_s))
    kv_s = kv_s.reshape(n_s, 1, N_KV_PROJ, N_KV, HEAD_DIM)
    kv_sample = kv_s[:, :, :4]
    win_sample = jnp.concatenate([state_win_kv, kv_s[:, :, 4:6]], axis=1)[:, 1:]
    gmlp_v_sample = jnp.stack([v.reshape(n_s, 1, D_V) for v in v_rows])

    return (y_p.reshape(n_p, seq, D_MODEL), y_s.reshape(n_s, 1, D_MODEL), kv_prompt, kv_sample,
            win_prompt, win_sample, gmlp_v_sample)
```

```python
import functools

import jax
import jax.numpy as jnp
from jax import lax
from jax.experimental import pallas as pl
from jax.experimental.pallas import tpu as pltpu

F32 = jnp.float32
BF16 = jnp.bfloat16

D_MODEL = 1024
D_FF = 2816
D_V = 3072
CHUNK = 128
N_GROUPS_A = 8
CG = D_V // N_GROUPS_A
N_HEADS = 16
HEAD_DIM = 64
N_KV = 4
HPG = N_HEADS // N_KV
CMP_LEN = 32
CMP_STRIDE = 16
CMP_HID = 256
SEL_BLOCK = 64
N_SEL = 16
WINDOW = 512
PAGE = 128
N_KV_PROJ = 6
EPS = 1e-6
NEG = -1e30
M_FLOOR = -1e29
PICKED = -3e38
FORCE_BONUS = 1e4
LOG2E = 1.4426950408889634
MASK_OFF = 1e30

LANE = 128
VMEM_LIMIT = 56 * 1024 * 1024
FF_CHUNK = 256
UV_CHUNK = 512
N_NEAR = 8
PAGES_PER_SEG = 16
XROWS = PAGES_PER_SEG * 8 + 8


def _dot(a, b):
    return jnp.dot(a, b, preferred_element_type=F32)


def _dot_nt(a, b):
    return lax.dot_general(a, b, (((1,), (1,)), ((), ())), preferred_element_type=F32)


def _split_bf16(x):
    hi = x.astype(BF16)
    lo = (x - hi.astype(F32)).astype(BF16)
    return hi, lo


def _params(n_grid):
    return pltpu.CompilerParams(dimension_semantics=("arbitrary",) * n_grid,
                                vmem_limit_bytes=VMEM_LIMIT)


def _full_spec(shape):
    nd = len(shape)
    return pl.BlockSpec(shape, lambda *_: (0,) * nd, pipeline_mode=pl.Buffered(1))


def _ada_norm(x, g, shift, scale):
    ms = jnp.mean(x * x, axis=-1, keepdims=True)
    h = x * lax.rsqrt(ms + EPS) * g
    return h * (1.0 + scale) + shift


def _mod_kernel(c_ref, w_ref, b_ref, o_ref):
    c = c_ref[...]
    a = c * jax.nn.sigmoid(c)
    w = w_ref[0]
    a_hi, a_lo = _split_bf16(a)
    w_hi, w_lo = _split_bf16(w)
    o_ref[0] = _dot(a_hi, w_hi) + _dot(a_hi, w_lo) + _dot(a_lo, w_hi) + b_ref[0]


def _mod_linear(c, w, b):
    n_l, d, n = w.shape
    m = c.shape[0]
    tn = 1024
    return pl.pallas_call(
        _mod_kernel,
        grid=(n_l, n // tn),
        in_specs=[pl.BlockSpec((m, d), lambda l, j: (0, 0)),
                  pl.BlockSpec((1, d, tn), lambda l, j: (l, 0, j)),
                  pl.BlockSpec((1, 1, tn), lambda l, j: (l, 0, j))],
        out_specs=pl.BlockSpec((1, m, tn), lambda l, j: (l, 0, j)),
        out_shape=jax.ShapeDtypeStruct((n_l, m, n), F32),
        compiler_params=_params(2),
    )(c, w, b.reshape(n_l, 1, n))


class _Tok:
    def __init__(self, n_seq, seq_len, per_row):
        self.n_seq, self.seq_len = n_seq, seq_len
        self.n_tok = n_seq * seq_len
        self.per_row = per_row

    def tiles(self, tm):
        if self.per_row:
            return self.n_tok, 1
        assert self.seq_len % tm == 0
        return tm, self.n_tok // tm

    def x_spec(self, tm, width=D_MODEL):
        return pl.BlockSpec((tm, width), lambda i: (i, 0))

    def mod_spec(self, tm):
        if self.per_row:
            return pl.BlockSpec((1, tm, D_MODEL), lambda i: (0, 0, 0))
        per_seq = self.seq_len // tm
        return pl.BlockSpec((1, 1, D_MODEL), lambda i: (i // per_seq, 0, 0))

    def mod_arr(self, m):
        if self.per_row:
            return m.reshape(1, self.n_tok, D_MODEL)
        return m.reshape(self.n_seq, 1, D_MODEL)


def _ffn_kernel(x_ref, sh_ref, sc_ref, gt_ref, g_ref, wg_ref, wu_ref, wd_ref, o_ref, a_scr):
    x = x_ref[...]
    hb = _ada_norm(x, g_ref[...], sh_ref[0], sc_ref[0]).astype(BF16)
    for c in range(D_FF // FF_CHUNK):
        cs = slice(c * FF_CHUNK, (c + 1) * FF_CHUNK)
        gate = _dot(hb, wg_ref[:, cs])
        up = _dot(hb, wu_ref[:, cs])
        a_scr[:, cs] = (gate * jax.nn.sigmoid(gate) * up).astype(BF16)
    y = _dot(a_scr[...], wd_ref[...])
    o_ref[...] = x + (0.5 * gt_ref[0]) * y


def _ffn(tok, x, shift, scale, gate, g, wg, wu, wd):
    tm, nt = tok.tiles(512)
    return pl.pallas_call(
        _ffn_kernel,
        grid=(nt,),
        in_specs=[tok.x_spec(tm), tok.mod_spec(tm), tok.mod_spec(tm), tok.mod_spec(tm),
                  _full_spec((1, D_MODEL)), _full_spec((D_MODEL, D_FF)),
                  _full_spec((D_MODEL, D_FF)), _full_spec((D_FF, D_MODEL))],
        out_specs=tok.x_spec(tm),
        out_shape=jax.ShapeDtypeStruct((tok.n_tok, D_MODEL), F32),
        scratch_shapes=[pltpu.VMEM((tm, D_FF), BF16)],
        compiler_params=_params(1),
    )(x, tok.mod_arr(shift), tok.mod_arr(scale), tok.mod_arr(gate), g.reshape(1, D_MODEL), wg, wu, wd)


def _gmlp_uv(x_ref, sh_ref, sc_ref, g_ref, wuv_ref, lng_ref, lnb_ref, u_scr, v_scr):
    hb = _ada_norm(x_ref[...], g_ref[...], sh_ref[0], sc_ref[0]).astype(BF16)
    for c in range(D_V // UV_CHUNK):
        cs = slice(c * UV_CHUNK, (c + 1) * UV_CHUNK)
        cv = slice(D_V + c * UV_CHUNK, D_V + (c + 1) * UV_CHUNK)
        u_scr[:, cs] = jax.nn.gelu(_dot(hb, wuv_ref[:, cs]))
        v_scr[:, cs] = jax.nn.gelu(_dot(hb, wuv_ref[:, cv]))
    v = v_scr[...]
    mu = jnp.mean(v, axis=-1, keepdims=True)
    vc = v - mu
    var = jnp.mean(vc * vc, axis=-1, keepdims=True)
    return vc * lax.rsqrt(var + EPS) * lng_ref[...] + lnb_ref[...]


def _gmlp_prompt_kernel(x_ref, sh_ref, sc_ref, gt_ref, g_ref, wuv_ref, lng_ref, lnb_ref,
                        wsp_ref, bsp_ref, wout_ref, o_ref, u_scr, v_scr, a_scr):
    v_scr[...] = _gmlp_uv(x_ref, sh_ref, sc_ref, g_ref, wuv_ref, lng_ref, lnb_ref, u_scr, v_scr)
    row = lax.broadcasted_iota(jnp.int32, (CHUNK, CHUNK), 0)
    col = lax.broadcasted_iota(jnp.int32, (CHUNK, CHUNK), 1)
    tm = x_ref.shape[0]
    for grp in range(N_GROUPS_A):
        w = jnp.where(row >= col, wsp_ref[grp], 0.0).astype(BF16)
        cs = slice(grp * CG, (grp + 1) * CG)
        for n in range(tm // CHUNK):
            rs = slice(n * CHUNK, (n + 1) * CHUNK)
            s = _dot(w, v_scr[rs, cs].astype(BF16)) + bsp_ref[:, cs]
            a_scr[rs, cs] = (u_scr[rs, cs] * s).astype(BF16)
    y = _dot(a_scr[...], wout_ref[...])
    o_ref[...] = x_ref[...] + gt_ref[0] * y


def _gmlp_sample_kernel(x_ref, sh_ref, sc_ref, gt_ref, g_ref, wuv_ref, lng_ref, lnb_ref,
                        wrow_ref, brow_ref, wout_ref, o_ref, vout_ref, u_scr, v_scr):
    vn = _gmlp_uv(x_ref, sh_ref, sc_ref, g_ref, wuv_ref, lng_ref, lnb_ref, u_scr, v_scr)
    vout_ref[...] = vn
    s = vn * wrow_ref[...] + brow_ref[...]
    y = _dot((u_scr[...] * s).astype(BF16), wout_ref[...])
    o_ref[...] = x_ref[...] + gt_ref[0] * y


def _gmlp(tok, x, shift, scale, gate, g, wuv, ln_g, ln_b, w_sp, b_sp, wout):
    tm, nt = tok.tiles(256)
    common = [tok.x_spec(tm), tok.mod_spec(tm), tok.mod_spec(tm), tok.mod_spec(tm),
              _full_spec((1, D_MODEL)), _full_spec((D_MODEL, 2 * D_V)),
              _full_spec((1, D_V)), _full_spec((1, D_V))]
    args = [x, tok.mod_arr(shift), tok.mod_arr(scale), tok.mod_arr(gate), g.reshape(1, D_MODEL), wuv,
            ln_g.reshape(1, D_V), ln_b.reshape(1, D_V)]
    x_shape = jax.ShapeDtypeStruct((tok.n_tok, D_MODEL), F32)
    if tok.per_row:
        wrow = jnp.repeat(w_sp[:, 0, 0], CG).reshape(1, D_V)
        brow = jnp.repeat(b_sp[:, 0], CG).reshape(1, D_V)
        return pl.pallas_call(
            _gmlp_sample_kernel,
            grid=(nt,),
            in_specs=common + [_full_spec((1, D_V)), _full_spec((1, D_V)), _full_spec((D_V, D_MODEL))],
            out_specs=[tok.x_spec(tm), tok.x_spec(tm, D_V)],
            out_shape=[x_shape, jax.ShapeDtypeStruct((tok.n_tok, D_V), F32)],
            scratch_shapes=[pltpu.VMEM((tm, D_V), F32), pltpu.VMEM((tm, D_V), F32)],
            compiler_params=_params(1),
        )(*args, wrow, brow, wout)
    bias = jnp.repeat(b_sp.T, CG, axis=1)
    out = pl.pallas_call(
        _gmlp_prompt_kernel,
        grid=(nt,),
        in_specs=common + [_full_spec((N_GROUPS_A, CHUNK, CHUNK)), _full_spec((CHUNK, D_V)),
                           _full_spec((D_V, D_MODEL))],
        out_specs=tok.x_spec(tm),
        out_shape=x_shape,
        scratch_shapes=[pltpu.VMEM((tm, D_V), F32), pltpu.VMEM((tm, D_V), F32),
                        pltpu.VMEM((tm, D_V), BF16)],
        compiler_params=_params(1),
    )(*args, w_sp, bias, wout)
    return out, None


def _kv_prompt_kernel(x_ref, sh_ref, sc_ref, g_ref, w_ref, wvt_ref, ak_ref, kv_ref, k_ref, vt_ref):
    hb = _ada_norm(x_ref[...], g_ref[...], sh_ref[0], sc_ref[0]).astype(BF16)
    kv = _dot(hb, w_ref[...])
    kv_ref[...] = kv
    vt = _dot_nt(wvt_ref[...], hb).astype(BF16)
    tm = x_ref.shape[0]
    gd = N_KV * HEAD_DIM
    for c in range(tm // LANE):
        rs = slice(c * LANE, (c + 1) * LANE)
        vt_ref[0, c] = vt[:, rs]
        for j, slot in enumerate((2, 4)):
            for grp in range(N_KV):
                col = slot * gd + grp * HEAD_DIM
                k_ref[0, j * N_KV + grp, c, :, 0:HEAD_DIM] = kv[rs, col:col + HEAD_DIM].astype(BF16)
                k_ref[0, j * N_KV + grp, c, :, HEAD_DIM:LANE] = ak_ref[...]


def _kv_sample_kernel(x_ref, sh_ref, sc_ref, g_ref, w_ref, kv_ref):
    hb = _ada_norm(x_ref[...], g_ref[...], sh_ref[0], sc_ref[0]).astype(BF16)
    kv_ref[...] = _dot(hb, w_ref[...])


def _kv_proj(tok, x, shift, scale, g, w, wvt, alibi_k):
    tm, nt = tok.tiles(512)
    n_kv = N_KV_PROJ * N_KV * HEAD_DIM
    in_specs = [tok.x_spec(tm), tok.mod_spec(tm), tok.mod_spec(tm), _full_spec((1, D_MODEL)),
                _full_spec((D_MODEL, n_kv))]
    args = [x, tok.mod_arr(shift), tok.mod_arr(scale), g.reshape(1, D_MODEL), w]
    kv_shape = jax.ShapeDtypeStruct((tok.n_tok, n_kv), F32)
    if tok.per_row:
        return pl.pallas_call(
            _kv_sample_kernel, grid=(nt,), in_specs=in_specs, out_specs=tok.x_spec(tm, n_kv),
            out_shape=kv_shape, compiler_params=_params(1))(*args)
    per_seq = tok.seq_len // tm
    n_qt = tok.seq_len // LANE
    sub = tm // LANE
    return pl.pallas_call(
        _kv_prompt_kernel,
        grid=(nt,),
        in_specs=in_specs + [_full_spec((2 * N_KV * HEAD_DIM, D_MODEL)), _full_spec((LANE, HEAD_DIM))],
        out_specs=[tok.x_spec(tm, n_kv),
                   pl.BlockSpec((1, 2 * N_KV, sub, LANE, LANE),
                                lambda i: (i // per_seq, 0, i % per_seq, 0, 0)),
                   pl.BlockSpec((1, sub, 2 * N_KV * HEAD_DIM, LANE),
                                lambda i: (i // per_seq, i % per_seq, 0, 0))],
        out_shape=[kv_shape,
                   jax.ShapeDtypeStruct((tok.n_seq, 2 * N_KV, n_qt, LANE, LANE), BF16),
                   jax.ShapeDtypeStruct((tok.n_seq, n_qt, 2 * N_KV * HEAD_DIM, LANE), BF16)],
        compiler_params=_params(1),
    )(*args, wvt, alibi_k)


def _compress_kernel(tbl_ref, *refs):
    pages = refs[:PAGES_PER_SEG + 1]
    perm_ref, w1_ref, w2_ref, w2t_ref, pe_ref, akc_ref, kc_ref, vct_ref, x_scr, hb_scr = refs[PAGES_PER_SEG + 1:]
    del tbl_ref
    perm = perm_ref[...]
    n_heads = 2 * N_KV
    half = CMP_STRIDE * HEAD_DIM
    for p in range(PAGES_PER_SEG + 1):
        xp = _dot(perm, pages[p][0].astype(BF16))
        for hd in range(n_heads):
            cs = slice(hd * HEAD_DIM, (hd + 1) * HEAD_DIM)
            x_scr[hd, p * 8:(p + 1) * 8, :] = jnp.concatenate(
                [xp[s * 8:(s + 1) * 8, cs] for s in range(CMP_STRIDE)], axis=1)
    n_blk = PAGES_PER_SEG * 8
    for slot in range(2):
        x = x_scr[slot * N_KV:(slot + 1) * N_KV].reshape(N_KV * XROWS, half).astype(BF16)
        ha = _dot(x, w1_ref[slot, 0:half, :])
        hb_scr[...] = _dot(x, w1_ref[slot, half:2 * half, :])
        pe_hid = _dot(pe_ref[slot], w1_ref[slot])[0:1]
        for grp in range(N_KV):
            hid = ha[grp * XROWS:grp * XROWS + n_blk] + hb_scr[pl.ds(grp * XROWS + 1, n_blk), :] + pe_hid
            act = jax.nn.gelu(hid).astype(BF16)
            if slot == 0:
                kc_ref[0, grp, :, 0:HEAD_DIM] = _dot(act, w2_ref[0]).astype(BF16)
                kc_ref[0, grp, :, HEAD_DIM:LANE] = akc_ref[...]
            else:
                vct_ref[0, grp] = _dot_nt(w2t_ref[1], act).astype(BF16)


def _compress(src, table, w1, w2, w2t, pe, alibi_kc, n_seq, n_seg):
    n_blk = PAGES_PER_SEG * 8
    gd2 = 2 * N_KV * HEAD_DIM
    dst = jnp.arange(PAGE)
    src_row = (dst % 8) * CMP_STRIDE + dst // 8
    perm = (jnp.arange(PAGE)[None, :] == src_row[:, None]).astype(BF16)

    def page_spec(p):
        return pl.BlockSpec((1, PAGE, gd2), lambda i, tbl, p=p: (tbl[i, p], 0, 0))

    grid_spec = pltpu.PrefetchScalarGridSpec(
        num_scalar_prefetch=1,
        grid=(n_seq * n_seg,),
        in_specs=[page_spec(p) for p in range(PAGES_PER_SEG + 1)] + [
            pl.BlockSpec((PAGE, PAGE), lambda i, tbl: (0, 0)),
            pl.BlockSpec(w1.shape, lambda i, tbl: (0, 0, 0)),
            pl.BlockSpec(w2.shape, lambda i, tbl: (0, 0, 0)),
            pl.BlockSpec(w2t.shape, lambda i, tbl: (0, 0, 0)),
            pl.BlockSpec(pe.shape, lambda i, tbl: (0, 0, 0)),
            pl.BlockSpec((LANE, HEAD_DIM), lambda i, tbl: (0, 0))],
        out_specs=[pl.BlockSpec((1, N_KV, n_blk, LANE), lambda i, tbl: (i // n_seg, 0, i % n_seg, 0)),
                   pl.BlockSpec((1, N_KV, HEAD_DIM, n_blk), lambda i, tbl: (i // n_seg, 0, 0, i % n_seg))],
        scratch_shapes=[pltpu.VMEM((2 * N_KV, XROWS, CMP_STRIDE * HEAD_DIM), F32),
                        pltpu.VMEM((N_KV * XROWS, CMP_HID), F32)])
    return pl.pallas_call(
        _compress_kernel,
        grid_spec=grid_spec,
        out_shape=[jax.ShapeDtypeStruct((n_seq, N_KV, n_seg * n_blk, LANE), BF16),
                   jax.ShapeDtypeStruct((n_seq, N_KV, HEAD_DIM, n_seg * n_blk), BF16)],
        compiler_params=_params(1),
    )(table, *([src] * (PAGES_PER_SEG + 1)), perm, w1, w2, w2t, pe, alibi_kc)


def _qg_prompt_kernel(x_ref, sh_ref, sc_ref, g_ref, wq_ref, wgt_ref, aq_ref, q_ref, gt_ref):
    hb = _ada_norm(x_ref[...], g_ref[...], sh_ref[0], sc_ref[0]).astype(BF16)
    q = _dot(hb, wq_ref[...]) * (HEAD_DIM ** -0.5 * LOG2E)
    tm = x_ref.shape[0]
    for h in range(N_HEADS):
        q_ref[0, h, :, 0:HEAD_DIM] = q[:, h * HEAD_DIM:(h + 1) * HEAD_DIM].astype(BF16)
        for c in range(tm // LANE):
            q_ref[0, h, c * LANE:(c + 1) * LANE, HEAD_DIM:LANE] = aq_ref[h]
    gates = jax.nn.sigmoid(_dot_nt(wgt_ref[...], hb))
    for grp in range(N_KV):
        gt_ref[0, grp] = gates[grp * 16:(grp + 1) * 16, :]


def _qg_sample_kernel(x_ref, sh_ref, sc_ref, g_ref, wq_ref, wg_ref, q_ref, gt_ref):
    hb = _ada_norm(x_ref[...], g_ref[...], sh_ref[0], sc_ref[0]).astype(BF16)
    q_ref[...] = _dot(hb, wq_ref[...]) * (HEAD_DIM ** -0.5)
    gt_ref[...] = jax.nn.sigmoid(_dot(hb, wg_ref[...]))


def _qg_proj(tok, x, shift, scale, g, wq, wg_t, wg_nat, alibi_q):
    tm, nt = tok.tiles(512)
    in_specs = [tok.x_spec(tm), tok.mod_spec(tm), tok.mod_spec(tm), _full_spec((1, D_MODEL)),
                _full_spec((D_MODEL, D_MODEL))]
    args = [x, tok.mod_arr(shift), tok.mod_arr(scale), g.reshape(1, D_MODEL), wq]
    if tok.per_row:
        return pl.pallas_call(
            _qg_sample_kernel, grid=(nt,),
            in_specs=in_specs + [_full_spec((D_MODEL, LANE))],
            out_specs=[tok.x_spec(tm), tok.x_spec(tm, LANE)],
            out_shape=[jax.ShapeDtypeStruct((tok.n_tok, D_MODEL), F32),
                       jax.ShapeDtypeStruct((tok.n_tok, LANE), F32)],
            compiler_params=_params(1))(*args, wg_nat)
    per_seq = tok.seq_len // tm
    return pl.pallas_call(
        _qg_prompt_kernel, grid=(nt,),
        in_specs=in_specs + [_full_spec((N_KV * 16, D_MODEL)), _full_spec((N_HEADS, LANE, HEAD_DIM))],
        out_specs=[pl.BlockSpec((1, N_HEADS, tm, LANE), lambda i: (i // per_seq, 0, i % per_seq, 0)),
                   pl.BlockSpec((1, N_KV, 16, tm), lambda i: (i // per_seq, 0, 0, i % per_seq))],
        out_shape=[jax.ShapeDtypeStruct((tok.n_seq, N_HEADS, tok.seq_len, LANE), BF16),
                   jax.ShapeDtypeStruct((tok.n_seq, N_KV, 16, tok.seq_len), F32)],
        compiler_params=_params(1))(*args, wg_t, alibi_q)


def _select_blocks(work, idx_f, axis, rounds):
    sel = jnp.zeros_like(work)
    for _ in range(rounds):
        mx = jnp.max(work, axis=axis, keepdims=True)
        first = jnp.min(jnp.where(work == mx, idx_f, 1e9), axis=axis, keepdims=True)
        chosen = idx_f == first
        ok = jnp.where(mx > NEG / 2, 1.0, 0.0)
        sel = jnp.maximum(sel, jnp.where(chosen, ok, 0.0))
        work = jnp.where(chosen, PICKED, work)
    return sel


def _softmax_group(scores, offsets, state):
    m_old, l_old = state
    m_new = m_old
    for s, off in zip(scores, offsets):
        m_new = jnp.maximum(m_new, jnp.max(s, axis=0, keepdims=True) - off)
    alpha = jnp.exp2(m_old - m_new)
    l_new = alpha * l_old
    probs = []
    for s, off in zip(scores, offsets):
        p = jnp.exp2(s - (m_new + off))
        l_new = l_new + jnp.sum(p, axis=0, keepdims=True)
        probs.append(p)
    return (m_new, l_new), alpha, probs


def _nsa_prompt_kernel(q_ref, g_ref, kc_ref, vct_ref, ks_ref, vst_ref, kw_ref, vwt_ref, cov_ref, hot_ref, sl_ref,
                       o_ref, qp_scr, m_scr, l_scr, acc_scr, bits_ref, *, ncp):
    i = pl.program_id(2)
    cols = HPG * LANE
    key_io = lax.broadcasted_iota(jnp.int32, (LANE, LANE), 0)
    tok_io = lax.broadcasted_iota(jnp.int32, (LANE, LANE), 1)
    sl = sl_ref[0]
    qx = q_ref[0].reshape(cols, LANE)
    fi = i.astype(F32)

    def tile4(a):
        return jnp.concatenate([a] * HPG, axis=1)

    def finish(l):
        return jnp.where(l > 0.0, 1.0 / l, 0.0)

    init = (jnp.full((1, cols), M_FLOOR, F32), jnp.zeros((1, cols), F32))

    n_chunks = ncp // LANE
    rel_c = tile4((tok_io - CMP_STRIDE * key_io).astype(F32))
    scores, offsets = [], []
    for c in range(n_chunks):
        base = float(CMP_STRIDE * LANE * c + CMP_LEN - 1)
        s_c = _dot_nt(kc_ref[0, 0, c * LANE:(c + 1) * LANE, :], qx)
        scores.append(jnp.where(rel_c >= base - LANE * fi, s_c, NEG))
        offsets.append(sl * (LANE * fi - base))
    (_, l), _, probs = _softmax_group(scores, offsets, init)
    p_hi, p_lo = _split_bf16(jnp.concatenate(probs, axis=0))
    inv = finish(l)
    o_cmp = _dot(vct_ref[0, 0], p_hi) * inv
    cov = cov_ref[...]
    imp4 = (_dot(cov, p_hi) + _dot(cov, p_lo)) * inv
    imp = imp4[:, 0:LANE]
    for h in range(1, HPG):
        imp = imp + imp4[:, h * LANE:(h + 1) * LANE]

    q_blk = 2 * i + tok_io // SEL_BLOCK
    forced = (key_io == 0) | (key_io == q_blk) | (key_io == q_blk - 1)
    valid = key_io <= q_blk
    work = jnp.where(valid & jnp.logical_not(forced), imp, NEG)
    sel = _select_blocks(work, key_io.astype(F32), 0, N_SEL - 3)
    sel = jnp.where(valid & forced, 1.0, sel)
    sel_bias_t = jnp.where(sel.T > 0.5, 0.0, NEG).astype(BF16)
    for h in range(HPG):
        qp_scr[h * LANE:(h + 1) * LANE, 0:LANE] = sel_bias_t
        qp_scr[h * LANE:(h + 1) * LANE, LANE:2 * LANE] = q_ref[0, h]

    cnt = _dot_nt(jnp.ones((8, LANE), BF16), sel.astype(BF16))[0:1]
    lane = lax.broadcasted_iota(jnp.int32, (1, LANE), 1)
    live = (cnt > 0.0) & (lane >= 2) & (lane < 2 * (i - N_NEAR))
    for w in range(LANE // 32):
        bits_ref[w] = jnp.sum(jnp.where(live & (lane // 32 == w), jnp.left_shift(1, lane % 32), 0))

    n_wt = WINDOW // LANE
    rel = tile4((tok_io - key_io).astype(F32))
    scores, offsets, vts = [], [], []
    for kt_i in range(n_wt + 1):
        jt = i - n_wt + kt_i
        jt_c = jnp.maximum(jt, 0)
        s = _dot_nt(kw_ref[0, 0, jt_c], qx)
        if kt_i == 0:
            s = jnp.where(rel <= 0.0, s, NEG)
        elif kt_i == n_wt:
            s = jnp.where(rel >= 0.0, s, NEG)
        scores.append(s)
        offsets.append(sl * float(LANE * (n_wt - kt_i)) + jnp.where(jt >= 0, 0.0, MASK_OFF).astype(F32))
        vts.append(vwt_ref[0, jt_c])
    (_, l), _, probs = _softmax_group(scores, offsets, init)
    p = jnp.concatenate([x.astype(BF16) for x in probs], axis=0)
    o_win = _dot(jnp.concatenate(vts, axis=1), p) * finish(l)

    def sel_scores(jp):
        kx = jnp.concatenate([hot_ref[jp], ks_ref[0, 0, jp]], axis=1)
        return _dot_nt(kx, qp_scr[...])

    scores = [sel_scores(0), jnp.where(rel >= 0.0, sel_scores(i), NEG)]
    offsets = [sl * (LANE * fi) + jnp.where(i > N_NEAR, 0.0, MASK_OFF).astype(F32), jnp.zeros((1, cols), F32)]
    vts = [vst_ref[0, 0], vst_ref[0, i]]
    for r in range(1, N_NEAR + 1):
        jp = i - r
        jp_c = jnp.maximum(jp, 0)
        scores.append(sel_scores(jp_c))
        offsets.append(sl * float(LANE * r) + jnp.where(jp >= 0, 0.0, MASK_OFF).astype(F32))
        vts.append(vst_ref[0, jp_c])
    (m, l), _, probs = _softmax_group(scores, offsets, init)
    p = jnp.concatenate([x.astype(BF16) for x in probs], axis=0)
    m_scr[...] = m
    l_scr[...] = l
    acc_scr[...] = _dot(jnp.concatenate(vts, axis=1), p)

    def group_body(k, carry):
        word = bits_ref[k // 4]
        used = jnp.right_shift(word, (8 * k) % 32) & 255

        @pl.when(used != 0)
        def _():
            scores, offsets, vts = [], [], []
            for r in range(4):
                jp = 4 * k + r
                scores.append(sel_scores(jp))
                dead = jnp.where((jp == 0) | (jp >= i - N_NEAR), MASK_OFF, 0.0).astype(F32)
                offsets.append(sl * (LANE * (i - jp)).astype(F32) + dead)
                vts.append(vst_ref[0, jp])
            (m, l), alpha, probs = _softmax_group(scores, offsets, (m_scr[...], l_scr[...]))
            p = jnp.concatenate([x.astype(BF16) for x in probs], axis=0)
            m_scr[...] = m
            l_scr[...] = l
            acc_scr[...] = alpha * acc_scr[...] + _dot(jnp.concatenate(vts, axis=1), p)
        return carry

    lax.fori_loop(0, jnp.maximum(i - N_NEAR + 3, 0) // 4, group_body, 0)
    o_sel = acc_scr[...] * finish(l_scr[...])

    for h in range(HPG):
        cs = slice(h * LANE, (h + 1) * LANE)
        g_c = g_ref[0, 0, 3 * h:3 * h + 1, :]
        g_s = g_ref[0, 0, 3 * h + 1:3 * h + 2, :]
        g_w = g_ref[0, 0, 3 * h + 2:3 * h + 3, :]
        o_ref[0, h * HEAD_DIM:(h + 1) * HEAD_DIM, :] = g_c * o_cmp[:, cs] + g_s * o_sel[:, cs] + g_w * o_win[:, cs]


def _nsa_prompt(q, gates_t, kc, vct, k_ext, v_t, cov_t, onehot, sl_rows, n_seq, seq_len):
    n_qt = seq_len // LANE
    ncp = kc.shape[2]
    assert n_qt % 4 == 0
    kern = functools.partial(_nsa_prompt_kernel, ncp=ncp)
    cols = HPG * LANE
    return pl.pallas_call(
        kern,
        grid=(n_seq, N_KV, n_qt),
        in_specs=[pl.BlockSpec((1, HPG, LANE, LANE), lambda b, g, i: (b, g, i, 0)),
                  pl.BlockSpec((1, 1, 16, LANE), lambda b, g, i: (b, g, 0, i)),
                  pl.BlockSpec((1, 1, ncp, LANE), lambda b, g, i: (b, g, 0, 0)),
                  pl.BlockSpec((1, 1, HEAD_DIM, ncp), lambda b, g, i: (b, g, 0, 0)),
                  pl.BlockSpec((1, 1, n_qt, LANE, LANE), lambda b, g, i: (b, g, 0, 0, 0)),
                  pl.BlockSpec((1, n_qt, HEAD_DIM, LANE), lambda b, g, i: (b, 0, g, 0)),
                  pl.BlockSpec((1, 1, n_qt, LANE, LANE), lambda b, g, i: (b, N_KV + g, 0, 0, 0)),
                  pl.BlockSpec((1, n_qt, HEAD_DIM, LANE), lambda b, g, i: (b, 0, N_KV + g, 0)),
                  pl.BlockSpec((LANE, ncp), lambda b, g, i: (0, 0)),
                  pl.BlockSpec((n_qt, LANE, LANE), lambda b, g, i: (0, 0, 0)),
                  pl.BlockSpec((1, 1, cols), lambda b, g, i: (g, 0, 0))],
        out_specs=pl.BlockSpec((1, HPG * HEAD_DIM, LANE), lambda b, g, i: (b, g, i)),
        out_shape=jax.ShapeDtypeStruct((n_seq, N_HEADS * HEAD_DIM, seq_len), F32),
        scratch_shapes=[pltpu.VMEM((cols, 2 * LANE), BF16),
                        pltpu.VMEM((1, cols), F32),
                        pltpu.VMEM((1, cols), F32),
                        pltpu.VMEM((HEAD_DIM, cols), F32),
                        pltpu.SMEM((LANE // 32,), jnp.int32)],
        compiler_params=_params(3),
    )(q, gates_t, kc, vct, k_ext, v_t, k_ext, v_t, cov_t, onehot, sl_rows)


def _nsa_sample_kernel(tbl_ref, *refs, n_pages):
    pages = refs[:n_pages]
    (q_ref, g_ref, kc_ref, vct_ref, win_ref, new_ref, cov_ref, exp_ref, slope_ref,
     o_ref, s_scr) = refs[n_pages:]
    del tbl_ref
    gd = N_KV * HEAD_DIM
    past = n_pages * PAGE
    q_pos = float(past)
    qb = q_ref[0].astype(BF16)
    slope = slope_ref[...]
    row_grp = lax.broadcasted_iota(jnp.int32, (N_HEADS, LANE), 0) // HPG
    lane_f = lax.broadcasted_iota(jnp.int32, (N_HEADS, LANE), 1).astype(F32)
    new = new_ref[0]
    first_row = lax.broadcasted_iota(jnp.int32, (LANE, HEAD_DIM), 0) == 0

    def own_rows(g_sel, pick):
        out = pick(0)
        for grp in range(1, N_KV):
            out = jnp.where(g_sel == grp, pick(grp), out)
        return out

    def new_tile(slot, grp):
        col = slot * gd + grp * HEAD_DIM
        return jnp.where(first_row, new[:, col:col + HEAD_DIM], 0.0).astype(BF16)

    def softmax_rows(s, valid):
        s = jnp.where(valid, s, NEG)
        m = jnp.max(s, axis=1, keepdims=True)
        p = jnp.where(valid, jnp.exp(s - m), 0.0)
        l = jnp.sum(p, axis=1, keepdims=True)
        return p * jnp.where(l > 0.0, 1.0 / l, 0.0)

    d_c = q_pos - (CMP_STRIDE * lane_f + (CMP_LEN - 1))
    s_c = own_rows(row_grp, lambda grp: _dot_nt(qb, kc_ref[0, grp][:, 0:HEAD_DIM])) - slope * d_c
    p_c = softmax_rows(s_c, d_c >= 0.0)
    p_cb = p_c.astype(BF16)
    row_grp_o = lax.broadcasted_iota(jnp.int32, (N_HEADS, HEAD_DIM), 0) // HPG
    o_c = own_rows(row_grp_o, lambda grp: _dot_nt(p_cb, vct_ref[0, grp]))

    p_grp = own_rows(row_grp, lambda grp: jnp.broadcast_to(
        jnp.sum(p_c[grp * HPG:(grp + 1) * HPG], axis=0, keepdims=True), (N_HEADS, LANE)))
    p_hi, p_lo = _split_bf16(jnp.concatenate([p_grp, jnp.zeros((LANE - N_HEADS, LANE), F32)], axis=0))
    cov_t = cov_ref[...]
    imp_t = _dot_nt(cov_t, p_hi) + _dot_nt(cov_t, p_lo)
    blk = lax.broadcasted_iota(jnp.int32, (LANE, LANE), 0)
    q_blk = past // SEL_BLOCK
    forced = (blk == 0) | (blk == q_blk) | (blk == q_blk - 1)
    valid = blk <= q_blk
    work = jnp.where(valid & jnp.logical_not(forced), imp_t, NEG)
    sel_t = _select_blocks(work, blk.astype(F32), 0, N_SEL - 3)
    sel = jnp.where(valid & forced, 1.0, sel_t).T[0:N_HEADS]

    n_t = n_pages + 1
    for t in range(n_t):
        if t < n_pages:
            tile = lambda grp, t=t: pages[t][0, :, grp * HEAD_DIM:(grp + 1) * HEAD_DIM].astype(BF16)
        else:
            tile = lambda grp: new_tile(2, grp)
        s_scr[:, t * LANE:(t + 1) * LANE] = own_rows(row_grp, lambda grp: _dot_nt(qb, tile(grp)))
    width = n_t * LANE
    kpos = lax.broadcasted_iota(jnp.int32, (N_HEADS, width), 1).astype(F32)
    d_s = q_pos - kpos
    chosen = _dot(sel.astype(BF16), exp_ref[...])
    slope_w = jnp.concatenate([slope] * n_t, axis=1)
    p_s = softmax_rows(s_scr[...] - slope_w * d_s, (chosen > 0.5) & (d_s >= 0.0)).astype(BF16)
    o_s = jnp.zeros((N_HEADS, HEAD_DIM), F32)
    for t in range(n_t):
        pt = p_s[:, t * LANE:(t + 1) * LANE]
        if t < n_pages:
            vtile = lambda grp, t=t: pages[t][0, :, gd + grp * HEAD_DIM: gd + (grp + 1) * HEAD_DIM].astype(BF16)
        else:
            vtile = lambda grp: new_tile(3, grp)
        o_s = o_s + own_rows(row_grp_o, lambda grp: _dot(pt, vtile(grp)))

    buf = win_ref.shape[1]
    n_w = buf // LANE + 1
    w_parts = []
    for t in range(n_w):
        if t < n_w - 1:
            tile = lambda grp, t=t: win_ref[0, t * LANE:(t + 1) * LANE,
                                            grp * HEAD_DIM:(grp + 1) * HEAD_DIM].astype(BF16)
        else:
            tile = lambda grp: new_tile(4, grp)
        w_parts.append(own_rows(row_grp, lambda grp: _dot_nt(qb, tile(grp))))
    s_w = jnp.concatenate(w_parts, axis=1)
    wpos = lax.broadcasted_iota(jnp.int32, (N_HEADS, n_w * LANE), 1).astype(F32) + float(past - buf)
    d_w = q_pos - wpos
    slope_ww = jnp.concatenate([slope] * n_w, axis=1)
    p_w = softmax_rows(s_w - slope_ww * d_w, (d_w >= 0.0) & (d_w <= float(WINDOW))).astype(BF16)
    o_w = jnp.zeros((N_HEADS, HEAD_DIM), F32)
    for t in range(n_w):
        pt = p_w[:, t * LANE:(t + 1) * LANE]
        if t < n_w - 1:
            vtile = lambda grp, t=t: win_ref[0, t * LANE:(t + 1) * LANE,
                                             gd + grp * HEAD_DIM: gd + (grp + 1) * HEAD_DIM].astype(BF16)
        else:
            vtile = lambda grp: new_tile(5, grp)
        o_w = o_w + own_rows(row_grp_o, lambda grp: _dot(pt, vtile(grp)))

    gt = g_ref[0]
    o_ref[0] = gt[:, 0:1] * o_c + gt[:, 1:2] * o_s + gt[:, 2:3] * o_w


def _nsa_sample(q, gates, kc, vct, cache_pages, page_table, win, kv_new, cov, expand, slope16):
    n_seq, n_pages = page_table.shape
    gd = N_KV * HEAD_DIM
    buf = win.shape[1]
    kern = functools.partial(_nsa_sample_kernel, n_pages=n_pages)

    def page_spec(p):
        return pl.BlockSpec((1, PAGE, 2 * gd), lambda b, tbl, p=p: (tbl[b, p], 0, 1))

    def const_spec(shape):
        nd = len(shape)
        return pl.BlockSpec(shape, lambda b, tbl: (0,) * nd)

    grid_spec = pltpu.PrefetchScalarGridSpec(
        num_scalar_prefetch=1,
        grid=(n_seq,),
        in_specs=[page_spec(p) for p in range(n_pages)] + [
            pl.BlockSpec((1, N_HEADS, HEAD_DIM), lambda b, tbl: (b, 0, 0)),
            pl.BlockSpec((1, N_HEADS, 3), lambda b, tbl: (b, 0, 0)),
            pl.BlockSpec((1, N_KV, LANE, LANE), lambda b, tbl: (b, 0, 0, 0)),
            pl.BlockSpec((1, N_KV, HEAD_DIM, LANE), lambda b, tbl: (b, 0, 0, 0)),
            pl.BlockSpec((1, buf, 2 * gd), lambda b, tbl: (b, 0, 0)),
            pl.BlockSpec((1, 1, N_KV_PROJ * gd), lambda b, tbl: (b, 0, 0)),
            const_spec(cov.shape), const_spec(expand.shape), const_spec(slope16.shape)],
        out_specs=pl.BlockSpec((1, N_HEADS, HEAD_DIM), lambda b, tbl: (b, 0, 0)),
        scratch_shapes=[pltpu.VMEM((N_HEADS, (n_pages + 1) * LANE), F32)])
    return pl.pallas_call(
        kern,
        grid_spec=grid_spec,
        out_shape=jax.ShapeDtypeStruct((n_seq, N_HEADS, HEAD_DIM), F32),
        compiler_params=_params(1),
    )(page_table, *([cache_pages] * n_pages), q, gates, kc, vct, win, kv_new, cov, expand, slope16)


def _oproj_prompt_kernel(ot_ref, x_ref, gt_ref, wo_ref, o_ref):
    o = ot_ref[0].T.astype(BF16)
    o_ref[...] = x_ref[...] + gt_ref[0] * _dot(o, wo_ref[...])


def _oproj_sample_kernel(a_ref, x_ref, gt_ref, wo_ref, o_ref):
    o_ref[...] = x_ref[...] + gt_ref[0] * _dot(a_ref[...].astype(BF16), wo_ref[...])


def _out_proj(tok, attn, x, gate, wo):
    tm, nt = tok.tiles(512)
    if tok.per_row:
        kern, a_spec = _oproj_sample_kernel, tok.x_spec(tm)
    else:
        per_seq = tok.seq_len // tm
        kern = _oproj_prompt_kernel
        a_spec = pl.BlockSpec((1, D_MODEL, tm), lambda i: (i // per_seq, 0, i % per_seq))
    return pl.pallas_call(
        kern, grid=(nt,),
        in_specs=[a_spec, tok.x_spec(tm), tok.mod_spec(tm), _full_spec((D_MODEL, D_MODEL))],
        out_specs=tok.x_spec(tm),
        out_shape=jax.ShapeDtypeStruct((tok.n_tok, D_MODEL), F32),
        compiler_params=_params(1))(attn, x, tok.mod_arr(gate), wo)


def _final_kernel(x_ref, sh_ref, sc_ref, g_ref, o_ref):
    o_ref[...] = _ada_norm(x_ref[...], g_ref[...], sh_ref[0], sc_ref[0])


def _final(tok, x, shift, scale, g):
    tm, nt = tok.tiles(512)
    return pl.pallas_call(
        _final_kernel, grid=(nt,),
        in_specs=[tok.x_spec(tm), tok.mod_spec(tm), tok.mod_spec(tm), _full_spec((1, D_MODEL))],
        out_specs=tok.x_spec(tm),
        out_shape=jax.ShapeDtypeStruct((tok.n_tok, D_MODEL), F32),
        compiler_params=_params(1))(x, tok.mod_arr(shift), tok.mod_arr(scale), g.reshape(1, D_MODEL))


def _alibi_slopes():
    h = jnp.arange(1, N_HEADS + 1, dtype=F32)
    return jnp.exp2(-8.0 * h / N_HEADS)


def _cover(n_cmp, n_sel):
    c_start = jnp.arange(n_cmp)[:, None] * CMP_STRIDE
    s_start = jnp.arange(n_sel)[None, :] * SEL_BLOCK
    return ((c_start < s_start + SEL_BLOCK) & (c_start + CMP_LEN > s_start)).astype(BF16)


def _trunk(tok, x, mods, kv_mod, f_mod, wts, ctx):
    depth = wts['norm_g'].shape[0]
    n_a = depth // 2
    v_rows = []
    kv = None
    attn_ctx = None
    for l in range(depth):
        m = mods[l]
        if l == n_a:
            kv_out = _kv_proj(tok, x, kv_mod[:, 0], kv_mod[:, 1], wts['kv_norm_g'], wts['kv_w'], wts['kv_wvt'],
                              wts['alibi_k'])
            kv, attn_ctx = ctx['prepare'](kv_out)
        x = _ffn(tok, x, m[:, 0], m[:, 1], m[:, 2], wts['norm_g'][l, 0],
                 wts['ffn_w_gate'][l, 0], wts['ffn_w_up'][l, 0], wts['ffn_w_down'][l, 0])
        if l < n_a:
            x, v = _gmlp(tok, x, m[:, 3], m[:, 4], m[:, 5], wts['norm_g'][l, 1], wts['gmlp_w_uv'][l],
                         wts['gmlp_ln_g'][l], wts['gmlp_ln_b'][l], wts['gmlp_w_sp'][l], wts['gmlp_b_sp'][l],
                         wts['gmlp_w_out'][l])
            v_rows.append(v)
        else:
            j = l - n_a
            q, gates = _qg_proj(tok, x, m[:, 3], m[:, 4], wts['norm_g'][l, 1], wts['nsa_wq'][j],
                                wts['nsa_wg_t'][j], wts['nsa_wg'][j], wts['alibi_q'])
            attn = ctx['attend'](q, gates, attn_ctx)
            x = _out_proj(tok, attn, x, m[:, 5], wts['nsa_w_o'][j])
        x = _ffn(tok, x, m[:, 6], m[:, 7], m[:, 8], wts['norm_g'][l, 2],
                 wts['ffn_w_gate'][l, 1], wts['ffn_w_up'][l, 1], wts['ffn_w_down'][l, 1])
    y = _final(tok, x, f_mod[:, 0], f_mod[:, 1], wts['final_g'])
    return y, kv, v_rows


def kernel(x_prompt, x_sample, cache_kv, state_win_kv, page_table, c_prompt, c_sample, ada_w, ada_b, norm_g, ffn_w_gate, ffn_w_up, ffn_w_down, gmlp_w_uv, gmlp_ln_g, gmlp_ln_b, gmlp_w_sp, gmlp_b_sp, gmlp_w_out, nsa_w_qg, nsa_w_o, kv_norm_g, kv_ada_w, kv_ada_b, kv_w, cmp_w1, cmp_w2, cmp_pe, final_g, final_ada_w, final_ada_b):
    n_p, seq, _ = x_prompt.shape
    n_s, dec_seq, _ = x_sample.shape
    assert dec_seq == 1 and seq % (PAGES_PER_SEG * PAGE) == 0
    depth = ada_w.shape[0]
    n_b = nsa_w_qg.shape[0]
    gd = N_KV * HEAD_DIM
    n_pages = page_table.shape[1]
    assert n_pages == PAGES_PER_SEG
    past = n_pages * PAGE

    nq = N_HEADS * HEAD_DIM
    wg_cols = nsa_w_qg[:, :, nq:]
    wg_pad = jnp.pad(wg_cols.reshape(n_b, D_MODEL, N_KV, HPG * 3), ((0, 0), (0, 0), (0, 0), (0, 16 - HPG * 3)))
    wts = dict(
        norm_g=norm_g, kv_norm_g=kv_norm_g, final_g=final_g,
        ffn_w_gate=ffn_w_gate.astype(BF16), ffn_w_up=ffn_w_up.astype(BF16), ffn_w_down=ffn_w_down.astype(BF16),
        gmlp_w_uv=gmlp_w_uv.astype(BF16), gmlp_ln_g=gmlp_ln_g, gmlp_ln_b=gmlp_ln_b,
        gmlp_w_sp=gmlp_w_sp, gmlp_b_sp=gmlp_b_sp, gmlp_w_out=gmlp_w_out.astype(BF16),
        nsa_wq=nsa_w_qg[:, :, :nq].astype(BF16),
        nsa_wg_t=jnp.swapaxes(wg_pad.reshape(n_b, D_MODEL, N_KV * 16), 1, 2).astype(BF16),
        nsa_wg=jnp.pad(wg_cols, ((0, 0), (0, 0), (0, LANE - N_HEADS * 3))).astype(BF16),
        nsa_w_o=nsa_w_o.astype(BF16),
        kv_w=kv_w.astype(BF16),
        kv_wvt=jnp.concatenate([kv_w[:, 3 * gd:4 * gd], kv_w[:, 5 * gd:6 * gd]], axis=1).T.astype(BF16),
    )
    w1 = cmp_w1.astype(BF16)
    w2 = cmp_w2.astype(BF16)
    w2t = jnp.swapaxes(cmp_w2, 1, 2).astype(BF16)
    pe = jnp.broadcast_to(cmp_pe.astype(BF16).reshape(2, 1, CMP_LEN * HEAD_DIM), (2, 16, CMP_LEN * HEAD_DIM))
    slopes = _alibi_slopes()
    sl2 = slopes * LOG2E
    off_f = jnp.arange(LANE, dtype=F32)

    def split3(x):
        a = x.astype(BF16)
        b = (x - a.astype(F32)).astype(BF16)
        c = (x - a.astype(F32) - b.astype(F32)).astype(BF16)
        return [a, b, c]

    q_cols = split3(-sl2[:, None] * off_f[None, :]) + \
        [jnp.broadcast_to(c[:, None], (N_HEADS, LANE)) for c in split3(sl2)] + \
        [jnp.broadcast_to(c[:, None], (N_HEADS, LANE)) for c in split3(sl2 * CMP_STRIDE)]
    wts['alibi_q'] = jnp.pad(jnp.stack(q_cols, axis=-1), ((0, 0), (0, 0), (0, HEAD_DIM - 9)))
    ones, zeros, offs = jnp.ones((LANE,), BF16), jnp.zeros((LANE,), BF16), off_f.astype(BF16)
    wts['alibi_k'] = jnp.pad(jnp.stack([ones] * 3 + [offs] * 3 + [zeros] * 3, axis=-1), ((0, 0), (0, HEAD_DIM - 9)))
    alibi_kc = jnp.pad(jnp.stack([ones] * 3 + [zeros] * 3 + [offs] * 3, axis=-1), ((0, 0), (0, HEAD_DIM - 9)))

    n_c = n_p + n_s
    c_all = jnp.pad(jnp.concatenate([c_prompt, c_sample], axis=0), ((0, (-n_c) % 8), (0, 0)))
    mod_all = _mod_linear(c_all, ada_w, ada_b)
    kv_mod_all = _mod_linear(c_all, kv_ada_w[None], kv_ada_b[None])[0]
    f_mod_all = _mod_linear(c_all, final_ada_w[None], final_ada_b[None])[0]

    def rows(a, lo, hi, k):
        return a[..., lo:hi, :].reshape(a.shape[:-2] + (hi - lo, k, D_MODEL))

    tok_p = _Tok(n_p, seq, per_row=False)
    n_seg = seq // (PAGES_PER_SEG * PAGE)
    n_qt = seq // LANE
    ncp = n_seg * PAGES_PER_SEG * 8
    n_sel_p = seq // SEL_BLOCK
    assert n_sel_p <= LANE
    cov_t = jnp.pad(_cover(ncp, n_sel_p).T, ((0, LANE - n_sel_p), (0, 0)))
    sl_rows = jnp.repeat(sl2.reshape(N_KV, 1, HPG), LANE, axis=2)
    onehot = (jnp.arange(LANE)[None, None, :] ==
              (2 * jnp.arange(n_qt)[:, None, None] + jnp.arange(LANE)[None, :, None] // SEL_BLOCK)).astype(BF16)

    def prepare_p(kv_out):
        kv, k_nat, v_t = kv_out
        pages_per_seq = seq // PAGE
        base = jnp.arange(n_p * n_seg, dtype=jnp.int32)[:, None] * PAGES_PER_SEG
        table = jnp.minimum(base + jnp.arange(PAGES_PER_SEG + 1, dtype=jnp.int32)[None, :],
                            n_p * pages_per_seq - 1)
        kc, vct = _compress(kv.reshape(n_p * pages_per_seq, PAGE, N_KV_PROJ * gd), table, w1, w2, w2t, pe,
                            alibi_kc, n_p, n_seg)
        return kv, (kc, vct, k_nat, v_t)

    def attend_p(q, gates, c):
        kc, vct, k_nat, v_t = c
        return _nsa_prompt(q, gates, kc, vct, k_nat, v_t, cov_t, onehot, sl_rows, n_p, seq)

    y_p, kv_p, _ = _trunk(tok_p, x_prompt.reshape(n_p * seq, D_MODEL),
                          rows(mod_all, 0, n_p, 9), rows(kv_mod_all, 0, n_p, 2), rows(f_mod_all, 0, n_p, 2),
                          wts, dict(prepare=prepare_p, attend=attend_p))
    kv_p = kv_p.reshape(n_p, seq, N_KV_PROJ, N_KV, HEAD_DIM)
    kv_prompt = kv_p[:, :, :4]
    win_prompt = kv_p[:, seq - min(WINDOW, seq):, 4:6]

    tok_s = _Tok(n_s, 1, per_row=True)
    cache_pages = cache_kv.reshape(cache_kv.shape[0], PAGE, 4 * gd)
    buf = state_win_kv.shape[1]
    win_flat = state_win_kv.reshape(n_s, buf, 2 * gd)
    n_sel_s = (past + 1 + SEL_BLOCK - 1) // SEL_BLOCK
    cov_s = jnp.pad(_cover(LANE, n_sel_s).T, ((0, LANE - n_sel_s), (0, 0)))
    n_keys = (n_pages + 1) * LANE
    expand_s = (jnp.arange(LANE)[:, None] == (jnp.arange(n_keys)[None, :] // SEL_BLOCK)).astype(BF16)
    slope16 = jnp.broadcast_to(slopes[:, None], (N_HEADS, LANE))

    def prepare_s(kv):
        table = jnp.concatenate([page_table, page_table[:, -1:]], axis=1)
        kc, vct = _compress(cache_pages, table, w1, w2, w2t, pe, alibi_kc, n_s, 1)
        return kv, (kc, vct, kv)

    def attend_s(q, gates, c):
        kc, vct, kv = c
        o = _nsa_sample(q.reshape(n_s, N_HEADS, HEAD_DIM), gates[:, :N_HEADS * 3].reshape(n_s, N_HEADS, 3),
                        kc, vct, cache_pages, page_table, win_flat, kv.reshape(n_s, 1, N_KV_PROJ * gd),
                        cov_s, expand_s, slope16)
        return o.reshape(n_s, N_HEADS * HEAD_DIM)

    y_s, kv_s, v_rows = _trunk(tok_s, x_sample.reshape(n_s, D_MODEL),
                               rows(mod_all, n_p, n_c, 9), rows(kv_mod_all, n_p, n_c, 2),
                               rows(f_mod_all, n_p, n_c, 2), wts, dict(prepare=prepare_s, attend=attend_s))
    kv_s = kv_s.reshape(n_s, 1, N_KV_PROJ, N_KV, HEAD_DIM)
    kv_sample = kv_s[:, :, :4]
    win_sample = jnp.concatenate([state_win_kv, kv_s[:, :, 4:6]], axis=1)[:, 1:]
    gmlp_v_sample = jnp.stack([v.reshape(n_s, 1, D_V) for v in v_rows])

    return (y_p.reshape(n_p, seq, D_MODEL), y_s.reshape(n_s, 1, D_MODEL), kv_prompt, kv_sample,
            win_prompt, win_sample, gmlp_v_sample)
```

```python
import functools

import jax
import jax.numpy as jnp
from jax import lax
from jax.experimental import pallas as pl
from jax.experimental.pallas import tpu as pltpu

F32 = jnp.float32
BF16 = jnp.bfloat16

D_MODEL = 1024
D_FF = 2816
D_V = 3072
CHUNK = 128
N_GROUPS_A = 8
CG = D_V // N_GROUPS_A
N_HEADS = 16
HEAD_DIM = 64
N_KV = 4
HPG = N_HEADS // N_KV
CMP_LEN = 32
CMP_STRIDE = 16
CMP_HID = 256
SEL_BLOCK = 64
N_SEL = 16
WINDOW = 512
PAGE = 128
N_KV_PROJ = 6
EPS = 1e-6
NEG = -1e30
M_FLOOR = -1e29
PICKED = -3e38
FORCE_BONUS = 1e4
LOG2E = 1.4426950408889634
MASK_OFF = 1e30

LANE = 128
VMEM_LIMIT = 56 * 1024 * 1024
FF_CHUNK = 256
UV_CHUNK = 512
VROWS = HEAD_DIM + 16
N_NEAR = 8
PAGES_PER_SEG = 16
XROWS = PAGES_PER_SEG * 8 + 8


def _dot(a, b):
    return jnp.dot(a, b, preferred_element_type=F32)


def _dot_nt(a, b):
    return lax.dot_general(a, b, (((1,), (1,)), ((), ())), preferred_element_type=F32)


def _split_bf16(x):
    hi = x.astype(BF16)
    lo = (x - hi.astype(F32)).astype(BF16)
    return hi, lo


def _params(n_grid):
    return pltpu.CompilerParams(dimension_semantics=("arbitrary",) * n_grid,
                                vmem_limit_bytes=VMEM_LIMIT)


def _full_spec(shape):
    nd = len(shape)
    return pl.BlockSpec(shape, lambda *_: (0,) * nd, pipeline_mode=pl.Buffered(1))


def _ada_norm(x, g, shift, scale):
    ms = jnp.mean(x * x, axis=-1, keepdims=True)
    h = x * lax.rsqrt(ms + EPS) * g
    return h * (1.0 + scale) + shift


def _mod_kernel(c_ref, w_ref, b_ref, o_ref):
    c = c_ref[...]
    a = c * jax.nn.sigmoid(c)
    w = w_ref[0]
    a_hi, a_lo = _split_bf16(a)
    w_hi, w_lo = _split_bf16(w)
    o_ref[0] = _dot(a_hi, w_hi) + _dot(a_hi, w_lo) + _dot(a_lo, w_hi) + b_ref[0]


def _mod_linear(c, w, b):
    n_l, d, n = w.shape
    m = c.shape[0]
    tn = 1024
    return pl.pallas_call(
        _mod_kernel,
        grid=(n_l, n // tn),
        in_specs=[pl.BlockSpec((m, d), lambda l, j: (0, 0)),
                  pl.BlockSpec((1, d, tn), lambda l, j: (l, 0, j)),
                  pl.BlockSpec((1, 1, tn), lambda l, j: (l, 0, j))],
        out_specs=pl.BlockSpec((1, m, tn), lambda l, j: (l, 0, j)),
        out_shape=jax.ShapeDtypeStruct((n_l, m, n), F32),
        compiler_params=_params(2),
    )(c, w, b.reshape(n_l, 1, n))


class _Tok:
    def __init__(self, n_seq, seq_len, per_row):
        self.n_seq, self.seq_len = n_seq, seq_len
        self.n_tok = n_seq * seq_len
        self.per_row = per_row

    def tiles(self, tm):
        if self.per_row:
            return self.n_tok, 1
        assert self.seq_len % tm == 0
        return tm, self.n_tok // tm

    def x_spec(self, tm, width=D_MODEL):
        return pl.BlockSpec((tm, width), lambda i: (i, 0))

    def mod_spec(self, tm):
        if self.per_row:
            return pl.BlockSpec((1, tm, D_MODEL), lambda i: (0, 0, 0))
        per_seq = self.seq_len // tm
        return pl.BlockSpec((1, 1, D_MODEL), lambda i: (i // per_seq, 0, 0))

    def mod_arr(self, m):
        if self.per_row:
            return m.reshape(1, self.n_tok, D_MODEL)
        return m.reshape(self.n_seq, 1, D_MODEL)


def _ffn_kernel(x_ref, sh_ref, sc_ref, gt_ref, g_ref, wg_ref, wu_ref, wd_ref, o_ref, a_scr):
    x = x_ref[...]
    hb = _ada_norm(x, g_ref[...], sh_ref[0], sc_ref[0]).astype(BF16)
    for c in range(D_FF // FF_CHUNK):
        cs = slice(c * FF_CHUNK, (c + 1) * FF_CHUNK)
        gate = _dot(hb, wg_ref[:, cs])
        up = _dot(hb, wu_ref[:, cs])
        a_scr[:, cs] = (gate * jax.nn.sigmoid(gate) * up).astype(BF16)
    y = _dot(a_scr[...], wd_ref[...])
    o_ref[...] = x + (0.5 * gt_ref[0]) * y


def _ffn(tok, x, shift, scale, gate, g, wg, wu, wd):
    tm, nt = tok.tiles(512)
    return pl.pallas_call(
        _ffn_kernel,
        grid=(nt,),
        in_specs=[tok.x_spec(tm), tok.mod_spec(tm), tok.mod_spec(tm), tok.mod_spec(tm),
                  _full_spec((1, D_MODEL)), _full_spec((D_MODEL, D_FF)),
                  _full_spec((D_MODEL, D_FF)), _full_spec((D_FF, D_MODEL))],
        out_specs=tok.x_spec(tm),
        out_shape=jax.ShapeDtypeStruct((tok.n_tok, D_MODEL), F32),
        scratch_shapes=[pltpu.VMEM((tm, D_FF), BF16)],
        compiler_params=_params(1),
    )(x, tok.mod_arr(shift), tok.mod_arr(scale), tok.mod_arr(gate), g.reshape(1, D_MODEL), wg, wu, wd)


def _gmlp_uv(x_ref, sh_ref, sc_ref, g_ref, wuv_ref, lng_ref, lnb_ref, u_scr, v_scr):
    hb = _ada_norm(x_ref[...], g_ref[...], sh_ref[0], sc_ref[0]).astype(BF16)
    for c in range(D_V // UV_CHUNK):
        cs = slice(c * UV_CHUNK, (c + 1) * UV_CHUNK)
        cv = slice(D_V + c * UV_CHUNK, D_V + (c + 1) * UV_CHUNK)
        u_scr[:, cs] = jax.nn.gelu(_dot(hb, wuv_ref[:, cs]))
        v_scr[:, cs] = jax.nn.gelu(_dot(hb, wuv_ref[:, cv]))
    v = v_scr[...]
    mu = jnp.mean(v, axis=-1, keepdims=True)
    vc = v - mu
    var = jnp.mean(vc * vc, axis=-1, keepdims=True)
    return vc * lax.rsqrt(var + EPS) * lng_ref[...] + lnb_ref[...]


def _gmlp_prompt_kernel(x_ref, sh_ref, sc_ref, gt_ref, g_ref, wuv_ref, lng_ref, lnb_ref,
                        wsp_ref, bsp_ref, wout_ref, o_ref, u_scr, v_scr, a_scr):
    v_scr[...] = _gmlp_uv(x_ref, sh_ref, sc_ref, g_ref, wuv_ref, lng_ref, lnb_ref, u_scr, v_scr)
    row = lax.broadcasted_iota(jnp.int32, (CHUNK, CHUNK), 0)
    col = lax.broadcasted_iota(jnp.int32, (CHUNK, CHUNK), 1)
    tm = x_ref.shape[0]
    for grp in range(N_GROUPS_A):
        w = jnp.where(row >= col, wsp_ref[grp], 0.0).astype(BF16)
        cs = slice(grp * CG, (grp + 1) * CG)
        for n in range(tm // CHUNK):
            rs = slice(n * CHUNK, (n + 1) * CHUNK)
            s = _dot(w, v_scr[rs, cs].astype(BF16)) + bsp_ref[:, cs]
            a_scr[rs, cs] = (u_scr[rs, cs] * s).astype(BF16)
    y = _dot(a_scr[...], wout_ref[...])
    o_ref[...] = x_ref[...] + gt_ref[0] * y


def _gmlp_sample_kernel(x_ref, sh_ref, sc_ref, gt_ref, g_ref, wuv_ref, lng_ref, lnb_ref,
                        wrow_ref, brow_ref, wout_ref, o_ref, vout_ref, u_scr, v_scr):
    vn = _gmlp_uv(x_ref, sh_ref, sc_ref, g_ref, wuv_ref, lng_ref, lnb_ref, u_scr, v_scr)
    vout_ref[...] = vn
    s = vn * wrow_ref[...] + brow_ref[...]
    y = _dot((u_scr[...] * s).astype(BF16), wout_ref[...])
    o_ref[...] = x_ref[...] + gt_ref[0] * y


def _gmlp(tok, x, shift, scale, gate, g, wuv, ln_g, ln_b, w_sp, b_sp, wout):
    tm, nt = tok.tiles(512)
    common = [tok.x_spec(tm), tok.mod_spec(tm), tok.mod_spec(tm), tok.mod_spec(tm),
              _full_spec((1, D_MODEL)), _full_spec((D_MODEL, 2 * D_V)),
              _full_spec((1, D_V)), _full_spec((1, D_V))]
    args = [x, tok.mod_arr(shift), tok.mod_arr(scale), tok.mod_arr(gate), g.reshape(1, D_MODEL), wuv,
            ln_g.reshape(1, D_V), ln_b.reshape(1, D_V)]
    x_shape = jax.ShapeDtypeStruct((tok.n_tok, D_MODEL), F32)
    if tok.per_row:
        wrow = jnp.repeat(w_sp[:, 0, 0], CG).reshape(1, D_V)
        brow = jnp.repeat(b_sp[:, 0], CG).reshape(1, D_V)
        return pl.pallas_call(
            _gmlp_sample_kernel,
            grid=(nt,),
            in_specs=common + [_full_spec((1, D_V)), _full_spec((1, D_V)), _full_spec((D_V, D_MODEL))],
            out_specs=[tok.x_spec(tm), tok.x_spec(tm, D_V)],
            out_shape=[x_shape, jax.ShapeDtypeStruct((tok.n_tok, D_V), F32)],
            scratch_shapes=[pltpu.VMEM((tm, D_V), F32), pltpu.VMEM((tm, D_V), F32)],
            compiler_params=_params(1),
        )(*args, wrow, brow, wout)
    bias = jnp.repeat(b_sp.T, CG, axis=1)
    out = pl.pallas_call(
        _gmlp_prompt_kernel,
        grid=(nt,),
        in_specs=common + [_full_spec((N_GROUPS_A, CHUNK, CHUNK)), _full_spec((CHUNK, D_V)),
                           _full_spec((D_V, D_MODEL))],
        out_specs=tok.x_spec(tm),
        out_shape=x_shape,
        scratch_shapes=[pltpu.VMEM((tm, D_V), F32), pltpu.VMEM((tm, D_V), F32),
                        pltpu.VMEM((tm, D_V), BF16)],
        compiler_params=_params(1),
    )(*args, w_sp, bias, wout)
    return out, None


def _kv_prompt_kernel(x_ref, sh_ref, sc_ref, g_ref, w_ref, wvt_ref, ak_ref, kv_ref, k_ref, vt_ref):
    hb = _ada_norm(x_ref[...], g_ref[...], sh_ref[0], sc_ref[0]).astype(BF16)
    kv = _dot(hb, w_ref[...])
    kv_ref[...] = kv
    vt = _dot_nt(wvt_ref[...], hb).astype(BF16)
    tm = x_ref.shape[0]
    gd = N_KV * HEAD_DIM
    ones_rows = jnp.where(lax.broadcasted_iota(jnp.int32, (VROWS - HEAD_DIM, LANE), 0) == 0, 1.0, 0.0).astype(BF16)
    for c in range(tm // LANE):
        rs = slice(c * LANE, (c + 1) * LANE)
        for hd in range(2 * N_KV):
            vt_ref[0, c, hd * VROWS:hd * VROWS + HEAD_DIM, :] = vt[hd * HEAD_DIM:(hd + 1) * HEAD_DIM, rs]
            vt_ref[0, c, hd * VROWS + HEAD_DIM:(hd + 1) * VROWS, :] = ones_rows
        for j, slot in enumerate((2, 4)):
            for grp in range(N_KV):
                col = slot * gd + grp * HEAD_DIM
                k_ref[0, j * N_KV + grp, c, :, 0:HEAD_DIM] = kv[rs, col:col + HEAD_DIM].astype(BF16)
                k_ref[0, j * N_KV + grp, c, :, HEAD_DIM:LANE] = ak_ref[...]


def _kv_sample_kernel(x_ref, sh_ref, sc_ref, g_ref, w_ref, kv_ref):
    hb = _ada_norm(x_ref[...], g_ref[...], sh_ref[0], sc_ref[0]).astype(BF16)
    kv_ref[...] = _dot(hb, w_ref[...])


def _kv_proj(tok, x, shift, scale, g, w, wvt, alibi_k):
    tm, nt = tok.tiles(512)
    n_kv = N_KV_PROJ * N_KV * HEAD_DIM
    in_specs = [tok.x_spec(tm), tok.mod_spec(tm), tok.mod_spec(tm), _full_spec((1, D_MODEL)),
                _full_spec((D_MODEL, n_kv))]
    args = [x, tok.mod_arr(shift), tok.mod_arr(scale), g.reshape(1, D_MODEL), w]
    kv_shape = jax.ShapeDtypeStruct((tok.n_tok, n_kv), F32)
    if tok.per_row:
        return pl.pallas_call(
            _kv_sample_kernel, grid=(nt,), in_specs=in_specs, out_specs=tok.x_spec(tm, n_kv),
            out_shape=kv_shape, compiler_params=_params(1))(*args)
    per_seq = tok.seq_len // tm
    n_qt = tok.seq_len // LANE
    sub = tm // LANE
    return pl.pallas_call(
        _kv_prompt_kernel,
        grid=(nt,),
        in_specs=in_specs + [_full_spec((2 * N_KV * HEAD_DIM, D_MODEL)), _full_spec((LANE, HEAD_DIM))],
        out_specs=[tok.x_spec(tm, n_kv),
                   pl.BlockSpec((1, 2 * N_KV, sub, LANE, LANE),
                                lambda i: (i // per_seq, 0, i % per_seq, 0, 0)),
                   pl.BlockSpec((1, sub, 2 * N_KV * VROWS, LANE),
                                lambda i: (i // per_seq, i % per_seq, 0, 0))],
        out_shape=[kv_shape,
                   jax.ShapeDtypeStruct((tok.n_seq, 2 * N_KV, n_qt, LANE, LANE), BF16),
                   jax.ShapeDtypeStruct((tok.n_seq, n_qt, 2 * N_KV * VROWS, LANE), BF16)],
        compiler_params=_params(1),
    )(*args, wvt, alibi_k)


def _compress_kernel(tbl_ref, *refs):
    pages = refs[:PAGES_PER_SEG + 1]
    perm_ref, w1_ref, w2_ref, w2t_ref, pe_ref, akc_ref, kc_ref, vct_ref, x_scr, hb_scr = refs[PAGES_PER_SEG + 1:]
    del tbl_ref
    perm = perm_ref[...]
    n_heads = 2 * N_KV
    half = CMP_STRIDE * HEAD_DIM
    for p in range(PAGES_PER_SEG + 1):
        xp = _dot(perm, pages[p][0].astype(BF16))
        for hd in range(n_heads):
            cs = slice(hd * HEAD_DIM, (hd + 1) * HEAD_DIM)
            x_scr[hd, p * 8:(p + 1) * 8, :] = jnp.concatenate(
                [xp[s * 8:(s + 1) * 8, cs] for s in range(CMP_STRIDE)], axis=1)
    n_blk = PAGES_PER_SEG * 8
    for slot in range(2):
        x = x_scr[slot * N_KV:(slot + 1) * N_KV].reshape(N_KV * XROWS, half).astype(BF16)
        ha = _dot(x, w1_ref[slot, 0:half, :])
        hb_scr[...] = _dot(x, w1_ref[slot, half:2 * half, :])
        pe_hid = _dot(pe_ref[slot], w1_ref[slot])[0:1]
        for grp in range(N_KV):
            hid = ha[grp * XROWS:grp * XROWS + n_blk] + hb_scr[pl.ds(grp * XROWS + 1, n_blk), :] + pe_hid
            act = jax.nn.gelu(hid).astype(BF16)
            if slot == 0:
                kc_ref[0, grp, :, 0:HEAD_DIM] = _dot(act, w2_ref[0]).astype(BF16)
                kc_ref[0, grp, :, HEAD_DIM:LANE] = akc_ref[...]
            else:
                vct_ref[0, grp, 0:HEAD_DIM, :] = _dot_nt(w2t_ref[1], act).astype(BF16)
                vct_ref[0, grp, HEAD_DIM:VROWS, :] = jnp.where(
                    lax.broadcasted_iota(jnp.int32, (VROWS - HEAD_DIM, n_blk), 0) == 0, 1.0, 0.0).astype(BF16)


def _compress(src, table, w1, w2, w2t, pe, alibi_kc, n_seq, n_seg):
    n_blk = PAGES_PER_SEG * 8
    gd2 = 2 * N_KV * HEAD_DIM
    dst = jnp.arange(PAGE)
    src_row = (dst % 8) * CMP_STRIDE + dst // 8
    perm = (jnp.arange(PAGE)[None, :] == src_row[:, None]).astype(BF16)

    def page_spec(p):
        return pl.BlockSpec((1, PAGE, gd2), lambda i, tbl, p=p: (tbl[i, p], 0, 0))

    grid_spec = pltpu.PrefetchScalarGridSpec(
        num_scalar_prefetch=1,
        grid=(n_seq * n_seg,),
        in_specs=[page_spec(p) for p in range(PAGES_PER_SEG + 1)] + [
            pl.BlockSpec((PAGE, PAGE), lambda i, tbl: (0, 0)),
            pl.BlockSpec(w1.shape, lambda i, tbl: (0, 0, 0)),
            pl.BlockSpec(w2.shape, lambda i, tbl: (0, 0, 0)),
            pl.BlockSpec(w2t.shape, lambda i, tbl: (0, 0, 0)),
            pl.BlockSpec(pe.shape, lambda i, tbl: (0, 0, 0)),
            pl.BlockSpec((LANE, HEAD_DIM), lambda i, tbl: (0, 0))],
        out_specs=[pl.BlockSpec((1, N_KV, n_blk, LANE), lambda i, tbl: (i // n_seg, 0, i % n_seg, 0)),
                   pl.BlockSpec((1, N_KV, VROWS, n_blk), lambda i, tbl: (i // n_seg, 0, 0, i % n_seg))],
        scratch_shapes=[pltpu.VMEM((2 * N_KV, XROWS, CMP_STRIDE * HEAD_DIM), F32),
                        pltpu.VMEM((N_KV * XROWS, CMP_HID), F32)])
    return pl.pallas_call(
        _compress_kernel,
        grid_spec=grid_spec,
        out_shape=[jax.ShapeDtypeStruct((n_seq, N_KV, n_seg * n_blk, LANE), BF16),
                   jax.ShapeDtypeStruct((n_seq, N_KV, VROWS, n_seg * n_blk), BF16)],
        compiler_params=_params(1),
    )(table, *([src] * (PAGES_PER_SEG + 1)), perm, w1, w2, w2t, pe, alibi_kc)


def _qg_prompt_kernel(x_ref, sh_ref, sc_ref, g_ref, wq_ref, wgt_ref, aq_ref, q_ref, gt_ref):
    hb = _ada_norm(x_ref[...], g_ref[...], sh_ref[0], sc_ref[0]).astype(BF16)
    q = _dot(hb, wq_ref[...]) * (HEAD_DIM ** -0.5 * LOG2E)
    tm = x_ref.shape[0]
    for h in range(N_HEADS):
        q_ref[0, h, :, 0:HEAD_DIM] = q[:, h * HEAD_DIM:(h + 1) * HEAD_DIM].astype(BF16)
        for c in range(tm // LANE):
            q_ref[0, h, c * LANE:(c + 1) * LANE, HEAD_DIM:LANE] = aq_ref[h]
    gates = jax.nn.sigmoid(_dot_nt(wgt_ref[...], hb))
    for grp in range(N_KV):
        gt_ref[0, grp] = gates[grp * 16:(grp + 1) * 16, :]


def _qg_sample_kernel(x_ref, sh_ref, sc_ref, g_ref, wq_ref, wg_ref, q_ref, gt_ref):
    hb = _ada_norm(x_ref[...], g_ref[...], sh_ref[0], sc_ref[0]).astype(BF16)
    q_ref[...] = _dot(hb, wq_ref[...]) * (HEAD_DIM ** -0.5)
    gt_ref[...] = jax.nn.sigmoid(_dot(hb, wg_ref[...]))


def _qg_proj(tok, x, shift, scale, g, wq, wg_t, wg_nat, alibi_q):
    tm, nt = tok.tiles(512)
    in_specs = [tok.x_spec(tm), tok.mod_spec(tm), tok.mod_spec(tm), _full_spec((1, D_MODEL)),
                _full_spec((D_MODEL, D_MODEL))]
    args = [x, tok.mod_arr(shift), tok.mod_arr(scale), g.reshape(1, D_MODEL), wq]
    if tok.per_row:
        return pl.pallas_call(
            _qg_sample_kernel, grid=(nt,),
            in_specs=in_specs + [_full_spec((D_MODEL, LANE))],
            out_specs=[tok.x_spec(tm), tok.x_spec(tm, LANE)],
            out_shape=[jax.ShapeDtypeStruct((tok.n_tok, D_MODEL), F32),
                       jax.ShapeDtypeStruct((tok.n_tok, LANE), F32)],
            compiler_params=_params(1))(*args, wg_nat)
    per_seq = tok.seq_len // tm
    return pl.pallas_call(
        _qg_prompt_kernel, grid=(nt,),
        in_specs=in_specs + [_full_spec((N_KV * 16, D_MODEL)), _full_spec((N_HEADS, LANE, HEAD_DIM))],
        out_specs=[pl.BlockSpec((1, N_HEADS, tm, LANE), lambda i: (i // per_seq, 0, i % per_seq, 0)),
                   pl.BlockSpec((1, N_KV, 16, tm), lambda i: (i // per_seq, 0, 0, i % per_seq))],
        out_shape=[jax.ShapeDtypeStruct((tok.n_seq, N_HEADS, tok.seq_len, LANE), BF16),
                   jax.ShapeDtypeStruct((tok.n_seq, N_KV, 16, tok.seq_len), F32)],
        compiler_params=_params(1))(*args, wg_t, alibi_q)


def _select_blocks(work, idx_f, axis, rounds):
    for _ in range(rounds):
        mx = jnp.max(work, axis=axis, keepdims=True)
        first = jnp.min(jnp.where(work == mx, idx_f, 1e9), axis=axis, keepdims=True)
        first = jnp.where(mx > NEG / 2, first, -1.0)
        work = jnp.where(idx_f == first, PICKED, work)
        yield
    return jnp.where(work == PICKED, 1.0, 0.0)


def _softmax_group(scores, offsets, m_old, exp_dtype=F32):
    m_new = m_old
    for s, off in zip(scores, offsets):
        m_new = jnp.maximum(m_new, jnp.max(s, axis=0, keepdims=True) - off)
        yield
    alpha = jnp.exp2(m_old - m_new)
    probs = []
    for s, off in zip(scores, offsets):
        probs.append(jnp.exp2((s - (m_new + off)).astype(exp_dtype)))
        yield
    return m_new, alpha, probs


def _run(gen):
    try:
        while True:
            next(gen)
    except StopIteration as stop:
        return stop.value


def _interleave(*gens):
    live = list(gens)
    while live:
        for g in list(live):
            try:
                next(g)
            except StopIteration:
                live.remove(g)


def _normalise(acc):
    l = acc[HEAD_DIM:HEAD_DIM + 1]
    return acc[0:HEAD_DIM] * jnp.where(l > 0.0, 1.0 / l, 0.0)


def _nsa_prompt_kernel(q_ref, g_ref, kc_ref, vct_ref, ks_ref, vst_ref, kw_ref, vwt_ref, cov_ref, hot_ref, sl_ref,
                       o_ref, qp_scr, oc_scr, m_scr, acc_scr, bits_ref, *, ncp, n_qt):
    j = pl.program_id(2)
    cols = HPG * LANE
    key_io = lax.broadcasted_iota(jnp.int32, (LANE, LANE), 0)
    tok_io = lax.broadcasted_iota(jnp.int32, (LANE, LANE), 1)
    sl = sl_ref[0]
    m_init = jnp.full((1, cols), M_FLOOR, F32)

    def tile4(a):
        return jnp.concatenate([a] * HPG, axis=1)

    @pl.when(j == 0)
    def _():
        qp_scr[1] = jnp.zeros(qp_scr.shape[1:], BF16)
        oc_scr[1] = jnp.zeros(oc_scr.shape[1:], F32)
        for w in range(LANE // 32):
            bits_ref[LANE // 32 + w] = 0

    i = jnp.maximum(j - 1, 0)
    slot = (j + 1) % 2

    def sel_keys(jp):
        return jnp.concatenate([hot_ref[jp], ks_ref[0, 0, jp]], axis=1)

    fi_a = i.astype(F32)
    qp = qp_scr[slot]
    rel = tile4((tok_io - key_io).astype(F32))

    def window_branch():
        qx = qp[:, LANE:2 * LANE]

        n_wt = WINDOW // LANE
        scores, offsets, vts = [], [], []
        tiles = [jnp.maximum(i - n_wt + kt_i, 0) for kt_i in range(n_wt + 1)]
        s_all = _dot_nt(jnp.concatenate([kw_ref[0, 0, t] for t in tiles], axis=0), qx)
        yield
        for kt_i in range(n_wt + 1):
            jt = i - n_wt + kt_i
            jt_c = tiles[kt_i]
            s = s_all[kt_i * LANE:(kt_i + 1) * LANE]
            if kt_i == 0:
                s = jnp.where(rel <= 0.0, s, NEG)
            elif kt_i == n_wt:
                s = jnp.where(rel >= 0.0, s, NEG)
            scores.append(s)
            offsets.append(sl * float(LANE * (n_wt - kt_i)) + jnp.where(jt >= 0, 0.0, MASK_OFF).astype(F32))
            vts.append(vwt_ref[0, jt_c])
        _, _, probs = yield from _softmax_group(scores, offsets, m_init, BF16)
        p = jnp.concatenate(probs, axis=0)
        return _normalise(_dot(jnp.concatenate(vts, axis=1), p))

    def near_tiles():
        fi = fi_a
        tiles = [0, i] + [jnp.maximum(i - r, 0) for r in range(1, N_NEAR + 1)]
        s_all = _dot_nt(jnp.concatenate([sel_keys(t) for t in tiles], axis=0), qp)
        yield
        scores = [s_all[k * LANE:(k + 1) * LANE] for k in range(len(tiles))]
        scores[1] = jnp.where(rel >= 0.0, scores[1], NEG)
        offsets = [sl * (LANE * fi) + jnp.where(i > N_NEAR, 0.0, MASK_OFF).astype(F32), jnp.zeros((1, cols), F32)]
        for r in range(1, N_NEAR + 1):
            offsets.append(sl * float(LANE * r) + jnp.where(i - r >= 0, 0.0, MASK_OFF).astype(F32))
        vts = [vst_ref[0, t] for t in tiles]
        m, _, probs = yield from _softmax_group(scores, offsets, m_init, BF16)
        p = jnp.concatenate(probs, axis=0)
        m_scr[...] = m
        acc_scr[...] = _dot(jnp.concatenate(vts, axis=1), p)

    def selection_half():
        i_s = jnp.minimum(j, n_qt - 1)
        slot_s = j % 2
        fi = i_s.astype(F32)
        qx = q_ref[0].reshape(cols, LANE)

        n_chunks = ncp // LANE
        rel_c = tile4((tok_io - CMP_STRIDE * key_io).astype(F32))
        scores, offsets = [], []
        s_all = _dot_nt(kc_ref[0, 0], qx)
        yield
        for c in range(n_chunks):
            base = float(CMP_STRIDE * LANE * c + CMP_LEN - 1)
            s_c = s_all[c * LANE:(c + 1) * LANE]
            scores.append(jnp.where(rel_c >= base - LANE * fi, s_c, NEG))
            offsets.append(sl * (LANE * fi - base))
        _, _, probs = yield from _softmax_group(scores, offsets, m_init)
        p_hi, p_lo = _split_bf16(jnp.concatenate(probs, axis=0))
        acc_c = _dot(vct_ref[0, 0], p_hi)
        oc_scr[slot_s] = _normalise(acc_c)
        yield
        l_c = acc_c[HEAD_DIM:HEAD_DIM + 1]
        cov = cov_ref[...]
        imp4 = (_dot(cov, p_hi) + _dot(cov, p_lo)) * jnp.where(l_c > 0.0, 1.0 / l_c, 0.0)
        imp = imp4[:, 0:LANE]
        for h in range(1, HPG):
            imp = imp + imp4[:, h * LANE:(h + 1) * LANE]
        yield

        q_blk = 2 * i_s + tok_io // SEL_BLOCK
        forced = (key_io == 0) | (key_io == q_blk) | (key_io == q_blk - 1)
        valid = key_io <= q_blk
        work = jnp.where(valid & jnp.logical_not(forced), imp, NEG)
        sel = yield from _select_blocks(work, key_io.astype(F32), 0, N_SEL - 3)
        sel = jnp.where(valid & forced, 1.0, sel)
        sel_bias_t = jnp.where(sel.T > 0.5, 0.0, NEG).astype(BF16)
        for h in range(HPG):
            qp_scr[slot_s, h * LANE:(h + 1) * LANE, 0:LANE] = sel_bias_t
            qp_scr[slot_s, h * LANE:(h + 1) * LANE, LANE:2 * LANE] = q_ref[0, h]

        cnt = _dot_nt(jnp.ones((8, LANE), BF16), sel.astype(BF16))[0:1]
        lane = lax.broadcasted_iota(jnp.int32, (1, LANE), 1)
        live = (cnt > 0.0) & (lane >= 2) & (lane < 2 * (i_s - N_NEAR))
        for w in range(LANE // 32):
            bits_ref[slot_s * (LANE // 32) + w] = jnp.sum(
                jnp.where(live & (lane // 32 == w), jnp.left_shift(1, lane % 32), 0))

    box = []

    def window_result():
        box.append((yield from window_branch()))

    _interleave(window_result(), near_tiles(), selection_half())
    o_win = box[0]

    def group_body(k, carry):
        word = bits_ref[slot * (LANE // 32) + k // 4]
        used = jnp.right_shift(word, (8 * k) % 32) & 255

        @pl.when(used != 0)
        def _():
            offsets, vts = [], []
            s_all = _dot_nt(jnp.concatenate([sel_keys(4 * k + r) for r in range(4)], axis=0), qp_scr[slot])
            scores = [s_all[r * LANE:(r + 1) * LANE] for r in range(4)]
            for r in range(4):
                jp = 4 * k + r
                dead = jnp.where((jp == 0) | (jp >= i - N_NEAR), MASK_OFF, 0.0).astype(F32)
                offsets.append(sl * (LANE * (i - jp)).astype(F32) + dead)
                vts.append(vst_ref[0, jp])
            m, alpha, probs = _run(_softmax_group(scores, offsets, m_scr[...], BF16))
            p = jnp.concatenate(probs, axis=0)
            m_scr[...] = m
            acc_scr[...] = alpha * acc_scr[...] + _dot(jnp.concatenate(vts, axis=1), p)
        return carry

    lax.fori_loop(0, jnp.maximum(i - N_NEAR + 3, 0) // 4, group_body, 0)
    o_sel = _normalise(acc_scr[...])

    o_cmp = oc_scr[slot]
    for h in range(HPG):
        cs = slice(h * LANE, (h + 1) * LANE)
        g_c = g_ref[0, 0, 3 * h:3 * h + 1, :]
        g_s = g_ref[0, 0, 3 * h + 1:3 * h + 2, :]
        g_w = g_ref[0, 0, 3 * h + 2:3 * h + 3, :]
        o_ref[0, h * HEAD_DIM:(h + 1) * HEAD_DIM, :] = g_c * o_cmp[:, cs] + g_s * o_sel[:, cs] + g_w * o_win[:, cs]


def _nsa_prompt(q, gates_t, kc, vct, k_ext, v_t, cov_t, onehot, sl_rows, n_seq, seq_len):
    n_qt = seq_len // LANE
    ncp = kc.shape[2]
    assert n_qt % 4 == 0
    kern = functools.partial(_nsa_prompt_kernel, ncp=ncp, n_qt=n_qt)
    cols = HPG * LANE
    last = n_qt - 1

    def sel_tile(j):
        return jnp.minimum(j, last)

    def att_tile(j):
        return jnp.maximum(j - 1, 0)

    return pl.pallas_call(
        kern,
        grid=(n_seq, N_KV, n_qt + 1),
        in_specs=[pl.BlockSpec((1, HPG, LANE, LANE), lambda b, g, j: (b, g, sel_tile(j), 0)),
                  pl.BlockSpec((1, 1, 16, LANE), lambda b, g, j: (b, g, 0, att_tile(j))),
                  pl.BlockSpec((1, 1, ncp, LANE), lambda b, g, j: (b, g, 0, 0)),
                  pl.BlockSpec((1, 1, VROWS, ncp), lambda b, g, j: (b, g, 0, 0)),
                  pl.BlockSpec((1, 1, n_qt, LANE, LANE), lambda b, g, j: (b, g, 0, 0, 0)),
                  pl.BlockSpec((1, n_qt, VROWS, LANE), lambda b, g, j: (b, 0, g, 0)),
                  pl.BlockSpec((1, 1, n_qt, LANE, LANE), lambda b, g, j: (b, N_KV + g, 0, 0, 0)),
                  pl.BlockSpec((1, n_qt, VROWS, LANE), lambda b, g, j: (b, 0, N_KV + g, 0)),
                  pl.BlockSpec((LANE, ncp), lambda b, g, j: (0, 0)),
                  pl.BlockSpec((n_qt, LANE, LANE), lambda b, g, j: (0, 0, 0)),
                  pl.BlockSpec((1, 1, cols), lambda b, g, j: (g, 0, 0))],
        out_specs=pl.BlockSpec((1, HPG * HEAD_DIM, LANE), lambda b, g, j: (b, g, att_tile(j))),
        out_shape=jax.ShapeDtypeStruct((n_seq, N_HEADS * HEAD_DIM, seq_len), F32),
        scratch_shapes=[pltpu.VMEM((2, cols, 2 * LANE), BF16),
                        pltpu.VMEM((2, HEAD_DIM, cols), F32),
                        pltpu.VMEM((1, cols), F32),
                        pltpu.VMEM((VROWS, cols), F32),
                        pltpu.SMEM((2 * (LANE // 32),), jnp.int32)],
        compiler_params=_params(3),
    )(q, gates_t, kc, vct, k_ext, v_t, k_ext, v_t, cov_t, onehot, sl_rows)


def _nsa_sample_kernel(tbl_ref, *refs, n_pages):
    pages = refs[:n_pages]
    (q_ref, g_ref, kc_ref, vct_ref, win_ref, new_ref, cov_ref, exp_ref, slope_ref,
     o_ref, s_scr) = refs[n_pages:]
    del tbl_ref
    gd = N_KV * HEAD_DIM
    past = n_pages * PAGE
    q_pos = float(past)
    qb = q_ref[0].astype(BF16)
    slope = slope_ref[...]
    row_grp = lax.broadcasted_iota(jnp.int32, (N_HEADS, LANE), 0) // HPG
    lane_f = lax.broadcasted_iota(jnp.int32, (N_HEADS, LANE), 1).astype(F32)
    new = new_ref[0]
    first_row = lax.broadcasted_iota(jnp.int32, (LANE, HEAD_DIM), 0) == 0

    def own_rows(g_sel, pick):
        out = pick(0)
        for grp in range(1, N_KV):
            out = jnp.where(g_sel == grp, pick(grp), out)
        return out

    def new_tile(slot, grp):
        col = slot * gd + grp * HEAD_DIM
        return jnp.where(first_row, new[:, col:col + HEAD_DIM], 0.0).astype(BF16)

    def softmax_rows(s, valid):
        s = jnp.where(valid, s, NEG)
        m = jnp.max(s, axis=1, keepdims=True)
        p = jnp.where(valid, jnp.exp(s - m), 0.0)
        l = jnp.sum(p, axis=1, keepdims=True)
        return p * jnp.where(l > 0.0, 1.0 / l, 0.0)

    d_c = q_pos - (CMP_STRIDE * lane_f + (CMP_LEN - 1))
    s_c = own_rows(row_grp, lambda grp: _dot_nt(qb, kc_ref[0, grp][:, 0:HEAD_DIM])) - slope * d_c
    p_c = softmax_rows(s_c, d_c >= 0.0)
    p_cb = p_c.astype(BF16)
    row_grp_o = lax.broadcasted_iota(jnp.int32, (N_HEADS, HEAD_DIM), 0) // HPG
    o_c = own_rows(row_grp_o, lambda grp: _dot_nt(p_cb, vct_ref[0, grp, 0:HEAD_DIM, :]))

    p_grp = own_rows(row_grp, lambda grp: jnp.broadcast_to(
        jnp.sum(p_c[grp * HPG:(grp + 1) * HPG], axis=0, keepdims=True), (N_HEADS, LANE)))
    p_hi, p_lo = _split_bf16(jnp.concatenate([p_grp, jnp.zeros((LANE - N_HEADS, LANE), F32)], axis=0))
    cov_t = cov_ref[...]
    imp_t = _dot_nt(cov_t, p_hi) + _dot_nt(cov_t, p_lo)
    blk = lax.broadcasted_iota(jnp.int32, (LANE, LANE), 0)
    q_blk = past // SEL_BLOCK
    forced = (blk == 0) | (blk == q_blk) | (blk == q_blk - 1)
    valid = blk <= q_blk
    work = jnp.where(valid & jnp.logical_not(forced), imp_t, NEG)
    sel_t = _run(_select_blocks(work, blk.astype(F32), 0, N_SEL - 3))
    sel = jnp.where(valid & forced, 1.0, sel_t).T[0:N_HEADS]

    n_t = n_pages + 1
    for t in range(n_t):
        if t < n_pages:
            tile = lambda grp, t=t: pages[t][0, :, grp * HEAD_DIM:(grp + 1) * HEAD_DIM].astype(BF16)
        else:
            tile = lambda grp: new_tile(2, grp)
        s_scr[:, t * LANE:(t + 1) * LANE] = own_rows(row_grp, lambda grp: _dot_nt(qb, tile(grp)))
    width = n_t * LANE
    kpos = lax.broadcasted_iota(jnp.int32, (N_HEADS, width), 1).astype(F32)
    d_s = q_pos - kpos
    chosen = _dot(sel.astype(BF16), exp_ref[...])
    slope_w = jnp.concatenate([slope] * n_t, axis=1)
    p_s = softmax_rows(s_scr[...] - slope_w * d_s, (chosen > 0.5) & (d_s >= 0.0)).astype(BF16)
    o_s = jnp.zeros((N_HEADS, HEAD_DIM), F32)
    for t in range(n_t):
        pt = p_s[:, t * LANE:(t + 1) * LANE]
        if t < n_pages:
            vtile = lambda grp, t=t: pages[t][0, :, gd + grp * HEAD_DIM: gd + (grp + 1) * HEAD_DIM].astype(BF16)
        else:
            vtile = lambda grp: new_tile(3, grp)
        o_s = o_s + own_rows(row_grp_o, lambda grp: _dot(pt, vtile(grp)))

    buf = win_ref.shape[1]
    n_w = buf // LANE + 1
    w_parts = []
    for t in range(n_w):
        if t < n_w - 1:
            tile = lambda grp, t=t: win_ref[0, t * LANE:(t + 1) * LANE,
                                            grp * HEAD_DIM:(grp + 1) * HEAD_DIM].astype(BF16)
        else:
            tile = lambda grp: new_tile(4, grp)
        w_parts.append(own_rows(row_grp, lambda grp: _dot_nt(qb, tile(grp))))
    s_w = jnp.concatenate(w_parts, axis=1)
    wpos = lax.broadcasted_iota(jnp.int32, (N_HEADS, n_w * LANE), 1).astype(F32) + float(past - buf)
    d_w = q_pos - wpos
    slope_ww = jnp.concatenate([slope] * n_w, axis=1)
    p_w = softmax_rows(s_w - slope_ww * d_w, (d_w >= 0.0) & (d_w <= float(WINDOW))).astype(BF16)
    o_w = jnp.zeros((N_HEADS, HEAD_DIM), F32)
    for t in range(n_w):
        pt = p_w[:, t * LANE:(t + 1) * LANE]
        if t < n_w - 1:
            vtile = lambda grp, t=t: win_ref[0, t * LANE:(t + 1) * LANE,
                                             gd + grp * HEAD_DIM: gd + (grp + 1) * HEAD_DIM].astype(BF16)
        else:
            vtile = lambda grp: new_tile(5, grp)
        o_w = o_w + own_rows(row_grp_o, lambda grp: _dot(pt, vtile(grp)))

    gt = g_ref[0]
    o_ref[0] = gt[:, 0:1] * o_c + gt[:, 1:2] * o_s + gt[:, 2:3] * o_w


def _nsa_sample(q, gates, kc, vct, cache_pages, page_table, win, kv_new, cov, expand, slope16):
    n_seq, n_pages = page_table.shape
    gd = N_KV * HEAD_DIM
    buf = win.shape[1]
    kern = functools.partial(_nsa_sample_kernel, n_pages=n_pages)

    def page_spec(p):
        return pl.BlockSpec((1, PAGE, 2 * gd), lambda b, tbl, p=p: (tbl[b, p], 0, 1))

    def const_spec(shape):
        nd = len(shape)
        return pl.BlockSpec(shape, lambda b, tbl: (0,) * nd)

    grid_spec = pltpu.PrefetchScalarGridSpec(
        num_scalar_prefetch=1,
        grid=(n_seq,),
        in_specs=[page_spec(p) for p in range(n_pages)] + [
            pl.BlockSpec((1, N_HEADS, HEAD_DIM), lambda b, tbl: (b, 0, 0)),
            pl.BlockSpec((1, N_HEADS, 3), lambda b, tbl: (b, 0, 0)),
            pl.BlockSpec((1, N_KV, LANE, LANE), lambda b, tbl: (b, 0, 0, 0)),
            pl.BlockSpec((1, N_KV, VROWS, LANE), lambda b, tbl: (b, 0, 0, 0)),
            pl.BlockSpec((1, buf, 2 * gd), lambda b, tbl: (b, 0, 0)),
            pl.BlockSpec((1, 1, N_KV_PROJ * gd), lambda b, tbl: (b, 0, 0)),
            const_spec(cov.shape), const_spec(expand.shape), const_spec(slope16.shape)],
        out_specs=pl.BlockSpec((1, N_HEADS, HEAD_DIM), lambda b, tbl: (b, 0, 0)),
        scratch_shapes=[pltpu.VMEM((N_HEADS, (n_pages + 1) * LANE), F32)])
    return pl.pallas_call(
        kern,
        grid_spec=grid_spec,
        out_shape=jax.ShapeDtypeStruct((n_seq, N_HEADS, HEAD_DIM), F32),
        compiler_params=_params(1),
    )(page_table, *([cache_pages] * n_pages), q, gates, kc, vct, win, kv_new, cov, expand, slope16)


def _oproj_prompt_kernel(ot_ref, x_ref, gt_ref, wo_ref, o_ref):
    o = ot_ref[0].T.astype(BF16)
    o_ref[...] = x_ref[...] + gt_ref[0] * _dot(o, wo_ref[...])


def _oproj_sample_kernel(a_ref, x_ref, gt_ref, wo_ref, o_ref):
    o_ref[...] = x_ref[...] + gt_ref[0] * _dot(a_ref[...].astype(BF16), wo_ref[...])


def _out_proj(tok, attn, x, gate, wo):
    tm, nt = tok.tiles(512)
    if tok.per_row:
        kern, a_spec = _oproj_sample_kernel, tok.x_spec(tm)
    else:
        per_seq = tok.seq_len // tm
        kern = _oproj_prompt_kernel
        a_spec = pl.BlockSpec((1, D_MODEL, tm), lambda i: (i // per_seq, 0, i % per_seq))
    return pl.pallas_call(
        kern, grid=(nt,),
        in_specs=[a_spec, tok.x_spec(tm), tok.mod_spec(tm), _full_spec((D_MODEL, D_MODEL))],
        out_specs=tok.x_spec(tm),
        out_shape=jax.ShapeDtypeStruct((tok.n_tok, D_MODEL), F32),
        compiler_params=_params(1))(attn, x, tok.mod_arr(gate), wo)


def _final_kernel(x_ref, sh_ref, sc_ref, g_ref, o_ref):
    o_ref[...] = _ada_norm(x_ref[...], g_ref[...], sh_ref[0], sc_ref[0])


def _final(tok, x, shift, scale, g):
    tm, nt = tok.tiles(512)
    return pl.pallas_call(
        _final_kernel, grid=(nt,),
        in_specs=[tok.x_spec(tm), tok.mod_spec(tm), tok.mod_spec(tm), _full_spec((1, D_MODEL))],
        out_specs=tok.x_spec(tm),
        out_shape=jax.ShapeDtypeStruct((tok.n_tok, D_MODEL), F32),
        compiler_params=_params(1))(x, tok.mod_arr(shift), tok.mod_arr(scale), g.reshape(1, D_MODEL))


def _alibi_slopes():
    h = jnp.arange(1, N_HEADS + 1, dtype=F32)
    return jnp.exp2(-8.0 * h / N_HEADS)


def _cover(n_cmp, n_sel):
    c_start = jnp.arange(n_cmp)[:, None] * CMP_STRIDE
    s_start = jnp.arange(n_sel)[None, :] * SEL_BLOCK
    return ((c_start < s_start + SEL_BLOCK) & (c_start + CMP_LEN > s_start)).astype(BF16)


def _trunk(tok, x, mods, kv_mod, f_mod, wts, ctx):
    depth = wts['norm_g'].shape[0]
    n_a = depth // 2
    v_rows = []
    kv = None
    attn_ctx = None
    for l in range(depth):
        m = mods[l]
        if l == n_a:
            kv_out = _kv_proj(tok, x, kv_mod[:, 0], kv_mod[:, 1], wts['kv_norm_g'], wts['kv_w'], wts['kv_wvt'],
                              wts['alibi_k'])
            kv, attn_ctx = ctx['prepare'](kv_out)
        x = _ffn(tok, x, m[:, 0], m[:, 1], m[:, 2], wts['norm_g'][l, 0],
                 wts['ffn_w_gate'][l, 0], wts['ffn_w_up'][l, 0], wts['ffn_w_down'][l, 0])
        if l < n_a:
            x, v = _gmlp(tok, x, m[:, 3], m[:, 4], m[:, 5], wts['norm_g'][l, 1], wts['gmlp_w_uv'][l],
                         wts['gmlp_ln_g'][l], wts['gmlp_ln_b'][l], wts['gmlp_w_sp'][l], wts['gmlp_b_sp'][l],
                         wts['gmlp_w_out'][l])
            v_rows.append(v)
        else:
            j = l - n_a
            q, gates = _qg_proj(tok, x, m[:, 3], m[:, 4], wts['norm_g'][l, 1], wts['nsa_wq'][j],
                                wts['nsa_wg_t'][j], wts['nsa_wg'][j], wts['alibi_q'])
            attn = ctx['attend'](q, gates, attn_ctx)
            x = _out_proj(tok, attn, x, m[:, 5], wts['nsa_w_o'][j])
        x = _ffn(tok, x, m[:, 6], m[:, 7], m[:, 8], wts['norm_g'][l, 2],
                 wts['ffn_w_gate'][l, 1], wts['ffn_w_up'][l, 1], wts['ffn_w_down'][l, 1])
    y = _final(tok, x, f_mod[:, 0], f_mod[:, 1], wts['final_g'])
    return y, kv, v_rows


def kernel(x_prompt, x_sample, cache_kv, state_win_kv, page_table, c_prompt, c_sample, ada_w, ada_b, norm_g, ffn_w_gate, ffn_w_up, ffn_w_down, gmlp_w_uv, gmlp_ln_g, gmlp_ln_b, gmlp_w_sp, gmlp_b_sp, gmlp_w_out, nsa_w_qg, nsa_w_o, kv_norm_g, kv_ada_w, kv_ada_b, kv_w, cmp_w1, cmp_w2, cmp_pe, final_g, final_ada_w, final_ada_b):
    n_p, seq, _ = x_prompt.shape
    n_s, dec_seq, _ = x_sample.shape
    assert dec_seq == 1 and seq % (PAGES_PER_SEG * PAGE) == 0
    depth = ada_w.shape[0]
    n_b = nsa_w_qg.shape[0]
    gd = N_KV * HEAD_DIM
    n_pages = page_table.shape[1]
    assert n_pages == PAGES_PER_SEG
    past = n_pages * PAGE

    nq = N_HEADS * HEAD_DIM
    wg_cols = nsa_w_qg[:, :, nq:]
    wg_pad = jnp.pad(wg_cols.reshape(n_b, D_MODEL, N_KV, HPG * 3), ((0, 0), (0, 0), (0, 0), (0, 16 - HPG * 3)))
    wts = dict(
        norm_g=norm_g, kv_norm_g=kv_norm_g, final_g=final_g,
        ffn_w_gate=ffn_w_gate.astype(BF16), ffn_w_up=ffn_w_up.astype(BF16), ffn_w_down=ffn_w_down.astype(BF16),
        gmlp_w_uv=gmlp_w_uv.astype(BF16), gmlp_ln_g=gmlp_ln_g, gmlp_ln_b=gmlp_ln_b,
        gmlp_w_sp=gmlp_w_sp, gmlp_b_sp=gmlp_b_sp, gmlp_w_out=gmlp_w_out.astype(BF16),
        nsa_wq=nsa_w_qg[:, :, :nq].astype(BF16),
        nsa_wg_t=jnp.swapaxes(wg_pad.reshape(n_b, D_MODEL, N_KV * 16), 1, 2).astype(BF16),
        nsa_wg=jnp.pad(wg_cols, ((0, 0), (0, 0), (0, LANE - N_HEADS * 3))).astype(BF16),
        nsa_w_o=nsa_w_o.astype(BF16),
        kv_w=kv_w.astype(BF16),
        kv_wvt=jnp.concatenate([kv_w[:, 3 * gd:4 * gd], kv_w[:, 5 * gd:6 * gd]], axis=1).T.astype(BF16),
    )
    w1 = cmp_w1.astype(BF16)
    w2 = cmp_w2.astype(BF16)
    w2t = jnp.swapaxes(cmp_w2, 1, 2).astype(BF16)
    pe = jnp.broadcast_to(cmp_pe.astype(BF16).reshape(2, 1, CMP_LEN * HEAD_DIM), (2, 16, CMP_LEN * HEAD_DIM))
    slopes = _alibi_slopes()
    sl2 = slopes * LOG2E
    off_f = jnp.arange(LANE, dtype=F32)

    def split3(x):
        a = x.astype(BF16)
        b = (x - a.astype(F32)).astype(BF16)
        c = (x - a.astype(F32) - b.astype(F32)).astype(BF16)
        return [a, b, c]

    q_cols = split3(-sl2[:, None] * off_f[None, :]) + \
        [jnp.broadcast_to(c[:, None], (N_HEADS, LANE)) for c in split3(sl2)] + \
        [jnp.broadcast_to(c[:, None], (N_HEADS, LANE)) for c in split3(sl2 * CMP_STRIDE)]
    wts['alibi_q'] = jnp.pad(jnp.stack(q_cols, axis=-1), ((0, 0), (0, 0), (0, HEAD_DIM - 9)))
    ones, zeros, offs = jnp.ones((LANE,), BF16), jnp.zeros((LANE,), BF16), off_f.astype(BF16)
    wts['alibi_k'] = jnp.pad(jnp.stack([ones] * 3 + [offs] * 3 + [zeros] * 3, axis=-1), ((0, 0), (0, HEAD_DIM - 9)))
    alibi_kc = jnp.pad(jnp.stack([ones] * 3 + [zeros] * 3 + [offs] * 3, axis=-1), ((0, 0), (0, HEAD_DIM - 9)))

    n_c = n_p + n_s
    c_all = jnp.pad(jnp.concatenate([c_prompt, c_sample], axis=0), ((0, (-n_c) % 8), (0, 0)))
    mod_all = _mod_linear(c_all, ada_w, ada_b)
    kv_mod_all = _mod_linear(c_all, kv_ada_w[None], kv_ada_b[None])[0]
    f_mod_all = _mod_linear(c_all, final_ada_w[None], final_ada_b[None])[0]

    def rows(a, lo, hi, k):
        return a[..., lo:hi, :].reshape(a.shape[:-2] + (hi - lo, k, D_MODEL))

    tok_p = _Tok(n_p, seq, per_row=False)
    n_seg = seq // (PAGES_PER_SEG * PAGE)
    n_qt = seq // LANE
    ncp = n_seg * PAGES_PER_SEG * 8
    n_sel_p = seq // SEL_BLOCK
    assert n_sel_p <= LANE
    cov_t = jnp.pad(_cover(ncp, n_sel_p).T, ((0, LANE - n_sel_p), (0, 0)))
    sl_rows = jnp.repeat(sl2.reshape(N_KV, 1, HPG), LANE, axis=2)
    onehot = (jnp.arange(LANE)[None, None, :] ==
              (2 * jnp.arange(n_qt)[:, None, None] + jnp.arange(LANE)[None, :, None] // SEL_BLOCK)).astype(BF16)

    def prepare_p(kv_out):
        kv, k_nat, v_t = kv_out
        pages_per_seq = seq // PAGE
        base = jnp.arange(n_p * n_seg, dtype=jnp.int32)[:, None] * PAGES_PER_SEG
        table = jnp.minimum(base + jnp.arange(PAGES_PER_SEG + 1, dtype=jnp.int32)[None, :],
                            n_p * pages_per_seq - 1)
        kc, vct = _compress(kv.reshape(n_p * pages_per_seq, PAGE, N_KV_PROJ * gd), table, w1, w2, w2t, pe,
                            alibi_kc, n_p, n_seg)
        return kv, (kc, vct, k_nat, v_t)

    def attend_p(q, gates, c):
        kc, vct, k_nat, v_t = c
        return _nsa_prompt(q, gates, kc, vct, k_nat, v_t, cov_t, onehot, sl_rows, n_p, seq)

    y_p, kv_p, _ = _trunk(tok_p, x_prompt.reshape(n_p * seq, D_MODEL),
                          rows(mod_all, 0, n_p, 9), rows(kv_mod_all, 0, n_p, 2), rows(f_mod_all, 0, n_p, 2),
                          wts, dict(prepare=prepare_p, attend=attend_p))
    kv_p = kv_p.reshape(n_p, seq, N_KV_PROJ, N_KV, HEAD_DIM)
    kv_prompt = kv_p[:, :, :4]
    win_prompt = kv_p[:, seq - min(WINDOW, seq):, 4:6]

    tok_s = _Tok(n_s, 1, per_row=True)
    cache_pages = cache_kv.reshape(cache_kv.shape[0], PAGE, 4 * gd)
    buf = state_win_kv.shape[1]
    win_flat = state_win_kv.reshape(n_s, buf, 2 * gd)
    n_sel_s = (past + 1 + SEL_BLOCK - 1) // SEL_BLOCK
    cov_s = jnp.pad(_cover(LANE, n_sel_s).T, ((0, LANE - n_sel_s), (0, 0)))
    n_keys = (n_pages + 1) * LANE
    expand_s = (jnp.arange(LANE)[:, None] == (jnp.arange(n_keys)[None, :] // SEL_BLOCK)).astype(BF16)
    slope16 = jnp.broadcast_to(slopes[:, None], (N_HEADS, LANE))

    def prepare_s(kv):
        table = jnp.concatenate([page_table, page_table[:, -1:]], axis=1)
        kc, vct = _compress(cache_pages, table, w1, w2, w2t, pe, alibi_kc, n_s, 1)
        return kv, (kc, vct, kv)

    def attend_s(q, gates, c):
        kc, vct, kv = c
        o = _nsa_sample(q.reshape(n_s, N_HEADS, HEAD_DIM), gates[:, :N_HEADS * 3].reshape(n_s, N_HEADS, 3),
                        kc, vct, cache_pages, page_table, win_flat, kv.reshape(n_s, 1, N_KV_PROJ * gd),
                        cov_s, expand_s, slope16)
        return o.reshape(n_s, N_HEADS * HEAD_DIM)

    y_s, kv_s, v_rows = _trunk(tok_s, x_sample.reshape(n_s, D_MODEL),
                               rows(mod_all, n_p, n_c, 9), rows(kv_mod_all, n_p, n_c, 2),
                               rows(f_mod_all, n_p, n_c, 2), wts, dict(prepare=prepare_s, attend=attend_s))
    kv_s = kv_s.reshape(n_s, 1, N_KV_PROJ, N_KV, HEAD_DIM)
    kv_sample = kv_s[:, :, :4]
    win_sample = jnp.concatenate([state_win_kv, kv_s[:, :, 4:6]], axis=1)[:, 1:]
    gmlp_v_sample = jnp.stack([v.reshape(n_s, 1, D_V) for v in v_rows])

    return (y_p.reshape(n_p, seq, D_MODEL), y_s.reshape(n_s, 1, D_MODEL), kv_prompt, kv_sample,
            win_prompt, win_sample, gmlp_v_sample)
```

```python
import functools

import jax
import jax.numpy as jnp
from jax import lax
from jax.experimental import pallas as pl
from jax.experimental.pallas import tpu as pltpu

F32 = jnp.float32
BF16 = jnp.bfloat16

D_MODEL = 1024
D_FF = 2816
D_V = 3072
CHUNK = 128
N_GROUPS_A = 8
CG = D_V // N_GROUPS_A
N_HEADS = 16
HEAD_DIM = 64
N_KV = 4
HPG = N_HEADS // N_KV
CMP_LEN = 32
CMP_STRIDE = 16
CMP_HID = 256
SEL_BLOCK = 64
N_SEL = 16
WINDOW = 512
PAGE = 128
N_KV_PROJ = 6
EPS = 1e-6
NEG = -1e30
M_FLOOR = -1e29
PICKED = -3e38
FORCE_BONUS = 1e4
LOG2E = 1.4426950408889634
MASK_OFF = 1e30

LANE = 128
VMEM_LIMIT = 56 * 1024 * 1024
FF_CHUNK = 256
UV_CHUNK = 512
VROWS = HEAD_DIM + 16
N_NEAR = 8
SAMPLE_SEQS = 4
PAGES_PER_SEG = 16
XROWS = PAGES_PER_SEG * 8 + 8


def _dot(a, b):
    return jnp.dot(a, b, preferred_element_type=F32)


def _dot_nt(a, b):
    return lax.dot_general(a, b, (((1,), (1,)), ((), ())), preferred_element_type=F32)


def _split_bf16(x):
    hi = x.astype(BF16)
    lo = (x - hi.astype(F32)).astype(BF16)
    return hi, lo


def _params(n_grid):
    return pltpu.CompilerParams(dimension_semantics=("arbitrary",) * n_grid,
                                vmem_limit_bytes=VMEM_LIMIT)


def _full_spec(shape):
    nd = len(shape)
    return pl.BlockSpec(shape, lambda *_: (0,) * nd, pipeline_mode=pl.Buffered(1))


def _ada_norm(x, g, shift, scale):
    ms = jnp.mean(x * x, axis=-1, keepdims=True)
    h = x * lax.rsqrt(ms + EPS) * g
    return h * (1.0 + scale) + shift


def _mod_kernel(c_ref, w_ref, b_ref, o_ref):
    c = c_ref[...]
    a = c * jax.nn.sigmoid(c)
    w = w_ref[0]
    a_hi, a_lo = _split_bf16(a)
    w_hi, w_lo = _split_bf16(w)
    o_ref[0] = _dot(a_hi, w_hi) + _dot(a_hi, w_lo) + _dot(a_lo, w_hi) + b_ref[0]


def _mod_linear(c, w, b):
    n_l, d, n = w.shape
    m = c.shape[0]
    tn = 1024
    return pl.pallas_call(
        _mod_kernel,
        grid=(n_l, n // tn),
        in_specs=[pl.BlockSpec((m, d), lambda l, j: (0, 0)),
                  pl.BlockSpec((1, d, tn), lambda l, j: (l, 0, j)),
                  pl.BlockSpec((1, 1, tn), lambda l, j: (l, 0, j))],
        out_specs=pl.BlockSpec((1, m, tn), lambda l, j: (l, 0, j)),
        out_shape=jax.ShapeDtypeStruct((n_l, m, n), F32),
        compiler_params=_params(2),
    )(c, w, b.reshape(n_l, 1, n))


class _Tok:
    def __init__(self, n_seq, seq_len, per_row):
        self.n_seq, self.seq_len = n_seq, seq_len
        self.n_tok = n_seq * seq_len
        self.per_row = per_row

    def tiles(self, tm):
        if self.per_row:
            return self.n_tok, 1
        assert self.seq_len % tm == 0
        return tm, self.n_tok // tm

    def x_spec(self, tm, width=D_MODEL):
        return pl.BlockSpec((tm, width), lambda i: (i, 0))

    def mod_spec(self, tm):
        if self.per_row:
            return pl.BlockSpec((1, tm, D_MODEL), lambda i: (0, 0, 0))
        per_seq = self.seq_len // tm
        return pl.BlockSpec((1, 1, D_MODEL), lambda i: (i // per_seq, 0, 0))

    def mod_arr(self, m):
        if self.per_row:
            return m.reshape(1, self.n_tok, D_MODEL)
        return m.reshape(self.n_seq, 1, D_MODEL)


def _ffn_kernel(x_ref, sh_ref, sc_ref, gt_ref, g_ref, wg_ref, wu_ref, wd_ref, o_ref, a_scr):
    x = x_ref[...]
    hb = _ada_norm(x, g_ref[...], sh_ref[0], sc_ref[0]).astype(BF16)
    for c in range(D_FF // FF_CHUNK):
        cs = slice(c * FF_CHUNK, (c + 1) * FF_CHUNK)
        gate = _dot(hb, wg_ref[:, cs])
        up = _dot(hb, wu_ref[:, cs])
        a_scr[:, cs] = (gate * jax.nn.sigmoid(gate) * up).astype(BF16)
    y = _dot(a_scr[...], wd_ref[...])
    o_ref[...] = x + (0.5 * gt_ref[0]) * y


def _ffn(tok, x, shift, scale, gate, g, wg, wu, wd):
    tm, nt = tok.tiles(512)
    return pl.pallas_call(
        _ffn_kernel,
        grid=(nt,),
        in_specs=[tok.x_spec(tm), tok.mod_spec(tm), tok.mod_spec(tm), tok.mod_spec(tm),
                  _full_spec((1, D_MODEL)), _full_spec((D_MODEL, D_FF)),
                  _full_spec((D_MODEL, D_FF)), _full_spec((D_FF, D_MODEL))],
        out_specs=tok.x_spec(tm),
        out_shape=jax.ShapeDtypeStruct((tok.n_tok, D_MODEL), F32),
        scratch_shapes=[pltpu.VMEM((tm, D_FF), BF16)],
        compiler_params=_params(1),
    )(x, tok.mod_arr(shift), tok.mod_arr(scale), tok.mod_arr(gate), g.reshape(1, D_MODEL), wg, wu, wd)


def _gmlp_uv(x_ref, sh_ref, sc_ref, g_ref, wuv_ref, lng_ref, lnb_ref, u_scr, v_scr):
    hb = _ada_norm(x_ref[...], g_ref[...], sh_ref[0], sc_ref[0]).astype(BF16)
    for c in range(D_V // UV_CHUNK):
        cs = slice(c * UV_CHUNK, (c + 1) * UV_CHUNK)
        cv = slice(D_V + c * UV_CHUNK, D_V + (c + 1) * UV_CHUNK)
        u_scr[:, cs] = jax.nn.gelu(_dot(hb, wuv_ref[:, cs]))
        v_scr[:, cs] = jax.nn.gelu(_dot(hb, wuv_ref[:, cv]))
    v = v_scr[...]
    mu = jnp.mean(v, axis=-1, keepdims=True)
    vc = v - mu
    var = jnp.mean(vc * vc, axis=-1, keepdims=True)
    return vc * lax.rsqrt(var + EPS) * lng_ref[...] + lnb_ref[...]


def _gmlp_prompt_kernel(x_ref, sh_ref, sc_ref, gt_ref, g_ref, wuv_ref, lng_ref, lnb_ref,
                        wsp_ref, bsp_ref, wout_ref, o_ref, u_scr, v_scr, a_scr):
    v_scr[...] = _gmlp_uv(x_ref, sh_ref, sc_ref, g_ref, wuv_ref, lng_ref, lnb_ref, u_scr, v_scr)
    row = lax.broadcasted_iota(jnp.int32, (CHUNK, CHUNK), 0)
    col = lax.broadcasted_iota(jnp.int32, (CHUNK, CHUNK), 1)
    tm = x_ref.shape[0]
    for grp in range(N_GROUPS_A):
        w = jnp.where(row >= col, wsp_ref[grp], 0.0).astype(BF16)
        cs = slice(grp * CG, (grp + 1) * CG)
        for n in range(tm // CHUNK):
            rs = slice(n * CHUNK, (n + 1) * CHUNK)
            s = _dot(w, v_scr[rs, cs].astype(BF16)) + bsp_ref[:, cs]
            a_scr[rs, cs] = (u_scr[rs, cs] * s).astype(BF16)
    y = _dot(a_scr[...], wout_ref[...])
    o_ref[...] = x_ref[...] + gt_ref[0] * y


def _gmlp_sample_kernel(x_ref, sh_ref, sc_ref, gt_ref, g_ref, wuv_ref, lng_ref, lnb_ref,
                        wrow_ref, brow_ref, wout_ref, o_ref, vout_ref, u_scr, v_scr):
    vn = _gmlp_uv(x_ref, sh_ref, sc_ref, g_ref, wuv_ref, lng_ref, lnb_ref, u_scr, v_scr)
    vout_ref[...] = vn
    s = vn * wrow_ref[...] + brow_ref[...]
    y = _dot((u_scr[...] * s).astype(BF16), wout_ref[...])
    o_ref[...] = x_ref[...] + gt_ref[0] * y


def _gmlp(tok, x, shift, scale, gate, g, wuv, ln_g, ln_b, w_sp, b_sp, wout):
    tm, nt = tok.tiles(512)
    common = [tok.x_spec(tm), tok.mod_spec(tm), tok.mod_spec(tm), tok.mod_spec(tm),
              _full_spec((1, D_MODEL)), _full_spec((D_MODEL, 2 * D_V)),
              _full_spec((1, D_V)), _full_spec((1, D_V))]
    args = [x, tok.mod_arr(shift), tok.mod_arr(scale), tok.mod_arr(gate), g.reshape(1, D_MODEL), wuv,
            ln_g.reshape(1, D_V), ln_b.reshape(1, D_V)]
    x_shape = jax.ShapeDtypeStruct((tok.n_tok, D_MODEL), F32)
    if tok.per_row:
        wrow = jnp.repeat(w_sp[:, 0, 0], CG).reshape(1, D_V)
        brow = jnp.repeat(b_sp[:, 0], CG).reshape(1, D_V)
        return pl.pallas_call(
            _gmlp_sample_kernel,
            grid=(nt,),
            in_specs=common + [_full_spec((1, D_V)), _full_spec((1, D_V)), _full_spec((D_V, D_MODEL))],
            out_specs=[tok.x_spec(tm), tok.x_spec(tm, D_V)],
            out_shape=[x_shape, jax.ShapeDtypeStruct((tok.n_tok, D_V), F32)],
            scratch_shapes=[pltpu.VMEM((tm, D_V), F32), pltpu.VMEM((tm, D_V), F32)],
            compiler_params=_params(1),
        )(*args, wrow, brow, wout)
    bias = jnp.repeat(b_sp.T, CG, axis=1)
    out = pl.pallas_call(
        _gmlp_prompt_kernel,
        grid=(nt,),
        in_specs=common + [_full_spec((N_GROUPS_A, CHUNK, CHUNK)), _full_spec((CHUNK, D_V)),
                           _full_spec((D_V, D_MODEL))],
        out_specs=tok.x_spec(tm),
        out_shape=x_shape,
        scratch_shapes=[pltpu.VMEM((tm, D_V), F32), pltpu.VMEM((tm, D_V), F32),
                        pltpu.VMEM((tm, D_V), BF16)],
        compiler_params=_params(1),
    )(*args, w_sp, bias, wout)
    return out, None


def _kv_prompt_kernel(x_ref, sh_ref, sc_ref, g_ref, w_ref, wvt_ref, ak_ref, kv_ref, k_ref, vt_ref):
    hb = _ada_norm(x_ref[...], g_ref[...], sh_ref[0], sc_ref[0]).astype(BF16)
    kv = _dot(hb, w_ref[...])
    kv_ref[...] = kv
    vt = _dot_nt(wvt_ref[...], hb).astype(BF16)
    tm = x_ref.shape[0]
    gd = N_KV * HEAD_DIM
    ones_rows = jnp.where(lax.broadcasted_iota(jnp.int32, (VROWS - HEAD_DIM, LANE), 0) == 0, 1.0, 0.0).astype(BF16)
    for c in range(tm // LANE):
        rs = slice(c * LANE, (c + 1) * LANE)
        for hd in range(2 * N_KV):
            vt_ref[0, c, hd * VROWS:hd * VROWS + HEAD_DIM, :] = vt[hd * HEAD_DIM:(hd + 1) * HEAD_DIM, rs]
            vt_ref[0, c, hd * VROWS + HEAD_DIM:(hd + 1) * VROWS, :] = ones_rows
        for j, slot in enumerate((2, 4)):
            for grp in range(N_KV):
                col = slot * gd + grp * HEAD_DIM
                k_ref[0, j * N_KV + grp, c, :, 0:HEAD_DIM] = kv[rs, col:col + HEAD_DIM].astype(BF16)
                k_ref[0, j * N_KV + grp, c, :, HEAD_DIM:LANE] = ak_ref[...]


def _kv_sample_kernel(x_ref, sh_ref, sc_ref, g_ref, w_ref, kv_ref):
    hb = _ada_norm(x_ref[...], g_ref[...], sh_ref[0], sc_ref[0]).astype(BF16)
    kv_ref[...] = _dot(hb, w_ref[...])


def _kv_proj(tok, x, shift, scale, g, w, wvt, alibi_k):
    tm, nt = tok.tiles(512)
    n_kv = N_KV_PROJ * N_KV * HEAD_DIM
    in_specs = [tok.x_spec(tm), tok.mod_spec(tm), tok.mod_spec(tm), _full_spec((1, D_MODEL)),
                _full_spec((D_MODEL, n_kv))]
    args = [x, tok.mod_arr(shift), tok.mod_arr(scale), g.reshape(1, D_MODEL), w]
    kv_shape = jax.ShapeDtypeStruct((tok.n_tok, n_kv), F32)
    if tok.per_row:
        return pl.pallas_call(
            _kv_sample_kernel, grid=(nt,), in_specs=in_specs, out_specs=tok.x_spec(tm, n_kv),
            out_shape=kv_shape, compiler_params=_params(1))(*args)
    per_seq = tok.seq_len // tm
    n_qt = tok.seq_len // LANE
    sub = tm // LANE
    return pl.pallas_call(
        _kv_prompt_kernel,
        grid=(nt,),
        in_specs=in_specs + [_full_spec((2 * N_KV * HEAD_DIM, D_MODEL)), _full_spec((LANE, HEAD_DIM))],
        out_specs=[tok.x_spec(tm, n_kv),
                   pl.BlockSpec((1, 2 * N_KV, sub, LANE, LANE),
                                lambda i: (i // per_seq, 0, i % per_seq, 0, 0)),
                   pl.BlockSpec((1, sub, 2 * N_KV * VROWS, LANE),
                                lambda i: (i // per_seq, i % per_seq, 0, 0))],
        out_shape=[kv_shape,
                   jax.ShapeDtypeStruct((tok.n_seq, 2 * N_KV, n_qt, LANE, LANE), BF16),
                   jax.ShapeDtypeStruct((tok.n_seq, n_qt, 2 * N_KV * VROWS, LANE), BF16)],
        compiler_params=_params(1),
    )(*args, wvt, alibi_k)


def _compress_kernel(tbl_ref, *refs):
    pages = refs[:PAGES_PER_SEG + 1]
    perm_ref, w1_ref, w2_ref, w2t_ref, pe_ref, akc_ref, kc_ref, vct_ref, x_scr, hb_scr = refs[PAGES_PER_SEG + 1:]
    del tbl_ref
    perm = perm_ref[...]
    n_heads = 2 * N_KV
    half = CMP_STRIDE * HEAD_DIM
    for p in range(PAGES_PER_SEG + 1):
        xp = _dot(perm, pages[p][0].astype(BF16))
        for hd in range(n_heads):
            cs = slice(hd * HEAD_DIM, (hd + 1) * HEAD_DIM)
            x_scr[hd, p * 8:(p + 1) * 8, :] = jnp.concatenate(
                [xp[s * 8:(s + 1) * 8, cs] for s in range(CMP_STRIDE)], axis=1)
    n_blk = PAGES_PER_SEG * 8
    for slot in range(2):
        x = x_scr[slot * N_KV:(slot + 1) * N_KV].reshape(N_KV * XROWS, half).astype(BF16)
        ha = _dot(x, w1_ref[slot, 0:half, :])
        hb_scr[...] = _dot(x, w1_ref[slot, half:2 * half, :])
        pe_hid = _dot(pe_ref[slot], w1_ref[slot])[0:1]
        for grp in range(N_KV):
            hid = ha[grp * XROWS:grp * XROWS + n_blk] + hb_scr[pl.ds(grp * XROWS + 1, n_blk), :] + pe_hid
            act = jax.nn.gelu(hid).astype(BF16)
            if slot == 0:
                kc_ref[0, grp, :, 0:HEAD_DIM] = _dot(act, w2_ref[0]).astype(BF16)
                kc_ref[0, grp, :, HEAD_DIM:LANE] = akc_ref[...]
            else:
                vct_ref[0, grp, 0:HEAD_DIM, :] = _dot_nt(w2t_ref[1], act).astype(BF16)
                vct_ref[0, grp, HEAD_DIM:VROWS, :] = jnp.where(
                    lax.broadcasted_iota(jnp.int32, (VROWS - HEAD_DIM, n_blk), 0) == 0, 1.0, 0.0).astype(BF16)


def _compress(src, table, w1, w2, w2t, pe, alibi_kc, n_seq, n_seg):
    n_blk = PAGES_PER_SEG * 8
    gd2 = 2 * N_KV * HEAD_DIM
    dst = jnp.arange(PAGE)
    src_row = (dst % 8) * CMP_STRIDE + dst // 8
    perm = (jnp.arange(PAGE)[None, :] == src_row[:, None]).astype(BF16)

    def page_spec(p):
        return pl.BlockSpec((1, PAGE, gd2), lambda i, tbl, p=p: (tbl[i, p], 0, 0))

    grid_spec = pltpu.PrefetchScalarGridSpec(
        num_scalar_prefetch=1,
        grid=(n_seq * n_seg,),
        in_specs=[page_spec(p) for p in range(PAGES_PER_SEG + 1)] + [
            pl.BlockSpec((PAGE, PAGE), lambda i, tbl: (0, 0)),
            pl.BlockSpec(w1.shape, lambda i, tbl: (0, 0, 0)),
            pl.BlockSpec(w2.shape, lambda i, tbl: (0, 0, 0)),
            pl.BlockSpec(w2t.shape, lambda i, tbl: (0, 0, 0)),
            pl.BlockSpec(pe.shape, lambda i, tbl: (0, 0, 0)),
            pl.BlockSpec((LANE, HEAD_DIM), lambda i, tbl: (0, 0))],
        out_specs=[pl.BlockSpec((1, N_KV, n_blk, LANE), lambda i, tbl: (i // n_seg, 0, i % n_seg, 0)),
                   pl.BlockSpec((1, N_KV, VROWS, n_blk), lambda i, tbl: (i // n_seg, 0, 0, i % n_seg))],
        scratch_shapes=[pltpu.VMEM((2 * N_KV, XROWS, CMP_STRIDE * HEAD_DIM), F32),
                        pltpu.VMEM((N_KV * XROWS, CMP_HID), F32)])
    return pl.pallas_call(
        _compress_kernel,
        grid_spec=grid_spec,
        out_shape=[jax.ShapeDtypeStruct((n_seq, N_KV, n_seg * n_blk, LANE), BF16),
                   jax.ShapeDtypeStruct((n_seq, N_KV, VROWS, n_seg * n_blk), BF16)],
        compiler_params=_params(1),
    )(table, *([src] * (PAGES_PER_SEG + 1)), perm, w1, w2, w2t, pe, alibi_kc)


def _qg_prompt_kernel(x_ref, sh_ref, sc_ref, g_ref, wq_ref, wgt_ref, aq_ref, q_ref, gt_ref):
    hb = _ada_norm(x_ref[...], g_ref[...], sh_ref[0], sc_ref[0]).astype(BF16)
    q = _dot(hb, wq_ref[...]) * (HEAD_DIM ** -0.5 * LOG2E)
    tm = x_ref.shape[0]
    for h in range(N_HEADS):
        q_ref[0, h, :, 0:HEAD_DIM] = q[:, h * HEAD_DIM:(h + 1) * HEAD_DIM].astype(BF16)
        for c in range(tm // LANE):
            q_ref[0, h, c * LANE:(c + 1) * LANE, HEAD_DIM:LANE] = aq_ref[h]
    gates = jax.nn.sigmoid(_dot_nt(wgt_ref[...], hb))
    for grp in range(N_KV):
        gt_ref[0, grp] = gates[grp * 16:(grp + 1) * 16, :]


def _qg_sample_kernel(x_ref, sh_ref, sc_ref, g_ref, wq_ref, wg_ref, q_ref, gt_ref):
    hb = _ada_norm(x_ref[...], g_ref[...], sh_ref[0], sc_ref[0]).astype(BF16)
    q_ref[...] = _dot(hb, wq_ref[...]) * (HEAD_DIM ** -0.5)
    gt_ref[...] = jax.nn.sigmoid(_dot(hb, wg_ref[...]))


def _qg_proj(tok, x, shift, scale, g, wq, wg_t, wg_nat, alibi_q):
    tm, nt = tok.tiles(512)
    in_specs = [tok.x_spec(tm), tok.mod_spec(tm), tok.mod_spec(tm), _full_spec((1, D_MODEL)),
                _full_spec((D_MODEL, D_MODEL))]
    args = [x, tok.mod_arr(shift), tok.mod_arr(scale), g.reshape(1, D_MODEL), wq]
    if tok.per_row:
        return pl.pallas_call(
            _qg_sample_kernel, grid=(nt,),
            in_specs=in_specs + [_full_spec((D_MODEL, LANE))],
            out_specs=[tok.x_spec(tm), tok.x_spec(tm, LANE)],
            out_shape=[jax.ShapeDtypeStruct((tok.n_tok, D_MODEL), F32),
                       jax.ShapeDtypeStruct((tok.n_tok, LANE), F32)],
            compiler_params=_params(1))(*args, wg_nat)
    per_seq = tok.seq_len // tm
    return pl.pallas_call(
        _qg_prompt_kernel, grid=(nt,),
        in_specs=in_specs + [_full_spec((N_KV * 16, D_MODEL)), _full_spec((N_HEADS, LANE, HEAD_DIM))],
        out_specs=[pl.BlockSpec((1, N_HEADS, tm, LANE), lambda i: (i // per_seq, 0, i % per_seq, 0)),
                   pl.BlockSpec((1, N_KV, 16, tm), lambda i: (i // per_seq, 0, 0, i % per_seq))],
        out_shape=[jax.ShapeDtypeStruct((tok.n_seq, N_HEADS, tok.seq_len, LANE), BF16),
                   jax.ShapeDtypeStruct((tok.n_seq, N_KV, 16, tok.seq_len), F32)],
        compiler_params=_params(1))(*args, wg_t, alibi_q)


def _select_blocks(work, idx_f, axis, rounds):
    for _ in range(rounds):
        mx = jnp.max(work, axis=axis, keepdims=True)
        first = jnp.min(jnp.where(work == mx, idx_f, 1e9), axis=axis, keepdims=True)
        first = jnp.where(mx > NEG / 2, first, -1.0)
        work = jnp.where(idx_f == first, PICKED, work)
        yield
    return jnp.where(work == PICKED, 1.0, 0.0)


def _softmax_group(scores, offsets, m_old, exp_dtype=F32):
    m_new = m_old
    for s, off in zip(scores, offsets):
        m_new = jnp.maximum(m_new, jnp.max(s, axis=0, keepdims=True) - off)
        yield
    alpha = jnp.exp2(m_old - m_new)
    probs = []
    for s, off in zip(scores, offsets):
        probs.append(jnp.exp2((s - (m_new + off)).astype(exp_dtype)))
        yield
    return m_new, alpha, probs


def _run(gen):
    try:
        while True:
            next(gen)
    except StopIteration as stop:
        return stop.value


def _interleave(*gens):
    live = list(gens)
    while live:
        for g in list(live):
            try:
                next(g)
            except StopIteration:
                live.remove(g)


def _normalise(acc):
    l = acc[HEAD_DIM:HEAD_DIM + 1]
    return acc[0:HEAD_DIM] * jnp.where(l > 0.0, 1.0 / l, 0.0)


def _nsa_prompt_kernel(q_ref, g_ref, kc_ref, vct_ref, ks_ref, vst_ref, kw_ref, vwt_ref, cov_ref, hot_ref, sl_ref,
                       o_ref, qp_scr, oc_scr, m_scr, acc_scr, bits_ref, *, ncp, n_qt):
    j = pl.program_id(2)
    cols = HPG * LANE
    key_io = lax.broadcasted_iota(jnp.int32, (LANE, LANE), 0)
    tok_io = lax.broadcasted_iota(jnp.int32, (LANE, LANE), 1)
    sl = sl_ref[0]
    m_init = jnp.full((1, cols), M_FLOOR, F32)

    def tile4(a):
        return jnp.concatenate([a] * HPG, axis=1)

    @pl.when(j == 0)
    def _():
        qp_scr[1] = jnp.zeros(qp_scr.shape[1:], BF16)
        oc_scr[1] = jnp.zeros(oc_scr.shape[1:], F32)
        for w in range(LANE // 32):
            bits_ref[LANE // 32 + w] = 0

    i = jnp.maximum(j - 1, 0)
    slot = (j + 1) % 2

    def sel_keys(jp):
        return jnp.concatenate([hot_ref[jp], ks_ref[0, 0, jp]], axis=1)

    fi_a = i.astype(F32)
    qp = qp_scr[slot]
    rel = tile4((tok_io - key_io).astype(F32))

    def window_branch():
        qx = qp[:, LANE:2 * LANE]

        n_wt = WINDOW // LANE
        scores, offsets, vts = [], [], []
        tiles = [jnp.maximum(i - n_wt + kt_i, 0) for kt_i in range(n_wt + 1)]
        s_all = _dot_nt(jnp.concatenate([kw_ref[0, 0, t] for t in tiles], axis=0), qx)
        yield
        for kt_i in range(n_wt + 1):
            jt = i - n_wt + kt_i
            jt_c = tiles[kt_i]
            s = s_all[kt_i * LANE:(kt_i + 1) * LANE]
            if kt_i == 0:
                s = jnp.where(rel <= 0.0, s, NEG)
            elif kt_i == n_wt:
                s = jnp.where(rel >= 0.0, s, NEG)
            scores.append(s)
            offsets.append(sl * float(LANE * (n_wt - kt_i)) + jnp.where(jt >= 0, 0.0, MASK_OFF).astype(F32))
            vts.append(vwt_ref[0, jt_c])
        _, _, probs = yield from _softmax_group(scores, offsets, m_init, BF16)
        p = jnp.concatenate(probs, axis=0)
        return _normalise(_dot(jnp.concatenate(vts, axis=1), p))

    def near_tiles():
        fi = fi_a
        tiles = [0, i] + [jnp.maximum(i - r, 0) for r in range(1, N_NEAR + 1)]
        s_all = _dot_nt(jnp.concatenate([sel_keys(t) for t in tiles], axis=0), qp)
        yield
        scores = [s_all[k * LANE:(k + 1) * LANE] for k in range(len(tiles))]
        scores[1] = jnp.where(rel >= 0.0, scores[1], NEG)
        offsets = [sl * (LANE * fi) + jnp.where(i > N_NEAR, 0.0, MASK_OFF).astype(F32), jnp.zeros((1, cols), F32)]
        for r in range(1, N_NEAR + 1):
            offsets.append(sl * float(LANE * r) + jnp.where(i - r >= 0, 0.0, MASK_OFF).astype(F32))
        vts = [vst_ref[0, t] for t in tiles]
        m, _, probs = yield from _softmax_group(scores, offsets, m_init, BF16)
        p = jnp.concatenate(probs, axis=0)
        m_scr[...] = m
        acc_scr[...] = _dot(jnp.concatenate(vts, axis=1), p)

    def selection_half():
        i_s = jnp.minimum(j, n_qt - 1)
        slot_s = j % 2
        fi = i_s.astype(F32)
        qx = q_ref[0].reshape(cols, LANE)

        n_chunks = ncp // LANE
        rel_c = tile4((tok_io - CMP_STRIDE * key_io).astype(F32))
        scores, offsets = [], []
        s_all = _dot_nt(kc_ref[0, 0], qx)
        yield
        for c in range(n_chunks):
            base = float(CMP_STRIDE * LANE * c + CMP_LEN - 1)
            s_c = s_all[c * LANE:(c + 1) * LANE]
            scores.append(jnp.where(rel_c >= base - LANE * fi, s_c, NEG))
            offsets.append(sl * (LANE * fi - base))
        _, _, probs = yield from _softmax_group(scores, offsets, m_init)
        p_hi, p_lo = _split_bf16(jnp.concatenate(probs, axis=0))
        acc_c = _dot(vct_ref[0, 0], p_hi)
        oc_scr[slot_s] = _normalise(acc_c)
        yield
        l_c = acc_c[HEAD_DIM:HEAD_DIM + 1]
        cov = cov_ref[...]
        imp4 = (_dot(cov, p_hi) + _dot(cov, p_lo)) * jnp.where(l_c > 0.0, 1.0 / l_c, 0.0)
        imp = imp4[:, 0:LANE]
        for h in range(1, HPG):
            imp = imp + imp4[:, h * LANE:(h + 1) * LANE]
        yield

        q_blk = 2 * i_s + tok_io // SEL_BLOCK
        forced = (key_io == 0) | (key_io == q_blk) | (key_io == q_blk - 1)
        valid = key_io <= q_blk
        work = jnp.where(valid & jnp.logical_not(forced), imp, NEG)
        sel = yield from _select_blocks(work, key_io.astype(F32), 0, N_SEL - 3)
        sel = jnp.where(valid & forced, 1.0, sel)
        sel_bias_t = jnp.where(sel.T > 0.5, 0.0, NEG).astype(BF16)
        for h in range(HPG):
            qp_scr[slot_s, h * LANE:(h + 1) * LANE, 0:LANE] = sel_bias_t
            qp_scr[slot_s, h * LANE:(h + 1) * LANE, LANE:2 * LANE] = q_ref[0, h]

        cnt = _dot_nt(jnp.ones((8, LANE), BF16), sel.astype(BF16))[0:1]
        lane = lax.broadcasted_iota(jnp.int32, (1, LANE), 1)
        live = (cnt > 0.0) & (lane >= 2) & (lane < 2 * (i_s - N_NEAR))
        for w in range(LANE // 32):
            bits_ref[slot_s * (LANE // 32) + w] = jnp.sum(
                jnp.where(live & (lane // 32 == w), jnp.left_shift(1, lane % 32), 0))

    box = []

    def window_result():
        box.append((yield from window_branch()))

    _interleave(window_result(), near_tiles(), selection_half())
    o_win = box[0]

    def group_body(k, carry):
        word = bits_ref[slot * (LANE // 32) + k // 4]
        used = jnp.right_shift(word, (8 * k) % 32) & 255

        @pl.when(used != 0)
        def _():
            offsets, vts = [], []
            s_all = _dot_nt(jnp.concatenate([sel_keys(4 * k + r) for r in range(4)], axis=0), qp_scr[slot])
            scores = [s_all[r * LANE:(r + 1) * LANE] for r in range(4)]
            for r in range(4):
                jp = 4 * k + r
                dead = jnp.where((jp == 0) | (jp >= i - N_NEAR), MASK_OFF, 0.0).astype(F32)
                offsets.append(sl * (LANE * (i - jp)).astype(F32) + dead)
                vts.append(vst_ref[0, jp])
            m, alpha, probs = _run(_softmax_group(scores, offsets, m_scr[...], BF16))
            p = jnp.concatenate(probs, axis=0)
            m_scr[...] = m
            acc_scr[...] = alpha * acc_scr[...] + _dot(jnp.concatenate(vts, axis=1), p)
        return carry

    lax.fori_loop(0, jnp.maximum(i - N_NEAR + 3, 0) // 4, group_body, 0)
    o_sel = _normalise(acc_scr[...])

    o_cmp = oc_scr[slot]
    for h in range(HPG):
        cs = slice(h * LANE, (h + 1) * LANE)
        g_c = g_ref[0, 0, 3 * h:3 * h + 1, :]
        g_s = g_ref[0, 0, 3 * h + 1:3 * h + 2, :]
        g_w = g_ref[0, 0, 3 * h + 2:3 * h + 3, :]
        o_ref[0, h * HEAD_DIM:(h + 1) * HEAD_DIM, :] = g_c * o_cmp[:, cs] + g_s * o_sel[:, cs] + g_w * o_win[:, cs]


def _nsa_prompt(q, gates_t, kc, vct, k_ext, v_t, cov_t, onehot, sl_rows, n_seq, seq_len):
    n_qt = seq_len // LANE
    ncp = kc.shape[2]
    assert n_qt % 4 == 0
    kern = functools.partial(_nsa_prompt_kernel, ncp=ncp, n_qt=n_qt)
    cols = HPG * LANE
    last = n_qt - 1

    def sel_tile(j):
        return jnp.minimum(j, last)

    def att_tile(j):
        return jnp.maximum(j - 1, 0)

    return pl.pallas_call(
        kern,
        grid=(n_seq, N_KV, n_qt + 1),
        in_specs=[pl.BlockSpec((1, HPG, LANE, LANE), lambda b, g, j: (b, g, sel_tile(j), 0)),
                  pl.BlockSpec((1, 1, 16, LANE), lambda b, g, j: (b, g, 0, att_tile(j))),
                  pl.BlockSpec((1, 1, ncp, LANE), lambda b, g, j: (b, g, 0, 0)),
                  pl.BlockSpec((1, 1, VROWS, ncp), lambda b, g, j: (b, g, 0, 0)),
                  pl.BlockSpec((1, 1, n_qt, LANE, LANE), lambda b, g, j: (b, g, 0, 0, 0)),
                  pl.BlockSpec((1, n_qt, VROWS, LANE), lambda b, g, j: (b, 0, g, 0)),
                  pl.BlockSpec((1, 1, n_qt, LANE, LANE), lambda b, g, j: (b, N_KV + g, 0, 0, 0)),
                  pl.BlockSpec((1, n_qt, VROWS, LANE), lambda b, g, j: (b, 0, N_KV + g, 0)),
                  pl.BlockSpec((LANE, ncp), lambda b, g, j: (0, 0)),
                  pl.BlockSpec((n_qt, LANE, LANE), lambda b, g, j: (0, 0, 0)),
                  pl.BlockSpec((1, 1, cols), lambda b, g, j: (g, 0, 0))],
        out_specs=pl.BlockSpec((1, HPG * HEAD_DIM, LANE), lambda b, g, j: (b, g, att_tile(j))),
        out_shape=jax.ShapeDtypeStruct((n_seq, N_HEADS * HEAD_DIM, seq_len), F32),
        scratch_shapes=[pltpu.VMEM((2, cols, 2 * LANE), BF16),
                        pltpu.VMEM((2, HEAD_DIM, cols), F32),
                        pltpu.VMEM((1, cols), F32),
                        pltpu.VMEM((VROWS, cols), F32),
                        pltpu.SMEM((2 * (LANE // 32),), jnp.int32)],
        compiler_params=_params(3),
    )(q, gates_t, kc, vct, k_ext, v_t, k_ext, v_t, cov_t, onehot, sl_rows)


def _nsa_sample_kernel(tbl_ref, *refs, n_pages, n_e):
    del tbl_ref
    shared = refs[n_e * n_pages:]
    _interleave(*[_sample_sequence(e, refs[e * n_pages:(e + 1) * n_pages], *shared) for e in range(n_e)])


def _sample_sequence(e, pages, q_ref, g_ref, kc_ref, vct_ref, win_ref, new_ref, cov_ref, exp_ref, slope_ref,
                     o_ref, s_scr):
    n_pages = len(pages)
    gd = N_KV * HEAD_DIM
    past = n_pages * PAGE
    q_pos = float(past)
    qb = q_ref[e].astype(BF16)
    slope = slope_ref[...]
    row_grp = lax.broadcasted_iota(jnp.int32, (N_HEADS, LANE), 0) // HPG
    lane_f = lax.broadcasted_iota(jnp.int32, (N_HEADS, LANE), 1).astype(F32)
    new = new_ref[e]

    def own_rows(g_sel, pick):
        out = pick(0)
        for grp in range(1, N_KV):
            out = jnp.where(g_sel == grp, pick(grp), out)
        return out

    q_f = q_ref[e]
    row_grp_o = lax.broadcasted_iota(jnp.int32, (N_HEADS, HEAD_DIM), 0) // HPG
    q_pair = [jnp.concatenate([jnp.where(row_grp_o == 2 * pr, q_f, 0.0), jnp.where(row_grp_o == 2 * pr + 1, q_f, 0.0)],
                              axis=1).astype(BF16) for pr in range(N_KV // 2)]
    first_row2 = lax.broadcasted_iota(jnp.int32, (LANE, 2 * HEAD_DIM), 0) == 0

    def pair_scores(tile):
        return _dot_nt(q_pair[0], tile(0)) + _dot_nt(q_pair[1], tile(1))

    def own_halves(o_pairs):
        out = o_pairs[0][:, 0:HEAD_DIM]
        for grp in range(1, N_KV):
            half = o_pairs[grp // 2][:, (grp % 2) * HEAD_DIM:(grp % 2 + 1) * HEAD_DIM]
            out = jnp.where(row_grp_o == grp, half, out)
        return out

    def new_slab(slot, pr):
        col = slot * gd + pr * 2 * HEAD_DIM
        return jnp.where(first_row2, new[:, col:col + 2 * HEAD_DIM], 0.0).astype(BF16)

    def softmax_rows(s, valid):
        s = jnp.where(valid, s, NEG)
        m = jnp.max(s, axis=1, keepdims=True)
        p = jnp.where(valid, jnp.exp(s - m), 0.0)
        l = jnp.sum(p, axis=1, keepdims=True)
        return p * jnp.where(l > 0.0, 1.0 / l, 0.0)

    d_c = q_pos - (CMP_STRIDE * lane_f + (CMP_LEN - 1))
    s_c = own_rows(row_grp, lambda grp: _dot_nt(qb, kc_ref[e, grp][:, 0:HEAD_DIM])) - slope * d_c
    yield
    p_c = softmax_rows(s_c, d_c >= 0.0)
    p_cb = p_c.astype(BF16)
    o_c = own_rows(row_grp_o, lambda grp: _dot_nt(p_cb, vct_ref[e, grp, 0:HEAD_DIM, :]))
    yield

    p_grp = own_rows(row_grp, lambda grp: jnp.broadcast_to(
        jnp.sum(p_c[grp * HPG:(grp + 1) * HPG], axis=0, keepdims=True), (N_HEADS, LANE)))
    p_hi, p_lo = _split_bf16(jnp.concatenate([p_grp, jnp.zeros((LANE - N_HEADS, LANE), F32)], axis=0))
    cov_t = cov_ref[...]
    imp_t = _dot_nt(cov_t, p_hi) + _dot_nt(cov_t, p_lo)
    blk = lax.broadcasted_iota(jnp.int32, (LANE, LANE), 0)
    q_blk = past // SEL_BLOCK
    forced = (blk == 0) | (blk == q_blk) | (blk == q_blk - 1)
    valid = blk <= q_blk
    work = jnp.where(valid & jnp.logical_not(forced), imp_t, NEG)
    yield
    sel_t = yield from _select_blocks(work, blk.astype(F32), 0, N_SEL - 3)
    sel = jnp.where(valid & forced, 1.0, sel_t).T[0:N_HEADS]

    n_t = n_pages + 1
    for t in range(n_t):
        if t < n_pages:
            tile = lambda pr, t=t: pages[t][0, :, pr * LANE:(pr + 1) * LANE].astype(BF16)
        else:
            tile = lambda pr: new_slab(2, pr)
        s_scr[e, :, t * LANE:(t + 1) * LANE] = pair_scores(tile)
        if t % 4 == 3:
            yield
    width = n_t * LANE
    kpos = lax.broadcasted_iota(jnp.int32, (N_HEADS, width), 1).astype(F32)
    d_s = q_pos - kpos
    chosen = _dot(sel.astype(BF16), exp_ref[...])
    slope_w = jnp.concatenate([slope] * n_t, axis=1)
    p_s = softmax_rows(s_scr[e] - slope_w * d_s, (chosen > 0.5) & (d_s >= 0.0)).astype(BF16)
    yield
    o_sp = [jnp.zeros((N_HEADS, 2 * HEAD_DIM), F32) for _ in range(N_KV // 2)]
    for t in range(n_t):
        pt = p_s[:, t * LANE:(t + 1) * LANE]
        if t < n_pages:
            vtile = lambda pr, t=t: pages[t][0, :, gd + pr * LANE: gd + (pr + 1) * LANE].astype(BF16)
        else:
            vtile = lambda pr: new_slab(3, pr)
        o_sp = [o_sp[pr] + _dot(pt, vtile(pr)) for pr in range(N_KV // 2)]
        if t % 4 == 3:
            yield
    o_s = own_halves(o_sp)

    buf = win_ref.shape[1]
    n_w = buf // LANE + 1
    w_parts = []
    for t in range(n_w):
        if t < n_w - 1:
            tile = lambda pr, t=t: win_ref[e, t * LANE:(t + 1) * LANE, pr * LANE:(pr + 1) * LANE].astype(BF16)
        else:
            tile = lambda pr: new_slab(4, pr)
        w_parts.append(pair_scores(tile))
    s_w = jnp.concatenate(w_parts, axis=1)
    yield
    wpos = lax.broadcasted_iota(jnp.int32, (N_HEADS, n_w * LANE), 1).astype(F32) + float(past - buf)
    d_w = q_pos - wpos
    slope_ww = jnp.concatenate([slope] * n_w, axis=1)
    p_w = softmax_rows(s_w - slope_ww * d_w, (d_w >= 0.0) & (d_w <= float(WINDOW))).astype(BF16)
    yield
    o_wp = [jnp.zeros((N_HEADS, 2 * HEAD_DIM), F32) for _ in range(N_KV // 2)]
    for t in range(n_w):
        pt = p_w[:, t * LANE:(t + 1) * LANE]
        if t < n_w - 1:
            vtile = lambda pr, t=t: win_ref[e, t * LANE:(t + 1) * LANE,
                                            gd + pr * LANE: gd + (pr + 1) * LANE].astype(BF16)
        else:
            vtile = lambda pr: new_slab(5, pr)
        o_wp = [o_wp[pr] + _dot(pt, vtile(pr)) for pr in range(N_KV // 2)]
    o_w = own_halves(o_wp)

    gt = g_ref[e]
    o_ref[e] = gt[:, 0:1] * o_c + gt[:, 1:2] * o_s + gt[:, 2:3] * o_w


def _nsa_sample(q, gates, kc, vct, cache_pages, page_table, win, kv_new, cov, expand, slope16):
    n_seq, n_pages = page_table.shape
    gd = N_KV * HEAD_DIM
    buf = win.shape[1]
    n_e = SAMPLE_SEQS if n_seq % SAMPLE_SEQS == 0 else 1
    kern = functools.partial(_nsa_sample_kernel, n_pages=n_pages, n_e=n_e)

    def page_spec(e, p):
        return pl.BlockSpec((1, PAGE, 2 * gd), lambda b, tbl, e=e, p=p: (tbl[b * n_e + e, p], 0, 1))

    def const_spec(shape):
        nd = len(shape)
        return pl.BlockSpec(shape, lambda b, tbl: (0,) * nd)

    grid_spec = pltpu.PrefetchScalarGridSpec(
        num_scalar_prefetch=1,
        grid=(n_seq // n_e,),
        in_specs=[page_spec(e, p) for e in range(n_e) for p in range(n_pages)] + [
            pl.BlockSpec((n_e, N_HEADS, HEAD_DIM), lambda b, tbl: (b, 0, 0)),
            pl.BlockSpec((n_e, N_HEADS, 3), lambda b, tbl: (b, 0, 0)),
            pl.BlockSpec((n_e, N_KV, LANE, LANE), lambda b, tbl: (b, 0, 0, 0)),
            pl.BlockSpec((n_e, N_KV, VROWS, LANE), lambda b, tbl: (b, 0, 0, 0)),
            pl.BlockSpec((n_e, buf, 2 * gd), lambda b, tbl: (b, 0, 0)),
            pl.BlockSpec((n_e, 1, N_KV_PROJ * gd), lambda b, tbl: (b, 0, 0)),
            const_spec(cov.shape), const_spec(expand.shape), const_spec(slope16.shape)],
        out_specs=pl.BlockSpec((n_e, N_HEADS, HEAD_DIM), lambda b, tbl: (b, 0, 0)),
        scratch_shapes=[pltpu.VMEM((n_e, N_HEADS, (n_pages + 1) * LANE), F32)])
    return pl.pallas_call(
        kern,
        grid_spec=grid_spec,
        out_shape=jax.ShapeDtypeStruct((n_seq, N_HEADS, HEAD_DIM), F32),
        compiler_params=_params(1),
    )(page_table, *([cache_pages] * (n_e * n_pages)), q, gates, kc, vct, win, kv_new, cov, expand, slope16)


def _oproj_prompt_kernel(ot_ref, x_ref, gt_ref, wo_ref, o_ref):
    o = ot_ref[0].T.astype(BF16)
    o_ref[...] = x_ref[...] + gt_ref[0] * _dot(o, wo_ref[...])


def _oproj_sample_kernel(a_ref, x_ref, gt_ref, wo_ref, o_ref):
    o_ref[...] = x_ref[...] + gt_ref[0] * _dot(a_ref[...].astype(BF16), wo_ref[...])


def _out_proj(tok, attn, x, gate, wo):
    tm, nt = tok.tiles(512)
    if tok.per_row:
        kern, a_spec = _oproj_sample_kernel, tok.x_spec(tm)
    else:
        per_seq = tok.seq_len // tm
        kern = _oproj_prompt_kernel
        a_spec = pl.BlockSpec((1, D_MODEL, tm), lambda i: (i // per_seq, 0, i % per_seq))
    return pl.pallas_call(
        kern, grid=(nt,),
        in_specs=[a_spec, tok.x_spec(tm), tok.mod_spec(tm), _full_spec((D_MODEL, D_MODEL))],
        out_specs=tok.x_spec(tm),
        out_shape=jax.ShapeDtypeStruct((tok.n_tok, D_MODEL), F32),
        compiler_params=_params(1))(attn, x, tok.mod_arr(gate), wo)


def _final_kernel(x_ref, sh_ref, sc_ref, g_ref, o_ref):
    o_ref[...] = _ada_norm(x_ref[...], g_ref[...], sh_ref[0], sc_ref[0])


def _final(tok, x, shift, scale, g):
    tm, nt = tok.tiles(512)
    return pl.pallas_call(
        _final_kernel, grid=(nt,),
        in_specs=[tok.x_spec(tm), tok.mod_spec(tm), tok.mod_spec(tm), _full_spec((1, D_MODEL))],
        out_specs=tok.x_spec(tm),
        out_shape=jax.ShapeDtypeStruct((tok.n_tok, D_MODEL), F32),
        compiler_params=_params(1))(x, tok.mod_arr(shift), tok.mod_arr(scale), g.reshape(1, D_MODEL))


def _alibi_slopes():
    h = jnp.arange(1, N_HEADS + 1, dtype=F32)
    return jnp.exp2(-8.0 * h / N_HEADS)


def _cover(n_cmp, n_sel):
    c_start = jnp.arange(n_cmp)[:, None] * CMP_STRIDE
    s_start = jnp.arange(n_sel)[None, :] * SEL_BLOCK
    return ((c_start < s_start + SEL_BLOCK) & (c_start + CMP_LEN > s_start)).astype(BF16)


def _trunk(tok, x, mods, kv_mod, f_mod, wts, ctx):
    depth = wts['norm_g'].shape[0]
    n_a = depth // 2
    v_rows = []
    kv = None
    attn_ctx = None
    for l in range(depth):
        m = mods[l]
        if l == n_a:
            kv_out = _kv_proj(tok, x, kv_mod[:, 0], kv_mod[:, 1], wts['kv_norm_g'], wts['kv_w'], wts['kv_wvt'],
                              wts['alibi_k'])
            kv, attn_ctx = ctx['prepare'](kv_out)
        x = _ffn(tok, x, m[:, 0], m[:, 1], m[:, 2], wts['norm_g'][l, 0],
                 wts['ffn_w_gate'][l, 0], wts['ffn_w_up'][l, 0], wts['ffn_w_down'][l, 0])
        if l < n_a:
            x, v = _gmlp(tok, x, m[:, 3], m[:, 4], m[:, 5], wts['norm_g'][l, 1], wts['gmlp_w_uv'][l],
                         wts['gmlp_ln_g'][l], wts['gmlp_ln_b'][l], wts['gmlp_w_sp'][l], wts['gmlp_b_sp'][l],
                         wts['gmlp_w_out'][l])
            v_rows.append(v)
        else:
            j = l - n_a
            q, gates = _qg_proj(tok, x, m[:, 3], m[:, 4], wts['norm_g'][l, 1], wts['nsa_wq'][j],
                                wts['nsa_wg_t'][j], wts['nsa_wg'][j], wts['alibi_q'])
            attn = ctx['attend'](q, gates, attn_ctx)
            x = _out_proj(tok, attn, x, m[:, 5], wts['nsa_w_o'][j])
        x = _ffn(tok, x, m[:, 6], m[:, 7], m[:, 8], wts['norm_g'][l, 2],
                 wts['ffn_w_gate'][l, 1], wts['ffn_w_up'][l, 1], wts['ffn_w_down'][l, 1])
    y = _final(tok, x, f_mod[:, 0], f_mod[:, 1], wts['final_g'])
    return y, kv, v_rows


def kernel(x_prompt, x_sample, cache_kv, state_win_kv, page_table, c_prompt, c_sample, ada_w, ada_b, norm_g, ffn_w_gate, ffn_w_up, ffn_w_down, gmlp_w_uv, gmlp_ln_g, gmlp_ln_b, gmlp_w_sp, gmlp_b_sp, gmlp_w_out, nsa_w_qg, nsa_w_o, kv_norm_g, kv_ada_w, kv_ada_b, kv_w, cmp_w1, cmp_w2, cmp_pe, final_g, final_ada_w, final_ada_b):
    n_p, seq, _ = x_prompt.shape
    n_s, dec_seq, _ = x_sample.shape
    assert dec_seq == 1 and seq % (PAGES_PER_SEG * PAGE) == 0
    depth = ada_w.shape[0]
    n_b = nsa_w_qg.shape[0]
    gd = N_KV * HEAD_DIM
    n_pages = page_table.shape[1]
    assert n_pages == PAGES_PER_SEG
    past = n_pages * PAGE

    nq = N_HEADS * HEAD_DIM
    wg_cols = nsa_w_qg[:, :, nq:]
    wg_pad = jnp.pad(wg_cols.reshape(n_b, D_MODEL, N_KV, HPG * 3), ((0, 0), (0, 0), (0, 0), (0, 16 - HPG * 3)))
    wts = dict(
        norm_g=norm_g, kv_norm_g=kv_norm_g, final_g=final_g,
        ffn_w_gate=ffn_w_gate.astype(BF16), ffn_w_up=ffn_w_up.astype(BF16), ffn_w_down=ffn_w_down.astype(BF16),
        gmlp_w_uv=gmlp_w_uv.astype(BF16), gmlp_ln_g=gmlp_ln_g, gmlp_ln_b=gmlp_ln_b,
        gmlp_w_sp=gmlp_w_sp, gmlp_b_sp=gmlp_b_sp, gmlp_w_out=gmlp_w_out.astype(BF16),
        nsa_wq=nsa_w_qg[:, :, :nq].astype(BF16),
        nsa_wg_t=jnp.swapaxes(wg_pad.reshape(n_b, D_MODEL, N_KV * 16), 1, 2).astype(BF16),
        nsa_wg=jnp.pad(wg_cols, ((0, 0), (0, 0), (0, LANE - N_HEADS * 3))).astype(BF16),
        nsa_w_o=nsa_w_o.astype(BF16),
        kv_w=kv_w.astype(BF16),
        kv_wvt=jnp.concatenate([kv_w[:, 3 * gd:4 * gd], kv_w[:, 5 * gd:6 * gd]], axis=1).T.astype(BF16),
    )
    w1 = cmp_w1.astype(BF16)
    w2 = cmp_w2.astype(BF16)
    w2t = jnp.swapaxes(cmp_w2, 1, 2).astype(BF16)
    pe = jnp.broadcast_to(cmp_pe.astype(BF16).reshape(2, 1, CMP_LEN * HEAD_DIM), (2, 16, CMP_LEN * HEAD_DIM))
    slopes = _alibi_slopes()
    sl2 = slopes * LOG2E
    off_f = jnp.arange(LANE, dtype=F32)

    def split3(x):
        a = x.astype(BF16)
        b = (x - a.astype(F32)).astype(BF16)
        c = (x - a.astype(F32) - b.astype(F32)).astype(BF16)
        return [a, b, c]

    q_cols = split3(-sl2[:, None] * off_f[None, :]) + \
        [jnp.broadcast_to(c[:, None], (N_HEADS, LANE)) for c in split3(sl2)] + \
        [jnp.broadcast_to(c[:, None], (N_HEADS, LANE)) for c in split3(sl2 * CMP_STRIDE)]
    wts['alibi_q'] = jnp.pad(jnp.stack(q_cols, axis=-1), ((0, 0), (0, 0), (0, HEAD_DIM - 9)))
    ones, zeros, offs = jnp.ones((LANE,), BF16), jnp.zeros((LANE,), BF16), off_f.astype(BF16)
    wts['alibi_k'] = jnp.pad(jnp.stack([ones] * 3 + [offs] * 3 + [zeros] * 3, axis=-1), ((0, 0), (0, HEAD_DIM - 9)))
    alibi_kc = jnp.pad(jnp.stack([ones] * 3 + [zeros] * 3 + [offs] * 3, axis=-1), ((0, 0), (0, HEAD_DIM - 9)))

    n_c = n_p + n_s
    c_all = jnp.pad(jnp.concatenate([c_prompt, c_sample], axis=0), ((0, (-n_c) % 8), (0, 0)))
    mod_all = _mod_linear(c_all, ada_w, ada_b)
    kv_mod_all = _mod_linear(c_all, kv_ada_w[None], kv_ada_b[None])[0]
    f_mod_all = _mod_linear(c_all, final_ada_w[None], final_ada_b[None])[0]

    def rows(a, lo, hi, k):
        return a[..., lo:hi, :].reshape(a.shape[:-2] + (hi - lo, k, D_MODEL))

    tok_p = _Tok(n_p, seq, per_row=False)
    n_seg = seq // (PAGES_PER_SEG * PAGE)
    n_qt = seq // LANE
    ncp = n_seg * PAGES_PER_SEG * 8
    n_sel_p = seq // SEL_BLOCK
    assert n_sel_p <= LANE
    cov_t = jnp.pad(_cover(ncp, n_sel_p).T, ((0, LANE - n_sel_p), (0, 0)))
    sl_rows = jnp.repeat(sl2.reshape(N_KV, 1, HPG), LANE, axis=2)
    onehot = (jnp.arange(LANE)[None, None, :] ==
              (2 * jnp.arange(n_qt)[:, None, None] + jnp.arange(LANE)[None, :, None] // SEL_BLOCK)).astype(BF16)

    def prepare_p(kv_out):
        kv, k_nat, v_t = kv_out
        pages_per_seq = seq // PAGE
        base = jnp.arange(n_p * n_seg, dtype=jnp.int32)[:, None] * PAGES_PER_SEG
        table = jnp.minimum(base + jnp.arange(PAGES_PER_SEG + 1, dtype=jnp.int32)[None, :],
                            n_p * pages_per_seq - 1)
        kc, vct = _compress(kv.reshape(n_p * pages_per_seq, PAGE, N_KV_PROJ * gd), table, w1, w2, w2t, pe,
                            alibi_kc, n_p, n_seg)
        return kv, (kc, vct, k_nat, v_t)

    def attend_p(q, gates, c):
        kc, vct, k_nat, v_t = c
        return _nsa_prompt(q, gates, kc, vct, k_nat, v_t, cov_t, onehot, sl_rows, n_p, seq)

    y_p, kv_p, _ = _trunk(tok_p, x_prompt.reshape(n_p * seq, D_MODEL),
                          rows(mod_all, 0, n_p, 9), rows(kv_mod_all, 0, n_p, 2), rows(f_mod_all, 0, n_p, 2),
                          wts, dict(prepare=prepare_p, attend=attend_p))
    kv_p = kv_p.reshape(n_p, seq, N_KV_PROJ, N_KV, HEAD_DIM)
    kv_prompt = kv_p[:, :, :4]
    win_prompt = kv_p[:, seq - min(WINDOW, seq):, 4:6]

    tok_s = _Tok(n_s, 1, per_row=True)
    cache_pages = cache_kv.reshape(cache_kv.shape[0], PAGE, 4 * gd).astype(BF16)
    buf = state_win_kv.shape[1]
    win_flat = state_win_kv.reshape(n_s, buf, 2 * gd)
    n_sel_s = (past + 1 + SEL_BLOCK - 1) // SEL_BLOCK
    cov_s = jnp.pad(_cover(LANE, n_sel_s).T, ((0, LANE - n_sel_s), (0, 0)))
    n_keys = (n_pages + 1) * LANE
    expand_s = (jnp.arange(LANE)[:, None] == (jnp.arange(n_keys)[None, :] // SEL_BLOCK)).astype(BF16)
    slope16 = jnp.broadcast_to(slopes[:, None], (N_HEADS, LANE))

    def prepare_s(kv):
        table = jnp.concatenate([page_table, page_table[:, -1:]], axis=1)
        kc, vct = _compress(cache_pages, table, w1, w2, w2t, pe, alibi_kc, n_s, 1)
        return kv, (kc, vct, kv)

    def attend_s(q, gates, c):
        kc, vct, kv = c
        o = _nsa_sample(q.reshape(n_s, N_HEADS, HEAD_DIM), gates[:, :N_HEADS * 3].reshape(n_s, N_HEADS, 3),
                        kc, vct, cache_pages, page_table, win_flat, kv.reshape(n_s, 1, N_KV_PROJ * gd),
                        cov_s, expand_s, slope16)
        return o.reshape(n_s, N_HEADS * HEAD_DIM)

    y_s, kv_s, v_rows = _trunk(tok_s, x_sample.reshape(n_s, D_MODEL),
                               rows(mod_all, n_p, n_c, 9), rows(kv_mod_all, n_p, n_c, 2),
                               rows(f_mod_all, n_p, n_c, 2), wts, dict(prepare=prepare_s, attend=attend_s))
    kv_s = kv_s.reshape(n_s, 1, N_KV_PROJ, N_KV, HEAD_DIM)
    kv_sample = kv_s[:, :, :4]
    win_sample = jnp.concatenate([state_win_kv, kv_s[:, :, 4:6]], axis=1)[:, 1:]
    gmlp_v_sample = jnp.stack([v.reshape(n_s, 1, D_V) for v in v_rows])

    return (y_p.reshape(n_p, seq, D_MODEL), y_s.reshape(n_s, 1, D_MODEL), kv_prompt, kv_sample,
            win_prompt, win_sample, gmlp_v_sample)
```

```python
import functools

import jax
import jax.numpy as jnp
from jax import lax
from jax.experimental import pallas as pl
from jax.experimental.pallas import tpu as pltpu

F32 = jnp.float32
BF16 = jnp.bfloat16

D_MODEL = 1024
D_FF = 2816
D_V = 3072
CHUNK = 128
N_GROUPS_A = 8
CG = D_V // N_GROUPS_A
N_HEADS = 16
HEAD_DIM = 64
N_KV = 4
HPG = N_HEADS // N_KV
CMP_LEN = 32
CMP_STRIDE = 16
CMP_HID = 256
SEL_BLOCK = 64
N_SEL = 16
WINDOW = 512
PAGE = 128
N_KV_PROJ = 6
EPS = 1e-6
NEG = -1e30
M_FLOOR = -1e29
PICKED = -3e38
FORCE_BONUS = 1e4
LOG2E = 1.4426950408889634
MASK_OFF = 1e30

LANE = 128
VMEM_LIMIT = 56 * 1024 * 1024
FF_CHUNK = 256
UV_CHUNK = 512
VROWS = HEAD_DIM + 16
N_NEAR = 8
SAMPLE_SEQS = 4
PAGES_PER_SEG = 16
XROWS = PAGES_PER_SEG * 8 + 8


def _dot(a, b):
    return jnp.dot(a, b, preferred_element_type=F32)


def _dot_nt(a, b):
    return lax.dot_general(a, b, (((1,), (1,)), ((), ())), preferred_element_type=F32)


def _split_bf16(x):
    hi = x.astype(BF16)
    lo = (x - hi.astype(F32)).astype(BF16)
    return hi, lo


def _params(n_grid):
    return pltpu.CompilerParams(dimension_semantics=("arbitrary",) * n_grid,
                                vmem_limit_bytes=VMEM_LIMIT)


def _full_spec(shape):
    nd = len(shape)
    return pl.BlockSpec(shape, lambda *_: (0,) * nd, pipeline_mode=pl.Buffered(1))


def _ada_norm(x, g, shift, scale):
    ms = jnp.mean(x * x, axis=-1, keepdims=True)
    h = x * lax.rsqrt(ms + EPS) * g
    return h * (1.0 + scale) + shift


def _mod_kernel(c_ref, w_ref, b_ref, o_ref):
    c = c_ref[...]
    a = c * jax.nn.sigmoid(c)
    w = w_ref[0]
    a_hi, a_lo = _split_bf16(a)
    w_hi, w_lo = _split_bf16(w)
    o_ref[0] = _dot(a_hi, w_hi) + _dot(a_hi, w_lo) + _dot(a_lo, w_hi) + b_ref[0]


def _mod_linear(c, w, b):
    n_l, d, n = w.shape
    m = c.shape[0]
    tn = 1024
    return pl.pallas_call(
        _mod_kernel,
        grid=(n_l, n // tn),
        in_specs=[pl.BlockSpec((m, d), lambda l, j: (0, 0)),
                  pl.BlockSpec((1, d, tn), lambda l, j: (l, 0, j)),
                  pl.BlockSpec((1, 1, tn), lambda l, j: (l, 0, j))],
        out_specs=pl.BlockSpec((1, m, tn), lambda l, j: (l, 0, j)),
        out_shape=jax.ShapeDtypeStruct((n_l, m, n), F32),
        compiler_params=_params(2),
    )(c, w, b.reshape(n_l, 1, n))


class _Tok:
    def __init__(self, n_seq, seq_len, per_row):
        self.n_seq, self.seq_len = n_seq, seq_len
        self.n_tok = n_seq * seq_len
        self.per_row = per_row

    def tiles(self, tm):
        if self.per_row:
            return self.n_tok, 1
        assert self.seq_len % tm == 0
        return tm, self.n_tok // tm

    def x_spec(self, tm, width=D_MODEL):
        return pl.BlockSpec((tm, width), lambda i: (i, 0))

    def mod_spec(self, tm):
        if self.per_row:
            return pl.BlockSpec((1, tm, D_MODEL), lambda i: (0, 0, 0))
        per_seq = self.seq_len // tm
        return pl.BlockSpec((1, 1, D_MODEL), lambda i: (i // per_seq, 0, 0))

    def mod_arr(self, m):
        if self.per_row:
            return m.reshape(1, self.n_tok, D_MODEL)
        return m.reshape(self.n_seq, 1, D_MODEL)


def _ffn_kernel(x_ref, sh_ref, sc_ref, gt_ref, g_ref, wg_ref, wu_ref, wd_ref, o_ref, a_scr):
    x = x_ref[...]
    hb = _ada_norm(x, g_ref[...], sh_ref[0], sc_ref[0]).astype(BF16)
    for c in range(D_FF // FF_CHUNK):
        cs = slice(c * FF_CHUNK, (c + 1) * FF_CHUNK)
        gate = _dot(hb, wg_ref[:, cs])
        up = _dot(hb, wu_ref[:, cs])
        a_scr[:, cs] = (gate * jax.nn.sigmoid(gate) * up).astype(BF16)
    y = _dot(a_scr[...], wd_ref[...])
    o_ref[...] = x + (0.5 * gt_ref[0]) * y


def _ffn(tok, x, shift, scale, gate, g, wg, wu, wd):
    tm, nt = tok.tiles(512)
    return pl.pallas_call(
        _ffn_kernel,
        grid=(nt,),
        in_specs=[tok.x_spec(tm), tok.mod_spec(tm), tok.mod_spec(tm), tok.mod_spec(tm),
                  _full_spec((1, D_MODEL)), _full_spec((D_MODEL, D_FF)),
                  _full_spec((D_MODEL, D_FF)), _full_spec((D_FF, D_MODEL))],
        out_specs=tok.x_spec(tm),
        out_shape=jax.ShapeDtypeStruct((tok.n_tok, D_MODEL), F32),
        scratch_shapes=[pltpu.VMEM((tm, D_FF), BF16)],
        compiler_params=_params(1),
    )(x, tok.mod_arr(shift), tok.mod_arr(scale), tok.mod_arr(gate), g.reshape(1, D_MODEL), wg, wu, wd)


def _gmlp_uv(x_ref, sh_ref, sc_ref, g_ref, wuv_ref, lng_ref, lnb_ref, u_scr, v_scr):
    hb = _ada_norm(x_ref[...], g_ref[...], sh_ref[0], sc_ref[0]).astype(BF16)
    for c in range(D_V // UV_CHUNK):
        cs = slice(c * UV_CHUNK, (c + 1) * UV_CHUNK)
        cv = slice(D_V + c * UV_CHUNK, D_V + (c + 1) * UV_CHUNK)
        u_scr[:, cs] = jax.nn.gelu(_dot(hb, wuv_ref[:, cs]))
        v_scr[:, cs] = jax.nn.gelu(_dot(hb, wuv_ref[:, cv]))
    v = v_scr[...]
    mu = jnp.mean(v, axis=-1, keepdims=True)
    vc = v - mu
    var = jnp.mean(vc * vc, axis=-1, keepdims=True)
    return vc * lax.rsqrt(var + EPS) * lng_ref[...] + lnb_ref[...]


def _gmlp_prompt_kernel(x_ref, sh_ref, sc_ref, gt_ref, g_ref, wuv_ref, lng_ref, lnb_ref,
                        wsp_ref, bsp_ref, wout_ref, o_ref, u_scr, v_scr, a_scr):
    v_scr[...] = _gmlp_uv(x_ref, sh_ref, sc_ref, g_ref, wuv_ref, lng_ref, lnb_ref, u_scr, v_scr)
    row = lax.broadcasted_iota(jnp.int32, (CHUNK, CHUNK), 0)
    col = lax.broadcasted_iota(jnp.int32, (CHUNK, CHUNK), 1)
    tm = x_ref.shape[0]
    for grp in range(N_GROUPS_A):
        w = jnp.where(row >= col, wsp_ref[grp], 0.0).astype(BF16)
        cs = slice(grp * CG, (grp + 1) * CG)
        for n in range(tm // CHUNK):
            rs = slice(n * CHUNK, (n + 1) * CHUNK)
            s = _dot(w, v_scr[rs, cs].astype(BF16)) + bsp_ref[:, cs]
            a_scr[rs, cs] = (u_scr[rs, cs] * s).astype(BF16)
    y = _dot(a_scr[...], wout_ref[...])
    o_ref[...] = x_ref[...] + gt_ref[0] * y


def _gmlp_sample_kernel(x_ref, sh_ref, sc_ref, gt_ref, g_ref, wuv_ref, lng_ref, lnb_ref,
                        wrow_ref, brow_ref, wout_ref, o_ref, vout_ref, u_scr, v_scr):
    vn = _gmlp_uv(x_ref, sh_ref, sc_ref, g_ref, wuv_ref, lng_ref, lnb_ref, u_scr, v_scr)
    vout_ref[...] = vn
    s = vn * wrow_ref[...] + brow_ref[...]
    y = _dot((u_scr[...] * s).astype(BF16), wout_ref[...])
    o_ref[...] = x_ref[...] + gt_ref[0] * y


def _gmlp(tok, x, shift, scale, gate, g, wuv, ln_g, ln_b, w_sp, b_sp, wout):
    tm, nt = tok.tiles(512)
    common = [tok.x_spec(tm), tok.mod_spec(tm), tok.mod_spec(tm), tok.mod_spec(tm),
              _full_spec((1, D_MODEL)), _full_spec((D_MODEL, 2 * D_V)),
              _full_spec((1, D_V)), _full_spec((1, D_V))]
    args = [x, tok.mod_arr(shift), tok.mod_arr(scale), tok.mod_arr(gate), g.reshape(1, D_MODEL), wuv,
            ln_g.reshape(1, D_V), ln_b.reshape(1, D_V)]
    x_shape = jax.ShapeDtypeStruct((tok.n_tok, D_MODEL), F32)
    if tok.per_row:
        wrow = jnp.repeat(w_sp[:, 0, 0], CG).reshape(1, D_V)
        brow = jnp.repeat(b_sp[:, 0], CG).reshape(1, D_V)
        return pl.pallas_call(
            _gmlp_sample_kernel,
            grid=(nt,),
            in_specs=common + [_full_spec((1, D_V)), _full_spec((1, D_V)), _full_spec((D_V, D_MODEL))],
            out_specs=[tok.x_spec(tm), tok.x_spec(tm, D_V)],
            out_shape=[x_shape, jax.ShapeDtypeStruct((tok.n_tok, D_V), F32)],
            scratch_shapes=[pltpu.VMEM((tm, D_V), F32), pltpu.VMEM((tm, D_V), F32)],
            compiler_params=_params(1),
        )(*args, wrow, brow, wout)
    bias = jnp.repeat(b_sp.T, CG, axis=1)
    out = pl.pallas_call(
        _gmlp_prompt_kernel,
        grid=(nt,),
        in_specs=common + [_full_spec((N_GROUPS_A, CHUNK, CHUNK)), _full_spec((CHUNK, D_V)),
                           _full_spec((D_V, D_MODEL))],
        out_specs=tok.x_spec(tm),
        out_shape=x_shape,
        scratch_shapes=[pltpu.VMEM((tm, D_V), F32), pltpu.VMEM((tm, D_V), F32),
                        pltpu.VMEM((tm, D_V), BF16)],
        compiler_params=_params(1),
    )(*args, w_sp, bias, wout)
    return out, None


def _kv_prompt_kernel(x_ref, sh_ref, sc_ref, g_ref, w_ref, wvt_ref, ak_ref, kv_ref, k_ref, vt_ref):
    hb = _ada_norm(x_ref[...], g_ref[...], sh_ref[0], sc_ref[0]).astype(BF16)
    kv = _dot(hb, w_ref[...])
    kv_ref[...] = kv
    vt = _dot_nt(wvt_ref[...], hb).astype(BF16)
    tm = x_ref.shape[0]
    gd = N_KV * HEAD_DIM
    ones_rows = jnp.where(lax.broadcasted_iota(jnp.int32, (VROWS - HEAD_DIM, LANE), 0) == 0, 1.0, 0.0).astype(BF16)
    for c in range(tm // LANE):
        rs = slice(c * LANE, (c + 1) * LANE)
        for hd in range(2 * N_KV):
            vt_ref[0, c, hd * VROWS:hd * VROWS + HEAD_DIM, :] = vt[hd * HEAD_DIM:(hd + 1) * HEAD_DIM, rs]
            vt_ref[0, c, hd * VROWS + HEAD_DIM:(hd + 1) * VROWS, :] = ones_rows
        for j, slot in enumerate((2, 4)):
            for grp in range(N_KV):
                col = slot * gd + grp * HEAD_DIM
                k_ref[0, j * N_KV + grp, c, :, 0:HEAD_DIM] = kv[rs, col:col + HEAD_DIM].astype(BF16)
                k_ref[0, j * N_KV + grp, c, :, HEAD_DIM:LANE] = ak_ref[...]


def _kv_sample_kernel(x_ref, sh_ref, sc_ref, g_ref, w_ref, kv_ref):
    hb = _ada_norm(x_ref[...], g_ref[...], sh_ref[0], sc_ref[0]).astype(BF16)
    kv_ref[...] = _dot(hb, w_ref[...])


def _kv_proj(tok, x, shift, scale, g, w, wvt, alibi_k):
    tm, nt = tok.tiles(512)
    n_kv = N_KV_PROJ * N_KV * HEAD_DIM
    in_specs = [tok.x_spec(tm), tok.mod_spec(tm), tok.mod_spec(tm), _full_spec((1, D_MODEL)),
                _full_spec((D_MODEL, n_kv))]
    args = [x, tok.mod_arr(shift), tok.mod_arr(scale), g.reshape(1, D_MODEL), w]
    kv_shape = jax.ShapeDtypeStruct((tok.n_tok, n_kv), F32)
    if tok.per_row:
        return pl.pallas_call(
            _kv_sample_kernel, grid=(nt,), in_specs=in_specs, out_specs=tok.x_spec(tm, n_kv),
            out_shape=kv_shape, compiler_params=_params(1))(*args)
    per_seq = tok.seq_len // tm
    n_qt = tok.seq_len // LANE
    sub = tm // LANE
    return pl.pallas_call(
        _kv_prompt_kernel,
        grid=(nt,),
        in_specs=in_specs + [_full_spec((2 * N_KV * HEAD_DIM, D_MODEL)), _full_spec((LANE, HEAD_DIM))],
        out_specs=[tok.x_spec(tm, n_kv),
                   pl.BlockSpec((1, 2 * N_KV, sub, LANE, LANE),
                                lambda i: (i // per_seq, 0, i % per_seq, 0, 0)),
                   pl.BlockSpec((1, sub, 2 * N_KV * VROWS, LANE),
                                lambda i: (i // per_seq, i % per_seq, 0, 0))],
        out_shape=[kv_shape,
                   jax.ShapeDtypeStruct((tok.n_seq, 2 * N_KV, n_qt, LANE, LANE), BF16),
                   jax.ShapeDtypeStruct((tok.n_seq, n_qt, 2 * N_KV * VROWS, LANE), BF16)],
        compiler_params=_params(1),
    )(*args, wvt, alibi_k)


def _compress_kernel(tbl_ref, *refs):
    pages = refs[:PAGES_PER_SEG + 1]
    perm_ref, w1_ref, w2_ref, w2t_ref, pe_ref, akc_ref, kc_ref, vct_ref, x_scr, hb_scr = refs[PAGES_PER_SEG + 1:]
    del tbl_ref
    perm = perm_ref[...]
    n_heads = 2 * N_KV
    half = CMP_STRIDE * HEAD_DIM
    for p in range(PAGES_PER_SEG + 1):
        xp = _dot(perm, pages[p][0].astype(BF16))
        for hd in range(n_heads):
            cs = slice(hd * HEAD_DIM, (hd + 1) * HEAD_DIM)
            x_scr[hd, p * 8:(p + 1) * 8, :] = jnp.concatenate(
                [xp[s * 8:(s + 1) * 8, cs] for s in range(CMP_STRIDE)], axis=1)
    n_blk = PAGES_PER_SEG * 8
    for slot in range(2):
        x = x_scr[slot * N_KV:(slot + 1) * N_KV].reshape(N_KV * XROWS, half).astype(BF16)
        ha = _dot(x, w1_ref[slot, 0:half, :])
        hb_scr[...] = _dot(x, w1_ref[slot, half:2 * half, :])
        pe_hid = _dot(pe_ref[slot], w1_ref[slot])[0:1]
        for grp in range(N_KV):
            hid = ha[grp * XROWS:grp * XROWS + n_blk] + hb_scr[pl.ds(grp * XROWS + 1, n_blk), :] + pe_hid
            act = jax.nn.gelu(hid).astype(BF16)
            if slot == 0:
                kc_ref[0, grp, :, 0:HEAD_DIM] = _dot(act, w2_ref[0]).astype(BF16)
                kc_ref[0, grp, :, HEAD_DIM:LANE] = akc_ref[...]
            else:
                vct_ref[0, grp, 0:HEAD_DIM, :] = _dot_nt(w2t_ref[1], act).astype(BF16)
                vct_ref[0, grp, HEAD_DIM:VROWS, :] = jnp.where(
                    lax.broadcasted_iota(jnp.int32, (VROWS - HEAD_DIM, n_blk), 0) == 0, 1.0, 0.0).astype(BF16)


def _compress(src, table, w1, w2, w2t, pe, alibi_kc, n_seq, n_seg):
    n_blk = PAGES_PER_SEG * 8
    gd2 = 2 * N_KV * HEAD_DIM
    dst = jnp.arange(PAGE)
    src_row = (dst % 8) * CMP_STRIDE + dst // 8
    perm = (jnp.arange(PAGE)[None, :] == src_row[:, None]).astype(BF16)

    def page_spec(p):
        return pl.BlockSpec((1, PAGE, gd2), lambda i, tbl, p=p: (tbl[i, p], 0, 0))

    grid_spec = pltpu.PrefetchScalarGridSpec(
        num_scalar_prefetch=1,
        grid=(n_seq * n_seg,),
        in_specs=[page_spec(p) for p in range(PAGES_PER_SEG + 1)] + [
            pl.BlockSpec((PAGE, PAGE), lambda i, tbl: (0, 0)),
            pl.BlockSpec(w1.shape, lambda i, tbl: (0, 0, 0)),
            pl.BlockSpec(w2.shape, lambda i, tbl: (0, 0, 0)),
            pl.BlockSpec(w2t.shape, lambda i, tbl: (0, 0, 0)),
            pl.BlockSpec(pe.shape, lambda i, tbl: (0, 0, 0)),
            pl.BlockSpec((LANE, HEAD_DIM), lambda i, tbl: (0, 0))],
        out_specs=[pl.BlockSpec((1, N_KV, n_blk, LANE), lambda i, tbl: (i // n_seg, 0, i % n_seg, 0)),
                   pl.BlockSpec((1, N_KV, VROWS, n_blk), lambda i, tbl: (i // n_seg, 0, 0, i % n_seg))],
        scratch_shapes=[pltpu.VMEM((2 * N_KV, XROWS, CMP_STRIDE * HEAD_DIM), F32),
                        pltpu.VMEM((N_KV * XROWS, CMP_HID), F32)])
    return pl.pallas_call(
        _compress_kernel,
        grid_spec=grid_spec,
        out_shape=[jax.ShapeDtypeStruct((n_seq, N_KV, n_seg * n_blk, LANE), BF16),
                   jax.ShapeDtypeStruct((n_seq, N_KV, VROWS, n_seg * n_blk), BF16)],
        compiler_params=_params(1),
    )(table, *([src] * (PAGES_PER_SEG + 1)), perm, w1, w2, w2t, pe, alibi_kc)


def _qg_prompt_kernel(x_ref, sh_ref, sc_ref, g_ref, wq_ref, wgt_ref, aq_ref, q_ref, gt_ref):
    hb = _ada_norm(x_ref[...], g_ref[...], sh_ref[0], sc_ref[0]).astype(BF16)
    q = _dot(hb, wq_ref[...]) * (HEAD_DIM ** -0.5 * LOG2E)
    tm = x_ref.shape[0]
    for h in range(N_HEADS):
        q_ref[0, h, :, 0:HEAD_DIM] = q[:, h * HEAD_DIM:(h + 1) * HEAD_DIM].astype(BF16)
        for c in range(tm // LANE):
            q_ref[0, h, c * LANE:(c + 1) * LANE, HEAD_DIM:LANE] = aq_ref[h]
    gates = jax.nn.sigmoid(_dot_nt(wgt_ref[...], hb))
    for grp in range(N_KV):
        gt_ref[0, grp] = gates[grp * 16:(grp + 1) * 16, :]


def _qg_sample_kernel(x_ref, sh_ref, sc_ref, g_ref, wq_ref, wg_ref, q_ref, gt_ref):
    hb = _ada_norm(x_ref[...], g_ref[...], sh_ref[0], sc_ref[0]).astype(BF16)
    q_ref[...] = _dot(hb, wq_ref[...]) * (HEAD_DIM ** -0.5)
    gt_ref[...] = jax.nn.sigmoid(_dot(hb, wg_ref[...]))


def _qg_proj(tok, x, shift, scale, g, wq, wg_t, wg_nat, alibi_q):
    tm, nt = tok.tiles(512)
    in_specs = [tok.x_spec(tm), tok.mod_spec(tm), tok.mod_spec(tm), _full_spec((1, D_MODEL)),
                _full_spec((D_MODEL, D_MODEL))]
    args = [x, tok.mod_arr(shift), tok.mod_arr(scale), g.reshape(1, D_MODEL), wq]
    if tok.per_row:
        return pl.pallas_call(
            _qg_sample_kernel, grid=(nt,),
            in_specs=in_specs + [_full_spec((D_MODEL, LANE))],
            out_specs=[tok.x_spec(tm), tok.x_spec(tm, LANE)],
            out_shape=[jax.ShapeDtypeStruct((tok.n_tok, D_MODEL), F32),
                       jax.ShapeDtypeStruct((tok.n_tok, LANE), F32)],
            compiler_params=_params(1))(*args, wg_nat)
    per_seq = tok.seq_len // tm
    return pl.pallas_call(
        _qg_prompt_kernel, grid=(nt,),
        in_specs=in_specs + [_full_spec((N_KV * 16, D_MODEL)), _full_spec((N_HEADS, LANE, HEAD_DIM))],
        out_specs=[pl.BlockSpec((1, N_HEADS, tm, LANE), lambda i: (i // per_seq, 0, i % per_seq, 0)),
                   pl.BlockSpec((1, N_KV, 16, tm), lambda i: (i // per_seq, 0, 0, i % per_seq))],
        out_shape=[jax.ShapeDtypeStruct((tok.n_seq, N_HEADS, tok.seq_len, LANE), BF16),
                   jax.ShapeDtypeStruct((tok.n_seq, N_KV, 16, tok.seq_len), F32)],
        compiler_params=_params(1))(*args, wg_t, alibi_q)


def _select_blocks(work, idx_f, axis, rounds):
    for _ in range(rounds):
        mx = jnp.max(work, axis=axis, keepdims=True)
        first = jnp.min(jnp.where(work == mx, idx_f, 1e9), axis=axis, keepdims=True)
        first = jnp.where(mx > NEG / 2, first, -1.0)
        work = jnp.where(idx_f == first, PICKED, work)
        yield
    return jnp.where(work == PICKED, 1.0, 0.0)


def _softmax_group(scores, offsets, m_old, exp_dtype=F32, values=None):
    m_new = m_old
    for s, off in zip(scores, offsets):
        m_new = jnp.maximum(m_new, jnp.max(s, axis=0, keepdims=True) - off)
        yield
    alpha = jnp.exp2(m_old - m_new)
    probs = []
    acc = None
    for k, (s, off) in enumerate(zip(scores, offsets)):
        probs.append(jnp.exp2((s - (m_new + off)).astype(exp_dtype)))
        if values is not None and (k % 2 == 1 or k == len(scores) - 1):
            n = 2 if k % 2 == 1 else 1
            part = _dot(jnp.concatenate(values[k + 1 - n:k + 1], axis=1), jnp.concatenate(probs[-n:], axis=0))
            acc = part if acc is None else acc + part
        yield
    return m_new, alpha, (probs if values is None else acc)


def _run(gen):
    try:
        while True:
            next(gen)
    except StopIteration as stop:
        return stop.value


def _interleave(*gens, steps=None):
    live = [(g, 1 if steps is None else steps[k]) for k, g in enumerate(gens)]
    while live:
        for item in list(live):
            try:
                for _ in range(item[1]):
                    next(item[0])
            except StopIteration:
                live.remove(item)


def _normalise(acc):
    l = acc[HEAD_DIM:HEAD_DIM + 1]
    return acc[0:HEAD_DIM] * jnp.where(l > 0.0, 1.0 / l, 0.0)


def _nsa_prompt_kernel(q_ref, g_ref, kc_ref, vct_ref, ks_ref, vst_ref, kw_ref, vwt_ref, cov_ref, hot_ref, sl_ref,
                       o_ref, qp_scr, oc_scr, m_scr, acc_scr, bits_ref, *, ncp, n_qt):
    j = pl.program_id(2)
    cols = HPG * LANE
    key_io = lax.broadcasted_iota(jnp.int32, (LANE, LANE), 0)
    tok_io = lax.broadcasted_iota(jnp.int32, (LANE, LANE), 1)
    sl = sl_ref[0]
    m_init = jnp.full((1, cols), M_FLOOR, F32)

    def tile4(a):
        return jnp.concatenate([a] * HPG, axis=1)

    @pl.when(j == 0)
    def _():
        qp_scr[1] = jnp.zeros(qp_scr.shape[1:], BF16)
        oc_scr[1] = jnp.zeros(oc_scr.shape[1:], F32)
        for w in range(LANE // 32):
            bits_ref[LANE // 32 + w] = 0

    i = jnp.maximum(j - 1, 0)
    slot = (j + 1) % 2

    def sel_keys(jp):
        return jnp.concatenate([hot_ref[jp], ks_ref[0, 0, jp]], axis=1)

    fi_a = i.astype(F32)
    qp = qp_scr[slot]
    rel = tile4((tok_io - key_io).astype(F32))

    def window_branch():
        qx = qp[:, LANE:2 * LANE]

        n_wt = WINDOW // LANE
        scores, offsets, vts = [], [], []
        tiles = [jnp.maximum(i - n_wt + kt_i, 0) for kt_i in range(n_wt + 1)]
        s_all = _dot_nt(jnp.concatenate([kw_ref[0, 0, t] for t in tiles], axis=0), qx)
        yield
        for kt_i in range(n_wt + 1):
            jt = i - n_wt + kt_i
            jt_c = tiles[kt_i]
            s = s_all[kt_i * LANE:(kt_i + 1) * LANE]
            if kt_i == 0:
                s = jnp.where(rel <= 0.0, s, NEG)
            elif kt_i == n_wt:
                s = jnp.where(rel >= 0.0, s, NEG)
            scores.append(s)
            offsets.append(sl * float(LANE * (n_wt - kt_i)) + jnp.where(jt >= 0, 0.0, MASK_OFF).astype(F32))
            vts.append(vwt_ref[0, jt_c])
        _, _, acc = yield from _softmax_group(scores, offsets, m_init, BF16, vts)
        return _normalise(acc)

    def near_tiles():
        fi = fi_a
        tiles = [0, i] + [jnp.maximum(i - r, 0) for r in range(1, N_NEAR + 1)]
        s_all = _dot_nt(jnp.concatenate([sel_keys(t) for t in tiles], axis=0), qp)
        yield
        scores = [s_all[k * LANE:(k + 1) * LANE] for k in range(len(tiles))]
        scores[1] = jnp.where(rel >= 0.0, scores[1], NEG)
        offsets = [sl * (LANE * fi) + jnp.where(i > N_NEAR, 0.0, MASK_OFF).astype(F32), jnp.zeros((1, cols), F32)]
        for r in range(1, N_NEAR + 1):
            offsets.append(sl * float(LANE * r) + jnp.where(i - r >= 0, 0.0, MASK_OFF).astype(F32))
        vts = [vst_ref[0, t] for t in tiles]
        m, _, acc = yield from _softmax_group(scores, offsets, m_init, BF16, vts)
        m_scr[...] = m
        acc_scr[...] = acc

    def selection_half():
        i_s = jnp.minimum(j, n_qt - 1)
        slot_s = j % 2
        fi = i_s.astype(F32)
        qx = q_ref[0].reshape(cols, LANE)

        n_chunks = ncp // LANE
        rel_c = tile4((tok_io - CMP_STRIDE * key_io).astype(F32))
        scores, offsets = [], []
        s_all = _dot_nt(kc_ref[0, 0], qx)
        yield
        for c in range(n_chunks):
            base = float(CMP_STRIDE * LANE * c + CMP_LEN - 1)
            s_c = s_all[c * LANE:(c + 1) * LANE]
            scores.append(jnp.where(rel_c >= base - LANE * fi, s_c, NEG))
            offsets.append(sl * (LANE * fi - base))
        _, _, probs = yield from _softmax_group(scores, offsets, m_init)
        p_hi, p_lo = _split_bf16(jnp.concatenate(probs, axis=0))
        acc_c = _dot(vct_ref[0, 0], p_hi)
        oc_scr[slot_s] = _normalise(acc_c)
        yield
        l_c = acc_c[HEAD_DIM:HEAD_DIM + 1]
        cov = cov_ref[...]
        imp4 = (_dot(cov, p_hi) + _dot(cov, p_lo)) * jnp.where(l_c > 0.0, 1.0 / l_c, 0.0)
        imp = imp4[:, 0:LANE]
        for h in range(1, HPG):
            imp = imp + imp4[:, h * LANE:(h + 1) * LANE]
        yield

        q_blk = 2 * i_s + tok_io // SEL_BLOCK
        forced = (key_io == 0) | (key_io == q_blk) | (key_io == q_blk - 1)
        valid = key_io <= q_blk
        work = jnp.where(valid & jnp.logical_not(forced), imp, NEG)
        sel = yield from _select_blocks(work, key_io.astype(F32), 0, N_SEL - 3)
        sel = jnp.where(valid & forced, 1.0, sel)
        sel_bias_t = jnp.where(sel.T > 0.5, 0.0, NEG).astype(BF16)
        for h in range(HPG):
            qp_scr[slot_s, h * LANE:(h + 1) * LANE, 0:LANE] = sel_bias_t
            qp_scr[slot_s, h * LANE:(h + 1) * LANE, LANE:2 * LANE] = q_ref[0, h]

        cnt = _dot_nt(jnp.ones((8, LANE), BF16), sel.astype(BF16))[0:1]
        lane = lax.broadcasted_iota(jnp.int32, (1, LANE), 1)
        live = (cnt > 0.0) & (lane >= 2) & (lane < 2 * (i_s - N_NEAR))
        for w in range(LANE // 32):
            bits_ref[slot_s * (LANE // 32) + w] = jnp.sum(
                jnp.where(live & (lane // 32 == w), jnp.left_shift(1, lane % 32), 0))

    box = []

    def window_result():
        box.append((yield from window_branch()))

    _interleave(selection_half(), window_result(), near_tiles(), steps=(2, 1, 1))
    o_win = box[0]

    def group_body(k, carry):
        word = bits_ref[slot * (LANE // 32) + k // 4]
        used = jnp.right_shift(word, (8 * k) % 32) & 255

        @pl.when(used != 0)
        def _():
            offsets, vts = [], []
            s_all = _dot_nt(jnp.concatenate([sel_keys(4 * k + r) for r in range(4)], axis=0), qp_scr[slot])
            scores = [s_all[r * LANE:(r + 1) * LANE] for r in range(4)]
            for r in range(4):
                jp = 4 * k + r
                dead = jnp.where((jp == 0) | (jp >= i - N_NEAR), MASK_OFF, 0.0).astype(F32)
                offsets.append(sl * (LANE * (i - jp)).astype(F32) + dead)
                vts.append(vst_ref[0, jp])
            m, alpha, acc = _run(_softmax_group(scores, offsets, m_scr[...], BF16, vts))
            m_scr[...] = m
            acc_scr[...] = alpha * acc_scr[...] + acc
        return carry

    lax.fori_loop(0, jnp.maximum(i - N_NEAR + 3, 0) // 4, group_body, 0)
    o_sel = _normalise(acc_scr[...])

    o_cmp = oc_scr[slot]
    for h in range(HPG):
        cs = slice(h * LANE, (h + 1) * LANE)
        g_c = g_ref[0, 0, 3 * h:3 * h + 1, :]
        g_s = g_ref[0, 0, 3 * h + 1:3 * h + 2, :]
        g_w = g_ref[0, 0, 3 * h + 2:3 * h + 3, :]
        o_ref[0, h * HEAD_DIM:(h + 1) * HEAD_DIM, :] = g_c * o_cmp[:, cs] + g_s * o_sel[:, cs] + g_w * o_win[:, cs]


def _nsa_prompt(q, gates_t, kc, vct, k_ext, v_t, cov_t, onehot, sl_rows, n_seq, seq_len):
    n_qt = seq_len // LANE
    ncp = kc.shape[2]
    assert n_qt % 4 == 0
    kern = functools.partial(_nsa_prompt_kernel, ncp=ncp, n_qt=n_qt)
    cols = HPG * LANE
    last = n_qt - 1

    def sel_tile(j):
        return jnp.minimum(j, last)

    def att_tile(j):
        return jnp.maximum(j - 1, 0)

    return pl.pallas_call(
        kern,
        grid=(n_seq, N_KV, n_qt + 1),
        in_specs=[pl.BlockSpec((1, HPG, LANE, LANE), lambda b, g, j: (b, g, sel_tile(j), 0)),
                  pl.BlockSpec((1, 1, 16, LANE), lambda b, g, j: (b, g, 0, att_tile(j))),
                  pl.BlockSpec((1, 1, ncp, LANE), lambda b, g, j: (b, g, 0, 0)),
                  pl.BlockSpec((1, 1, VROWS, ncp), lambda b, g, j: (b, g, 0, 0)),
                  pl.BlockSpec((1, 1, n_qt, LANE, LANE), lambda b, g, j: (b, g, 0, 0, 0)),
                  pl.BlockSpec((1, n_qt, VROWS, LANE), lambda b, g, j: (b, 0, g, 0)),
                  pl.BlockSpec((1, 1, n_qt, LANE, LANE), lambda b, g, j: (b, N_KV + g, 0, 0, 0)),
                  pl.BlockSpec((1, n_qt, VROWS, LANE), lambda b, g, j: (b, 0, N_KV + g, 0)),
                  pl.BlockSpec((LANE, ncp), lambda b, g, j: (0, 0)),
                  pl.BlockSpec((n_qt, LANE, LANE), lambda b, g, j: (0, 0, 0)),
                  pl.BlockSpec((1, 1, cols), lambda b, g, j: (g, 0, 0))],
        out_specs=pl.BlockSpec((1, HPG * HEAD_DIM, LANE), lambda b, g, j: (b, g, att_tile(j))),
        out_shape=jax.ShapeDtypeStruct((n_seq, N_HEADS * HEAD_DIM, seq_len), F32),
        scratch_shapes=[pltpu.VMEM((2, cols, 2 * LANE), BF16),
                        pltpu.VMEM((2, HEAD_DIM, cols), F32),
                        pltpu.VMEM((1, cols), F32),
                        pltpu.VMEM((VROWS, cols), F32),
                        pltpu.SMEM((2 * (LANE // 32),), jnp.int32)],
        compiler_params=_params(3),
    )(q, gates_t, kc, vct, k_ext, v_t, k_ext, v_t, cov_t, onehot, sl_rows)


def _nsa_sample_kernel(tbl_ref, *refs, n_pages, n_e):
    del tbl_ref
    shared = refs[n_e * n_pages:]
    _interleave(*[_sample_sequence(e, refs[e * n_pages:(e + 1) * n_pages], *shared) for e in range(n_e)])


def _sample_sequence(e, pages, q_ref, g_ref, kc_ref, vct_ref, win_ref, new_ref, cov_ref, exp_ref, slope_ref,
                     o_ref, s_scr):
    n_pages = len(pages)
    gd = N_KV * HEAD_DIM
    past = n_pages * PAGE
    q_pos = float(past)
    qb = q_ref[e].astype(BF16)
    slope = slope_ref[...]
    row_grp = lax.broadcasted_iota(jnp.int32, (N_HEADS, LANE), 0) // HPG
    lane_f = lax.broadcasted_iota(jnp.int32, (N_HEADS, LANE), 1).astype(F32)
    new = new_ref[e]

    def own_rows(g_sel, pick):
        out = pick(0)
        for grp in range(1, N_KV):
            out = jnp.where(g_sel == grp, pick(grp), out)
        return out

    q_f = q_ref[e]
    row_grp_o = lax.broadcasted_iota(jnp.int32, (N_HEADS, HEAD_DIM), 0) // HPG
    q_pair = [jnp.concatenate([jnp.where(row_grp_o == 2 * pr, q_f, 0.0), jnp.where(row_grp_o == 2 * pr + 1, q_f, 0.0)],
                              axis=1).astype(BF16) for pr in range(N_KV // 2)]
    first_row2 = lax.broadcasted_iota(jnp.int32, (LANE, 2 * HEAD_DIM), 0) == 0

    def pair_scores(tile):
        return _dot_nt(q_pair[0], tile(0)) + _dot_nt(q_pair[1], tile(1))

    def own_halves(o_pairs):
        out = o_pairs[0][:, 0:HEAD_DIM]
        for grp in range(1, N_KV):
            half = o_pairs[grp // 2][:, (grp % 2) * HEAD_DIM:(grp % 2 + 1) * HEAD_DIM]
            out = jnp.where(row_grp_o == grp, half, out)
        return out

    def new_slab(slot, pr):
        col = slot * gd + pr * 2 * HEAD_DIM
        return jnp.where(first_row2, new[:, col:col + 2 * HEAD_DIM], 0.0).astype(BF16)

    def softmax_rows(s, valid):
        s = jnp.where(valid, s, NEG)
        m = jnp.max(s, axis=1, keepdims=True)
        p = jnp.where(valid, jnp.exp(s - m), 0.0)
        l = jnp.sum(p, axis=1, keepdims=True)
        return p * jnp.where(l > 0.0, 1.0 / l, 0.0)

    d_c = q_pos - (CMP_STRIDE * lane_f + (CMP_LEN - 1))
    s_c = own_rows(row_grp, lambda grp: _dot_nt(qb, kc_ref[e, grp][:, 0:HEAD_DIM])) - slope * d_c
    yield
    p_c = softmax_rows(s_c, d_c >= 0.0)
    p_cb = p_c.astype(BF16)
    o_c = own_rows(row_grp_o, lambda grp: _dot_nt(p_cb, vct_ref[e, grp, 0:HEAD_DIM, :]))
    yield

    p_grp = own_rows(row_grp, lambda grp: jnp.broadcast_to(
        jnp.sum(p_c[grp * HPG:(grp + 1) * HPG], axis=0, keepdims=True), (N_HEADS, LANE)))
    p_hi, p_lo = _split_bf16(jnp.concatenate([p_grp, jnp.zeros((LANE - N_HEADS, LANE), F32)], axis=0))
    cov_t = cov_ref[...]
    imp_t = _dot_nt(cov_t, p_hi) + _dot_nt(cov_t, p_lo)
    blk = lax.broadcasted_iota(jnp.int32, (LANE, LANE), 0)
    q_blk = past // SEL_BLOCK
    forced = (blk == 0) | (blk == q_blk) | (blk == q_blk - 1)
    valid = blk <= q_blk
    work = jnp.where(valid & jnp.logical_not(forced), imp_t, NEG)
    yield
    sel_t = yield from _select_blocks(work, blk.astype(F32), 0, N_SEL - 3)
    sel = jnp.where(valid & forced, 1.0, sel_t).T[0:N_HEADS]

    n_t = n_pages + 1
    for t in range(n_t):
        if t < n_pages:
            tile = lambda pr, t=t: pages[t][0, :, pr * LANE:(pr + 1) * LANE].astype(BF16)
        else:
            tile = lambda pr: new_slab(2, pr)
        s_scr[e, :, t * LANE:(t + 1) * LANE] = pair_scores(tile)
        if t % 4 == 3:
            yield
    width = n_t * LANE
    kpos = lax.broadcasted_iota(jnp.int32, (N_HEADS, width), 1).astype(F32)
    d_s = q_pos - kpos
    chosen = _dot(sel.astype(BF16), exp_ref[...])
    slope_w = jnp.concatenate([slope] * n_t, axis=1)
    p_s = softmax_rows(s_scr[e] - slope_w * d_s, (chosen > 0.5) & (d_s >= 0.0)).astype(BF16)
    yield
    o_sp = [jnp.zeros((N_HEADS, 2 * HEAD_DIM), F32) for _ in range(N_KV // 2)]
    for t in range(n_t):
        pt = p_s[:, t * LANE:(t + 1) * LANE]
        if t < n_pages:
            vtile = lambda pr, t=t: pages[t][0, :, gd + pr * LANE: gd + (pr + 1) * LANE].astype(BF16)
        else:
            vtile = lambda pr: new_slab(3, pr)
        o_sp = [o_sp[pr] + _dot(pt, vtile(pr)) for pr in range(N_KV // 2)]
        if t % 4 == 3:
            yield
    o_s = own_halves(o_sp)

    buf = win_ref.shape[1]
    n_w = buf // LANE + 1
    w_parts = []
    for t in range(n_w):
        if t < n_w - 1:
            tile = lambda pr, t=t: win_ref[e, t * LANE:(t + 1) * LANE, pr * LANE:(pr + 1) * LANE].astype(BF16)
        else:
            tile = lambda pr: new_slab(4, pr)
        w_parts.append(pair_scores(tile))
    s_w = jnp.concatenate(w_parts, axis=1)
    yield
    wpos = lax.broadcasted_iota(jnp.int32, (N_HEADS, n_w * LANE), 1).astype(F32) + float(past - buf)
    d_w = q_pos - wpos
    slope_ww = jnp.concatenate([slope] * n_w, axis=1)
    p_w = softmax_rows(s_w - slope_ww * d_w, (d_w >= 0.0) & (d_w <= float(WINDOW))).astype(BF16)
    yield
    o_wp = [jnp.zeros((N_HEADS, 2 * HEAD_DIM), F32) for _ in range(N_KV // 2)]
    for t in range(n_w):
        pt = p_w[:, t * LANE:(t + 1) * LANE]
        if t < n_w - 1:
            vtile = lambda pr, t=t: win_ref[e, t * LANE:(t + 1) * LANE,
                                            gd + pr * LANE: gd + (pr + 1) * LANE].astype(BF16)
        else:
            vtile = lambda pr: new_slab(5, pr)
        o_wp = [o_wp[pr] + _dot(pt, vtile(pr)) for pr in range(N_KV // 2)]
    o_w = own_halves(o_wp)

    gt = g_ref[e]
    o_ref[e] = gt[:, 0:1] * o_c + gt[:, 1:2] * o_s + gt[:, 2:3] * o_w


def _nsa_sample(q, gates, kc, vct, cache_pages, page_table, win, kv_new, cov, expand, slope16):
    n_seq, n_pages = page_table.shape
    gd = N_KV * HEAD_DIM
    buf = win.shape[1]
    n_e = SAMPLE_SEQS if n_seq % SAMPLE_SEQS == 0 else 1
    kern = functools.partial(_nsa_sample_kernel, n_pages=n_pages, n_e=n_e)

    def page_spec(e, p):
        return pl.BlockSpec((1, PAGE, 2 * gd), lambda b, tbl, e=e, p=p: (tbl[b * n_e + e, p], 0, 1))

    def const_spec(shape):
        nd = len(shape)
        return pl.BlockSpec(shape, lambda b, tbl: (0,) * nd)

    grid_spec = pltpu.PrefetchScalarGridSpec(
        num_scalar_prefetch=1,
        grid=(n_seq // n_e,),
        in_specs=[page_spec(e, p) for e in range(n_e) for p in range(n_pages)] + [
            pl.BlockSpec((n_e, N_HEADS, HEAD_DIM), lambda b, tbl: (b, 0, 0)),
            pl.BlockSpec((n_e, N_HEADS, 3), lambda b, tbl: (b, 0, 0)),
            pl.BlockSpec((n_e, N_KV, LANE, LANE), lambda b, tbl: (b, 0, 0, 0)),
            pl.BlockSpec((n_e, N_KV, VROWS, LANE), lambda b, tbl: (b, 0, 0, 0)),
            pl.BlockSpec((n_e, buf, 2 * gd), lambda b, tbl: (b, 0, 0)),
            pl.BlockSpec((n_e, 1, N_KV_PROJ * gd), lambda b, tbl: (b, 0, 0)),
            const_spec(cov.shape), const_spec(expand.shape), const_spec(slope16.shape)],
        out_specs=pl.BlockSpec((n_e, N_HEADS, HEAD_DIM), lambda b, tbl: (b, 0, 0)),
        scratch_shapes=[pltpu.VMEM((n_e, N_HEADS, (n_pages + 1) * LANE), F32)])
    return pl.pallas_call(
        kern,
        grid_spec=grid_spec,
        out_shape=jax.ShapeDtypeStruct((n_seq, N_HEADS, HEAD_DIM), F32),
        compiler_params=_params(1),
    )(page_table, *([cache_pages] * (n_e * n_pages)), q, gates, kc, vct, win, kv_new, cov, expand, slope16)


def _oproj_prompt_kernel(ot_ref, x_ref, gt_ref, wo_ref, o_ref):
    o = ot_ref[0].T.astype(BF16)
    o_ref[...] = x_ref[...] + gt_ref[0] * _dot(o, wo_ref[...])


def _oproj_sample_kernel(a_ref, x_ref, gt_ref, wo_ref, o_ref):
    o_ref[...] = x_ref[...] + gt_ref[0] * _dot(a_ref[...].astype(BF16), wo_ref[...])


def _out_proj(tok, attn, x, gate, wo):
    tm, nt = tok.tiles(512)
    if tok.per_row:
        kern, a_spec = _oproj_sample_kernel, tok.x_spec(tm)
    else:
        per_seq = tok.seq_len // tm
        kern = _oproj_prompt_kernel
        a_spec = pl.BlockSpec((1, D_MODEL, tm), lambda i: (i // per_seq, 0, i % per_seq))
    return pl.pallas_call(
        kern, grid=(nt,),
        in_specs=[a_spec, tok.x_spec(tm), tok.mod_spec(tm), _full_spec((D_MODEL, D_MODEL))],
        out_specs=tok.x_spec(tm),
        out_shape=jax.ShapeDtypeStruct((tok.n_tok, D_MODEL), F32),
        compiler_params=_params(1))(attn, x, tok.mod_arr(gate), wo)


def _final_kernel(x_ref, sh_ref, sc_ref, g_ref, o_ref):
    o_ref[...] = _ada_norm(x_ref[...], g_ref[...], sh_ref[0], sc_ref[0])


def _final(tok, x, shift, scale, g):
    tm, nt = tok.tiles(512)
    return pl.pallas_call(
        _final_kernel, grid=(nt,),
        in_specs=[tok.x_spec(tm), tok.mod_spec(tm), tok.mod_spec(tm), _full_spec((1, D_MODEL))],
        out_specs=tok.x_spec(tm),
        out_shape=jax.ShapeDtypeStruct((tok.n_tok, D_MODEL), F32),
        compiler_params=_params(1))(x, tok.mod_arr(shift), tok.mod_arr(scale), g.reshape(1, D_MODEL))


def _alibi_slopes():
    h = jnp.arange(1, N_HEADS + 1, dtype=F32)
    return jnp.exp2(-8.0 * h / N_HEADS)


def _cover(n_cmp, n_sel):
    c_start = jnp.arange(n_cmp)[:, None] * CMP_STRIDE
    s_start = jnp.arange(n_sel)[None, :] * SEL_BLOCK
    return ((c_start < s_start + SEL_BLOCK) & (c_start + CMP_LEN > s_start)).astype(BF16)


def _trunk(tok, x, mods, kv_mod, f_mod, wts, ctx):
    depth = wts['norm_g'].shape[0]
    n_a = depth // 2
    v_rows = []
    kv = None
    attn_ctx = None
    for l in range(depth):
        m = mods[l]
        if l == n_a:
            kv_out = _kv_proj(tok, x, kv_mod[:, 0], kv_mod[:, 1], wts['kv_norm_g'], wts['kv_w'], wts['kv_wvt'],
                              wts['alibi_k'])
            kv, attn_ctx = ctx['prepare'](kv_out)
        x = _ffn(tok, x, m[:, 0], m[:, 1], m[:, 2], wts['norm_g'][l, 0],
                 wts['ffn_w_gate'][l, 0], wts['ffn_w_up'][l, 0], wts['ffn_w_down'][l, 0])
        if l < n_a:
            x, v = _gmlp(tok, x, m[:, 3], m[:, 4], m[:, 5], wts['norm_g'][l, 1], wts['gmlp_w_uv'][l],
                         wts['gmlp_ln_g'][l], wts['gmlp_ln_b'][l], wts['gmlp_w_sp'][l], wts['gmlp_b_sp'][l],
                         wts['gmlp_w_out'][l])
            v_rows.append(v)
        else:
            j = l - n_a
            q, gates = _qg_proj(tok, x, m[:, 3], m[:, 4], wts['norm_g'][l, 1], wts['nsa_wq'][j],
                                wts['nsa_wg_t'][j], wts['nsa_wg'][j], wts['alibi_q'])
            attn = ctx['attend'](q, gates, attn_ctx)
            x = _out_proj(tok, attn, x, m[:, 5], wts['nsa_w_o'][j])
        x = _ffn(tok, x, m[:, 6], m[:, 7], m[:, 8], wts['norm_g'][l, 2],
                 wts['ffn_w_gate'][l, 1], wts['ffn_w_up'][l, 1], wts['ffn_w_down'][l, 1])
    y = _final(tok, x, f_mod[:, 0], f_mod[:, 1], wts['final_g'])
    return y, kv, v_rows


def kernel(x_prompt, x_sample, cache_kv, state_win_kv, page_table, c_prompt, c_sample, ada_w, ada_b, norm_g, ffn_w_gate, ffn_w_up, ffn_w_down, gmlp_w_uv, gmlp_ln_g, gmlp_ln_b, gmlp_w_sp, gmlp_b_sp, gmlp_w_out, nsa_w_qg, nsa_w_o, kv_norm_g, kv_ada_w, kv_ada_b, kv_w, cmp_w1, cmp_w2, cmp_pe, final_g, final_ada_w, final_ada_b):
    n_p, seq, _ = x_prompt.shape
    n_s, dec_seq, _ = x_sample.shape
    assert dec_seq == 1 and seq % (PAGES_PER_SEG * PAGE) == 0
    depth = ada_w.shape[0]
    n_b = nsa_w_qg.shape[0]
    gd = N_KV * HEAD_DIM
    n_pages = page_table.shape[1]
    assert n_pages == PAGES_PER_SEG
    past = n_pages * PAGE

    nq = N_HEADS * HEAD_DIM
    wg_cols = nsa_w_qg[:, :, nq:]
    wg_pad = jnp.pad(wg_cols.reshape(n_b, D_MODEL, N_KV, HPG * 3), ((0, 0), (0, 0), (0, 0), (0, 16 - HPG * 3)))
    wts = dict(
        norm_g=norm_g, kv_norm_g=kv_norm_g, final_g=final_g,
        ffn_w_gate=ffn_w_gate.astype(BF16), ffn_w_up=ffn_w_up.astype(BF16), ffn_w_down=ffn_w_down.astype(BF16),
        gmlp_w_uv=gmlp_w_uv.astype(BF16), gmlp_ln_g=gmlp_ln_g, gmlp_ln_b=gmlp_ln_b,
        gmlp_w_sp=gmlp_w_sp, gmlp_b_sp=gmlp_b_sp, gmlp_w_out=gmlp_w_out.astype(BF16),
        nsa_wq=nsa_w_qg[:, :, :nq].astype(BF16),
        nsa_wg_t=jnp.swapaxes(wg_pad.reshape(n_b, D_MODEL, N_KV * 16), 1, 2).astype(BF16),
        nsa_wg=jnp.pad(wg_cols, ((0, 0), (0, 0), (0, LANE - N_HEADS * 3))).astype(BF16),
        nsa_w_o=nsa_w_o.astype(BF16),
        kv_w=kv_w.astype(BF16),
        kv_wvt=jnp.concatenate([kv_w[:, 3 * gd:4 * gd], kv_w[:, 5 * gd:6 * gd]], axis=1).T.astype(BF16),
    )
    w1 = cmp_w1.astype(BF16)
    w2 = cmp_w2.astype(BF16)
    w2t = jnp.swapaxes(cmp_w2, 1, 2).astype(BF16)
    pe = jnp.broadcast_to(cmp_pe.astype(BF16).reshape(2, 1, CMP_LEN * HEAD_DIM), (2, 16, CMP_LEN * HEAD_DIM))
    slopes = _alibi_slopes()
    sl2 = slopes * LOG2E
    off_f = jnp.arange(LANE, dtype=F32)

    def split3(x):
        a = x.astype(BF16)
        b = (x - a.astype(F32)).astype(BF16)
        c = (x - a.astype(F32) - b.astype(F32)).astype(BF16)
        return [a, b, c]

    q_cols = split3(-sl2[:, None] * off_f[None, :]) + \
        [jnp.broadcast_to(c[:, None], (N_HEADS, LANE)) for c in split3(sl2)] + \
        [jnp.broadcast_to(c[:, None], (N_HEADS, LANE)) for c in split3(sl2 * CMP_STRIDE)]
    wts['alibi_q'] = jnp.pad(jnp.stack(q_cols, axis=-1), ((0, 0), (0, 0), (0, HEAD_DIM - 9)))
    ones, zeros, offs = jnp.ones((LANE,), BF16), jnp.zeros((LANE,), BF16), off_f.astype(BF16)
    wts['alibi_k'] = jnp.pad(jnp.stack([ones] * 3 + [offs] * 3 + [zeros] * 3, axis=-1), ((0, 0), (0, HEAD_DIM - 9)))
    alibi_kc = jnp.pad(jnp.stack([ones] * 3 + [zeros] * 3 + [offs] * 3, axis=-1), ((0, 0), (0, HEAD_DIM - 9)))

    n_c = n_p + n_s
    c_all = jnp.pad(jnp.concatenate([c_prompt, c_sample], axis=0), ((0, (-n_c) % 8), (0, 0)))
    mod_all = _mod_linear(c_all, ada_w, ada_b)
    kv_mod_all = _mod_linear(c_all, kv_ada_w[None], kv_ada_b[None])[0]
    f_mod_all = _mod_linear(c_all, final_ada_w[None], final_ada_b[None])[0]

    def rows(a, lo, hi, k):
        return a[..., lo:hi, :].reshape(a.shape[:-2] + (hi - lo, k, D_MODEL))

    tok_p = _Tok(n_p, seq, per_row=False)
    n_seg = seq // (PAGES_PER_SEG * PAGE)
    n_qt = seq // LANE
    ncp = n_seg * PAGES_PER_SEG * 8
    n_sel_p = seq // SEL_BLOCK
    assert n_sel_p <= LANE
    cov_t = jnp.pad(_cover(ncp, n_sel_p).T, ((0, LANE - n_sel_p), (0, 0)))
    sl_rows = jnp.repeat(sl2.reshape(N_KV, 1, HPG), LANE, axis=2)
    onehot = (jnp.arange(LANE)[None, None, :] ==
              (2 * jnp.arange(n_qt)[:, None, None] + jnp.arange(LANE)[None, :, None] // SEL_BLOCK)).astype(BF16)

    def prepare_p(kv_out):
        kv, k_nat, v_t = kv_out
        pages_per_seq = seq // PAGE
        base = jnp.arange(n_p * n_seg, dtype=jnp.int32)[:, None] * PAGES_PER_SEG
        table = jnp.minimum(base + jnp.arange(PAGES_PER_SEG + 1, dtype=jnp.int32)[None, :],
                            n_p * pages_per_seq - 1)
        kc, vct = _compress(kv.reshape(n_p * pages_per_seq, PAGE, N_KV_PROJ * gd), table, w1, w2, w2t, pe,
                            alibi_kc, n_p, n_seg)
        return kv, (kc, vct, k_nat, v_t)

    def attend_p(q, gates, c):
        kc, vct, k_nat, v_t = c
        return _nsa_prompt(q, gates, kc, vct, k_nat, v_t, cov_t, onehot, sl_rows, n_p, seq)

    y_p, kv_p, _ = _trunk(tok_p, x_prompt.reshape(n_p * seq, D_MODEL),
                          rows(mod_all, 0, n_p, 9), rows(kv_mod_all, 0, n_p, 2), rows(f_mod_all, 0, n_p, 2),
                          wts, dict(prepare=prepare_p, attend=attend_p))
    kv_p = kv_p.reshape(n_p, seq, N_KV_PROJ, N_KV, HEAD_DIM)
    kv_prompt = kv_p[:, :, :4]
    win_prompt = kv_p[:, seq - min(WINDOW, seq):, 4:6]

    tok_s = _Tok(n_s, 1, per_row=True)
    cache_pages = cache_kv.reshape(cache_kv.shape[0], PAGE, 4 * gd)
    buf = state_win_kv.shape[1]
    win_flat = state_win_kv.reshape(n_s, buf, 2 * gd)
    n_sel_s = (past + 1 + SEL_BLOCK - 1) // SEL_BLOCK
    cov_s = jnp.pad(_cover(LANE, n_sel_s).T, ((0, LANE - n_sel_s), (0, 0)))
    n_keys = (n_pages + 1) * LANE
    expand_s = (jnp.arange(LANE)[:, None] == (jnp.arange(n_keys)[None, :] // SEL_BLOCK)).astype(BF16)
    slope16 = jnp.broadcast_to(slopes[:, None], (N_HEADS, LANE))

    def prepare_s(kv):
        table = jnp.concatenate([page_table, page_table[:, -1:]], axis=1)
        kc, vct = _compress(cache_pages, table, w1, w2, w2t, pe, alibi_kc, n_s, 1)
        return kv, (kc, vct, kv)

    def attend_s(q, gates, c):
        kc, vct, kv = c
        o = _nsa_sample(q.reshape(n_s, N_HEADS, HEAD_DIM), gates[:, :N_HEADS * 3].reshape(n_s, N_HEADS, 3),
                        kc, vct, cache_pages, page_table, win_flat, kv.reshape(n_s, 1, N_KV_PROJ * gd),
                        cov_s, expand_s, slope16)
        return o.reshape(n_s, N_HEADS * HEAD_DIM)

    y_s, kv_s, v_rows = _trunk(tok_s, x_sample.reshape(n_s, D_MODEL),
                               rows(mod_all, n_p, n_c, 9), rows(kv_mod_all, n_p, n_c, 2),
                               rows(f_mod_all, n_p, n_c, 2), wts, dict(prepare=prepare_s, attend=attend_s))
    kv_s = kv_s.reshape(n_s, 1, N_KV_PROJ, N_KV, HEAD_DIM)
    kv_sample = kv_s[:, :, :4]
    win_sample = jnp.concatenate([state_win_kv, kv_s[:, :, 4:6]], axis=1)[:, 1:]
    gmlp_v_sample = jnp.stack([v.reshape(n_s, 1, D_V) for v in v_rows])

    return (y_p.reshape(n_p, seq, D_MODEL), y_s.reshape(n_s, 1, D_MODEL), kv_prompt, kv_sample,
            win_prompt, win_sample, gmlp_v_sample)
```

```python
import functools

import jax
import jax.numpy as jnp
from jax import lax
from jax.experimental import pallas as pl
from jax.experimental.pallas import tpu as pltpu

F32 = jnp.float32
BF16 = jnp.bfloat16

D_MODEL = 1024
D_FF = 2816
D_V = 3072
CHUNK = 128
N_GROUPS_A = 8
CG = D_V // N_GROUPS_A
N_HEADS = 16
HEAD_DIM = 64
N_KV = 4
HPG = N_HEADS // N_KV
CMP_LEN = 32
CMP_STRIDE = 16
CMP_HID = 256
SEL_BLOCK = 64
N_SEL = 16
WINDOW = 512
PAGE = 128
N_KV_PROJ = 6
EPS = 1e-6
NEG = -1e30
M_FLOOR = -1e29
PICKED = -3e38
FORCE_BONUS = 1e4
LOG2E = 1.4426950408889634
MASK_OFF = 1e30

LANE = 128
VMEM_LIMIT = 56 * 1024 * 1024
FF_CHUNK = 256
UV_CHUNK = 512
VROWS = HEAD_DIM + 16
N_NEAR = 8
SAMPLE_SEQS = 4
PAGES_PER_SEG = 16
XROWS = PAGES_PER_SEG * 8 + 8


def _dot(a, b):
    return jnp.dot(a, b, preferred_element_type=F32)


def _dot_nt(a, b):
    return lax.dot_general(a, b, (((1,), (1,)), ((), ())), preferred_element_type=F32)


def _split_bf16(x):
    hi = x.astype(BF16)
    lo = (x - hi.astype(F32)).astype(BF16)
    return hi, lo


def _params(n_grid):
    return pltpu.CompilerParams(dimension_semantics=("arbitrary",) * n_grid,
                                vmem_limit_bytes=VMEM_LIMIT)


def _full_spec(shape):
    nd = len(shape)
    return pl.BlockSpec(shape, lambda *_: (0,) * nd, pipeline_mode=pl.Buffered(1))


def _ada_norm(x, g, shift, scale):
    ms = jnp.mean(x * x, axis=-1, keepdims=True)
    h = x * lax.rsqrt(ms + EPS) * g
    return h * (1.0 + scale) + shift


def _mod_kernel(c_ref, w_ref, b_ref, o_ref):
    c = c_ref[...]
    a = c * jax.nn.sigmoid(c)
    w = w_ref[0]
    a_hi, a_lo = _split_bf16(a)
    w_hi, w_lo = _split_bf16(w)
    o_ref[0] = _dot(a_hi, w_hi) + _dot(a_hi, w_lo) + _dot(a_lo, w_hi) + b_ref[0]


def _mod_linear(c, w, b):
    n_l, d, n = w.shape
    m = c.shape[0]
    tn = 1024
    return pl.pallas_call(
        _mod_kernel,
        grid=(n_l, n // tn),
        in_specs=[pl.BlockSpec((m, d), lambda l, j: (0, 0)),
                  pl.BlockSpec((1, d, tn), lambda l, j: (l, 0, j)),
                  pl.BlockSpec((1, 1, tn), lambda l, j: (l, 0, j))],
        out_specs=pl.BlockSpec((1, m, tn), lambda l, j: (l, 0, j)),
        out_shape=jax.ShapeDtypeStruct((n_l, m, n), F32),
        compiler_params=_params(2),
    )(c, w, b.reshape(n_l, 1, n))


class _Tok:
    def __init__(self, n_seq, seq_len, per_row):
        self.n_seq, self.seq_len = n_seq, seq_len
        self.n_tok = n_seq * seq_len
        self.per_row = per_row

    def tiles(self, tm):
        if self.per_row:
            return self.n_tok, 1
        assert self.seq_len % tm == 0
        return tm, self.n_tok // tm

    def x_spec(self, tm, width=D_MODEL):
        return pl.BlockSpec((tm, width), lambda i: (i, 0))

    def mod_spec(self, tm):
        if self.per_row:
            return pl.BlockSpec((1, tm, D_MODEL), lambda i: (0, 0, 0))
        per_seq = self.seq_len // tm
        return pl.BlockSpec((1, 1, D_MODEL), lambda i: (i // per_seq, 0, 0))

    def mod_arr(self, m):
        if self.per_row:
            return m.reshape(1, self.n_tok, D_MODEL)
        return m.reshape(self.n_seq, 1, D_MODEL)


def _ffn_kernel(x_ref, sh_ref, sc_ref, gt_ref, g_ref, wg_ref, wu_ref, wd_ref, o_ref, a_scr):
    x = x_ref[...]
    hb = _ada_norm(x, g_ref[...], sh_ref[0], sc_ref[0]).astype(BF16)
    for c in range(D_FF // FF_CHUNK):
        cs = slice(c * FF_CHUNK, (c + 1) * FF_CHUNK)
        gate = _dot(hb, wg_ref[:, cs])
        up = _dot(hb, wu_ref[:, cs])
        a_scr[:, cs] = (gate * jax.nn.sigmoid(gate) * up).astype(BF16)
    y = _dot(a_scr[...], wd_ref[...])
    o_ref[...] = x + (0.5 * gt_ref[0]) * y


def _ffn(tok, x, shift, scale, gate, g, wg, wu, wd):
    tm, nt = tok.tiles(512)
    return pl.pallas_call(
        _ffn_kernel,
        grid=(nt,),
        in_specs=[tok.x_spec(tm), tok.mod_spec(tm), tok.mod_spec(tm), tok.mod_spec(tm),
                  _full_spec((1, D_MODEL)), _full_spec((D_MODEL, D_FF)),
                  _full_spec((D_MODEL, D_FF)), _full_spec((D_FF, D_MODEL))],
        out_specs=tok.x_spec(tm),
        out_shape=jax.ShapeDtypeStruct((tok.n_tok, D_MODEL), F32),
        scratch_shapes=[pltpu.VMEM((tm, D_FF), BF16)],
        compiler_params=_params(1),
    )(x, tok.mod_arr(shift), tok.mod_arr(scale), tok.mod_arr(gate), g.reshape(1, D_MODEL), wg, wu, wd)


def _gmlp_uv(x_ref, sh_ref, sc_ref, g_ref, wuv_ref, lng_ref, lnb_ref, u_scr, v_scr):
    hb = _ada_norm(x_ref[...], g_ref[...], sh_ref[0], sc_ref[0]).astype(BF16)
    for c in range(D_V // UV_CHUNK):
        cs = slice(c * UV_CHUNK, (c + 1) * UV_CHUNK)
        cv = slice(D_V + c * UV_CHUNK, D_V + (c + 1) * UV_CHUNK)
        u_scr[:, cs] = jax.nn.gelu(_dot(hb, wuv_ref[:, cs]))
        v_scr[:, cs] = jax.nn.gelu(_dot(hb, wuv_ref[:, cv]))
    v = v_scr[...]
    mu = jnp.mean(v, axis=-1, keepdims=True)
    vc = v - mu
    var = jnp.mean(vc * vc, axis=-1, keepdims=True)
    return vc * lax.rsqrt(var + EPS) * lng_ref[...] + lnb_ref[...]


def _gmlp_prompt_kernel(x_ref, sh_ref, sc_ref, gt_ref, g_ref, wuv_ref, lng_ref, lnb_ref,
                        wsp_ref, bsp_ref, wout_ref, o_ref, u_scr, v_scr, a_scr):
    v_scr[...] = _gmlp_uv(x_ref, sh_ref, sc_ref, g_ref, wuv_ref, lng_ref, lnb_ref, u_scr, v_scr)
    row = lax.broadcasted_iota(jnp.int32, (CHUNK, CHUNK), 0)
    col = lax.broadcasted_iota(jnp.int32, (CHUNK, CHUNK), 1)
    tm = x_ref.shape[0]
    for grp in range(N_GROUPS_A):
        w = jnp.where(row >= col, wsp_ref[grp], 0.0).astype(BF16)
        cs = slice(grp * CG, (grp + 1) * CG)
        for n in range(tm // CHUNK):
            rs = slice(n * CHUNK, (n + 1) * CHUNK)
            s = _dot(w, v_scr[rs, cs].astype(BF16)) + bsp_ref[:, cs]
            a_scr[rs, cs] = (u_scr[rs, cs] * s).astype(BF16)
    y = _dot(a_scr[...], wout_ref[...])
    o_ref[...] = x_ref[...] + gt_ref[0] * y


def _gmlp_sample_kernel(x_ref, sh_ref, sc_ref, gt_ref, g_ref, wuv_ref, lng_ref, lnb_ref,
                        wrow_ref, brow_ref, wout_ref, o_ref, vout_ref, u_scr, v_scr):
    vn = _gmlp_uv(x_ref, sh_ref, sc_ref, g_ref, wuv_ref, lng_ref, lnb_ref, u_scr, v_scr)
    vout_ref[...] = vn
    s = vn * wrow_ref[...] + brow_ref[...]
    y = _dot((u_scr[...] * s).astype(BF16), wout_ref[...])
    o_ref[...] = x_ref[...] + gt_ref[0] * y


def _gmlp(tok, x, shift, scale, gate, g, wuv, ln_g, ln_b, w_sp, b_sp, wout):
    tm, nt = tok.tiles(512)
    common = [tok.x_spec(tm), tok.mod_spec(tm), tok.mod_spec(tm), tok.mod_spec(tm),
              _full_spec((1, D_MODEL)), _full_spec((D_MODEL, 2 * D_V)),
              _full_spec((1, D_V)), _full_spec((1, D_V))]
    args = [x, tok.mod_arr(shift), tok.mod_arr(scale), tok.mod_arr(gate), g.reshape(1, D_MODEL), wuv,
            ln_g.reshape(1, D_V), ln_b.reshape(1, D_V)]
    x_shape = jax.ShapeDtypeStruct((tok.n_tok, D_MODEL), F32)
    if tok.per_row:
        wrow = jnp.repeat(w_sp[:, 0, 0], CG).reshape(1, D_V)
        brow = jnp.repeat(b_sp[:, 0], CG).reshape(1, D_V)
        return pl.pallas_call(
            _gmlp_sample_kernel,
            grid=(nt,),
            in_specs=common + [_full_spec((1, D_V)), _full_spec((1, D_V)), _full_spec((D_V, D_MODEL))],
            out_specs=[tok.x_spec(tm), tok.x_spec(tm, D_V)],
            out_shape=[x_shape, jax.ShapeDtypeStruct((tok.n_tok, D_V), F32)],
            scratch_shapes=[pltpu.VMEM((tm, D_V), F32), pltpu.VMEM((tm, D_V), F32)],
            compiler_params=_params(1),
        )(*args, wrow, brow, wout)
    bias = jnp.repeat(b_sp.T, CG, axis=1)
    out = pl.pallas_call(
        _gmlp_prompt_kernel,
        grid=(nt,),
        in_specs=common + [_full_spec((N_GROUPS_A, CHUNK, CHUNK)), _full_spec((CHUNK, D_V)),
                           _full_spec((D_V, D_MODEL))],
        out_specs=tok.x_spec(tm),
        out_shape=x_shape,
        scratch_shapes=[pltpu.VMEM((tm, D_V), F32), pltpu.VMEM((tm, D_V), F32),
                        pltpu.VMEM((tm, D_V), BF16)],
        compiler_params=_params(1),
    )(*args, w_sp, bias, wout)
    return out, None


def _kv_prompt_kernel(x_ref, sh_ref, sc_ref, g_ref, w_ref, wvt_ref, ak_ref, kv_ref, k_ref, vt_ref):
    hb = _ada_norm(x_ref[...], g_ref[...], sh_ref[0], sc_ref[0]).astype(BF16)
    kv = _dot(hb, w_ref[...])
    kv_ref[...] = kv
    vt = _dot_nt(wvt_ref[...], hb).astype(BF16)
    tm = x_ref.shape[0]
    gd = N_KV * HEAD_DIM
    ones_rows = jnp.where(lax.broadcasted_iota(jnp.int32, (VROWS - HEAD_DIM, LANE), 0) == 0, 1.0, 0.0).astype(BF16)
    for c in range(tm // LANE):
        rs = slice(c * LANE, (c + 1) * LANE)
        for hd in range(2 * N_KV):
            vt_ref[0, c, hd * VROWS:hd * VROWS + HEAD_DIM, :] = vt[hd * HEAD_DIM:(hd + 1) * HEAD_DIM, rs]
            vt_ref[0, c, hd * VROWS + HEAD_DIM:(hd + 1) * VROWS, :] = ones_rows
        for j, slot in enumerate((2, 4)):
            for grp in range(N_KV):
                col = slot * gd + grp * HEAD_DIM
                k_ref[0, j * N_KV + grp, c, :, 0:HEAD_DIM] = kv[rs, col:col + HEAD_DIM].astype(BF16)
                k_ref[0, j * N_KV + grp, c, :, HEAD_DIM:LANE] = ak_ref[...]


def _kv_sample_kernel(x_ref, sh_ref, sc_ref, g_ref, w_ref, kv_ref):
    hb = _ada_norm(x_ref[...], g_ref[...], sh_ref[0], sc_ref[0]).astype(BF16)
    kv_ref[...] = _dot(hb, w_ref[...])


def _kv_proj(tok, x, shift, scale, g, w, wvt, alibi_k):
    tm, nt = tok.tiles(512)
    n_kv = N_KV_PROJ * N_KV * HEAD_DIM
    in_specs = [tok.x_spec(tm), tok.mod_spec(tm), tok.mod_spec(tm), _full_spec((1, D_MODEL)),
                _full_spec((D_MODEL, n_kv))]
    args = [x, tok.mod_arr(shift), tok.mod_arr(scale), g.reshape(1, D_MODEL), w]
    kv_shape = jax.ShapeDtypeStruct((tok.n_tok, n_kv), F32)
    if tok.per_row:
        return pl.pallas_call(
            _kv_sample_kernel, grid=(nt,), in_specs=in_specs, out_specs=tok.x_spec(tm, n_kv),
            out_shape=kv_shape, compiler_params=_params(1))(*args)
    per_seq = tok.seq_len // tm
    n_qt = tok.seq_len // LANE
    sub = tm // LANE
    return pl.pallas_call(
        _kv_prompt_kernel,
        grid=(nt,),
        in_specs=in_specs + [_full_spec((2 * N_KV * HEAD_DIM, D_MODEL)), _full_spec((LANE, HEAD_DIM))],
        out_specs=[tok.x_spec(tm, n_kv),
                   pl.BlockSpec((1, 2 * N_KV, sub, LANE, LANE),
                                lambda i: (i // per_seq, 0, i % per_seq, 0, 0)),
                   pl.BlockSpec((1, sub, 2 * N_KV * VROWS, LANE),
                                lambda i: (i // per_seq, i % per_seq, 0, 0))],
        out_shape=[kv_shape,
                   jax.ShapeDtypeStruct((tok.n_seq, 2 * N_KV, n_qt, LANE, LANE), BF16),
                   jax.ShapeDtypeStruct((tok.n_seq, n_qt, 2 * N_KV * VROWS, LANE), BF16)],
        compiler_params=_params(1),
    )(*args, wvt, alibi_k)


def _compress_kernel(tbl_ref, *refs):
    pages = refs[:PAGES_PER_SEG + 1]
    perm_ref, w1_ref, w2_ref, w2t_ref, pe_ref, akc_ref, kc_ref, vct_ref, x_scr, hb_scr = refs[PAGES_PER_SEG + 1:]
    del tbl_ref
    perm = perm_ref[...]
    n_heads = 2 * N_KV
    half = CMP_STRIDE * HEAD_DIM
    for p in range(PAGES_PER_SEG + 1):
        xp = _dot(perm, pages[p][0].astype(BF16))
        for hd in range(n_heads):
            cs = slice(hd * HEAD_DIM, (hd + 1) * HEAD_DIM)
            x_scr[hd, p * 8:(p + 1) * 8, :] = jnp.concatenate(
                [xp[s * 8:(s + 1) * 8, cs] for s in range(CMP_STRIDE)], axis=1)
    n_blk = PAGES_PER_SEG * 8

    def one_slot(slot):
        x = x_scr[slot * N_KV:(slot + 1) * N_KV].reshape(N_KV * XROWS, half).astype(BF16)
        ha = _dot(x, w1_ref[slot, 0:half, :])
        yield
        hb_scr[slot] = _dot(x, w1_ref[slot, half:2 * half, :])
        pe_hid = _dot(pe_ref[slot], w1_ref[slot])[0:1]
        yield
        for grp in range(N_KV):
            hid = ha[grp * XROWS:grp * XROWS + n_blk] + hb_scr[slot, pl.ds(grp * XROWS + 1, n_blk), :] + pe_hid
            act = jax.nn.gelu(hid).astype(BF16)
            if slot == 0:
                kc_ref[0, grp, :, 0:HEAD_DIM] = _dot(act, w2_ref[0]).astype(BF16)
                kc_ref[0, grp, :, HEAD_DIM:LANE] = akc_ref[...]
            else:
                vct_ref[0, grp, 0:HEAD_DIM, :] = _dot_nt(w2t_ref[1], act).astype(BF16)
                vct_ref[0, grp, HEAD_DIM:VROWS, :] = jnp.where(
                    lax.broadcasted_iota(jnp.int32, (VROWS - HEAD_DIM, n_blk), 0) == 0, 1.0, 0.0).astype(BF16)
            yield

    _interleave(one_slot(0), one_slot(1))


def _compress(src, table, w1, w2, w2t, pe, alibi_kc, n_seq, n_seg):
    n_blk = PAGES_PER_SEG * 8
    gd2 = 2 * N_KV * HEAD_DIM
    dst = jnp.arange(PAGE)
    src_row = (dst % 8) * CMP_STRIDE + dst // 8
    perm = (jnp.arange(PAGE)[None, :] == src_row[:, None]).astype(BF16)

    def page_spec(p):
        return pl.BlockSpec((1, PAGE, gd2), lambda i, tbl, p=p: (tbl[i, p], 0, 0))

    grid_spec = pltpu.PrefetchScalarGridSpec(
        num_scalar_prefetch=1,
        grid=(n_seq * n_seg,),
        in_specs=[page_spec(p) for p in range(PAGES_PER_SEG + 1)] + [
            pl.BlockSpec((PAGE, PAGE), lambda i, tbl: (0, 0)),
            pl.BlockSpec(w1.shape, lambda i, tbl: (0, 0, 0)),
            pl.BlockSpec(w2.shape, lambda i, tbl: (0, 0, 0)),
            pl.BlockSpec(w2t.shape, lambda i, tbl: (0, 0, 0)),
            pl.BlockSpec(pe.shape, lambda i, tbl: (0, 0, 0)),
            pl.BlockSpec((LANE, HEAD_DIM), lambda i, tbl: (0, 0))],
        out_specs=[pl.BlockSpec((1, N_KV, n_blk, LANE), lambda i, tbl: (i // n_seg, 0, i % n_seg, 0)),
                   pl.BlockSpec((1, N_KV, VROWS, n_blk), lambda i, tbl: (i // n_seg, 0, 0, i % n_seg))],
        scratch_shapes=[pltpu.VMEM((2 * N_KV, XROWS, CMP_STRIDE * HEAD_DIM), F32),
                        pltpu.VMEM((2, N_KV * XROWS, CMP_HID), F32)])
    return pl.pallas_call(
        _compress_kernel,
        grid_spec=grid_spec,
        out_shape=[jax.ShapeDtypeStruct((n_seq, N_KV, n_seg * n_blk, LANE), BF16),
                   jax.ShapeDtypeStruct((n_seq, N_KV, VROWS, n_seg * n_blk), BF16)],
        compiler_params=_params(1),
    )(table, *([src] * (PAGES_PER_SEG + 1)), perm, w1, w2, w2t, pe, alibi_kc)


def _qg_prompt_kernel(x_ref, sh_ref, sc_ref, g_ref, wq_ref, wgt_ref, aq_ref, q_ref, gt_ref):
    hb = _ada_norm(x_ref[...], g_ref[...], sh_ref[0], sc_ref[0]).astype(BF16)
    q = _dot(hb, wq_ref[...]) * (HEAD_DIM ** -0.5 * LOG2E)
    tm = x_ref.shape[0]
    for h in range(N_HEADS):
        q_ref[0, h, :, 0:HEAD_DIM] = q[:, h * HEAD_DIM:(h + 1) * HEAD_DIM].astype(BF16)
        for c in range(tm // LANE):
            q_ref[0, h, c * LANE:(c + 1) * LANE, HEAD_DIM:LANE] = aq_ref[h]
    gates = jax.nn.sigmoid(_dot_nt(wgt_ref[...], hb))
    for grp in range(N_KV):
        gt_ref[0, grp] = gates[grp * 16:(grp + 1) * 16, :]


def _qg_sample_kernel(x_ref, sh_ref, sc_ref, g_ref, wq_ref, wg_ref, q_ref, gt_ref):
    hb = _ada_norm(x_ref[...], g_ref[...], sh_ref[0], sc_ref[0]).astype(BF16)
    q_ref[...] = _dot(hb, wq_ref[...]) * (HEAD_DIM ** -0.5)
    gt_ref[...] = jax.nn.sigmoid(_dot(hb, wg_ref[...]))


def _qg_proj(tok, x, shift, scale, g, wq, wg_t, wg_nat, alibi_q):
    tm, nt = tok.tiles(512)
    in_specs = [tok.x_spec(tm), tok.mod_spec(tm), tok.mod_spec(tm), _full_spec((1, D_MODEL)),
                _full_spec((D_MODEL, D_MODEL))]
    args = [x, tok.mod_arr(shift), tok.mod_arr(scale), g.reshape(1, D_MODEL), wq]
    if tok.per_row:
        return pl.pallas_call(
            _qg_sample_kernel, grid=(nt,),
            in_specs=in_specs + [_full_spec((D_MODEL, LANE))],
            out_specs=[tok.x_spec(tm), tok.x_spec(tm, LANE)],
            out_shape=[jax.ShapeDtypeStruct((tok.n_tok, D_MODEL), F32),
                       jax.ShapeDtypeStruct((tok.n_tok, LANE), F32)],
            compiler_params=_params(1))(*args, wg_nat)
    per_seq = tok.seq_len // tm
    return pl.pallas_call(
        _qg_prompt_kernel, grid=(nt,),
        in_specs=in_specs + [_full_spec((N_KV * 16, D_MODEL)), _full_spec((N_HEADS, LANE, HEAD_DIM))],
        out_specs=[pl.BlockSpec((1, N_HEADS, tm, LANE), lambda i: (i // per_seq, 0, i % per_seq, 0)),
                   pl.BlockSpec((1, N_KV, 16, tm), lambda i: (i // per_seq, 0, 0, i % per_seq))],
        out_shape=[jax.ShapeDtypeStruct((tok.n_seq, N_HEADS, tok.seq_len, LANE), BF16),
                   jax.ShapeDtypeStruct((tok.n_seq, N_KV, 16, tok.seq_len), F32)],
        compiler_params=_params(1))(*args, wg_t, alibi_q)


def _select_blocks(work, idx_f, axis, rounds):
    for _ in range(rounds):
        mx = jnp.max(work, axis=axis, keepdims=True)
        first = jnp.min(jnp.where(work == mx, idx_f, 1e9), axis=axis, keepdims=True)
        first = jnp.where(mx > NEG / 2, first, -1.0)
        work = jnp.where(idx_f == first, PICKED, work)
        yield
    return jnp.where(work == PICKED, 1.0, 0.0)


def _softmax_group(scores, offsets, m_old, exp_dtype=F32, values=None):
    m_new = m_old
    for s, off in zip(scores, offsets):
        m_new = jnp.maximum(m_new, jnp.max(s, axis=0, keepdims=True) - off)
        yield
    alpha = jnp.exp2(m_old - m_new)
    probs = []
    acc = None
    for k, (s, off) in enumerate(zip(scores, offsets)):
        probs.append(jnp.exp2((s - (m_new + off)).astype(exp_dtype)))
        if values is not None and (k % 2 == 1 or k == len(scores) - 1):
            n = 2 if k % 2 == 1 else 1
            part = _dot(jnp.concatenate(values[k + 1 - n:k + 1], axis=1), jnp.concatenate(probs[-n:], axis=0))
            acc = part if acc is None else acc + part
        yield
    return m_new, alpha, (probs if values is None else acc)


def _run(gen):
    try:
        while True:
            next(gen)
    except StopIteration as stop:
        return stop.value


def _interleave(*gens, steps=None):
    live = [(g, 1 if steps is None else steps[k]) for k, g in enumerate(gens)]
    while live:
        for item in list(live):
            try:
                for _ in range(item[1]):
                    next(item[0])
            except StopIteration:
                live.remove(item)


def _normalise(acc):
    l = acc[HEAD_DIM:HEAD_DIM + 1]
    return acc[0:HEAD_DIM] * jnp.where(l > 0.0, 1.0 / l, 0.0)


def _nsa_prompt_kernel(q_ref, g_ref, kc_ref, vct_ref, ks_ref, vst_ref, kw_ref, vwt_ref, cov_ref, hot_ref, sl_ref,
                       o_ref, qp_scr, oc_scr, m_scr, acc_scr, bits_ref, *, ncp, n_qt):
    j = pl.program_id(2)
    cols = HPG * LANE
    key_io = lax.broadcasted_iota(jnp.int32, (LANE, LANE), 0)
    tok_io = lax.broadcasted_iota(jnp.int32, (LANE, LANE), 1)
    sl = sl_ref[0]
    m_init = jnp.full((1, cols), M_FLOOR, F32)

    def tile4(a):
        return jnp.concatenate([a] * HPG, axis=1)

    @pl.when(j == 0)
    def _():
        qp_scr[1] = jnp.zeros(qp_scr.shape[1:], BF16)
        oc_scr[1] = jnp.zeros(oc_scr.shape[1:], F32)
        for w in range(LANE // 32):
            bits_ref[LANE // 32 + w] = 0

    i = jnp.maximum(j - 1, 0)
    slot = (j + 1) % 2

    def sel_keys(jp):
        return jnp.concatenate([hot_ref[jp], ks_ref[0, 0, jp]], axis=1)

    fi_a = i.astype(F32)
    qp = qp_scr[slot]
    rel = tile4((tok_io - key_io).astype(F32))

    def window_branch():
        qx = qp[:, LANE:2 * LANE]

        n_wt = WINDOW // LANE
        scores, offsets, vts = [], [], []
        tiles = [jnp.maximum(i - n_wt + kt_i, 0) for kt_i in range(n_wt + 1)]
        s_all = _dot_nt(jnp.concatenate([kw_ref[0, 0, t] for t in tiles], axis=0), qx)
        yield
        for kt_i in range(n_wt + 1):
            jt = i - n_wt + kt_i
            jt_c = tiles[kt_i]
            s = s_all[kt_i * LANE:(kt_i + 1) * LANE]
            if kt_i == 0:
                s = jnp.where(rel <= 0.0, s, NEG)
            elif kt_i == n_wt:
                s = jnp.where(rel >= 0.0, s, NEG)
            scores.append(s)
            offsets.append(sl * float(LANE * (n_wt - kt_i)) + jnp.where(jt >= 0, 0.0, MASK_OFF).astype(F32))
            vts.append(vwt_ref[0, jt_c])
        _, _, acc = yield from _softmax_group(scores, offsets, m_init, BF16, vts)
        return _normalise(acc)

    def near_tiles():
        fi = fi_a
        tiles = [0, i] + [jnp.maximum(i - r, 0) for r in range(1, N_NEAR + 1)]
        s_all = _dot_nt(jnp.concatenate([sel_keys(t) for t in tiles], axis=0), qp)
        yield
        scores = [s_all[k * LANE:(k + 1) * LANE] for k in range(len(tiles))]
        scores[1] = jnp.where(rel >= 0.0, scores[1], NEG)
        offsets = [sl * (LANE * fi) + jnp.where(i > N_NEAR, 0.0, MASK_OFF).astype(F32), jnp.zeros((1, cols), F32)]
        for r in range(1, N_NEAR + 1):
            offsets.append(sl * float(LANE * r) + jnp.where(i - r >= 0, 0.0, MASK_OFF).astype(F32))
        vts = [vst_ref[0, t] for t in tiles]
        m, _, acc = yield from _softmax_group(scores, offsets, m_init, BF16, vts)
        m_scr[...] = m
        acc_scr[...] = acc

    def selection_half():
        i_s = jnp.minimum(j, n_qt - 1)
        slot_s = j % 2
        fi = i_s.astype(F32)
        qx = q_ref[0].reshape(cols, LANE)

        n_chunks = ncp // LANE
        rel_c = tile4((tok_io - CMP_STRIDE * key_io).astype(F32))
        scores, offsets = [], []
        s_all = _dot_nt(kc_ref[0, 0], qx)
        yield
        for c in range(n_chunks):
            base = float(CMP_STRIDE * LANE * c + CMP_LEN - 1)
            s_c = s_all[c * LANE:(c + 1) * LANE]
            scores.append(jnp.where(rel_c >= base - LANE * fi, s_c, NEG))
            offsets.append(sl * (LANE * fi - base))
        _, _, probs = yield from _softmax_group(scores, offsets, m_init)
        p_hi, p_lo = _split_bf16(jnp.concatenate(probs, axis=0))
        acc_c = _dot(vct_ref[0, 0], p_hi)
        oc_scr[slot_s] = _normalise(acc_c)
        yield
        l_c = acc_c[HEAD_DIM:HEAD_DIM + 1]
        cov = cov_ref[...]
        imp4 = (_dot(cov, p_hi) + _dot(cov, p_lo)) * jnp.where(l_c > 0.0, 1.0 / l_c, 0.0)
        imp = imp4[:, 0:LANE]
        for h in range(1, HPG):
            imp = imp + imp4[:, h * LANE:(h + 1) * LANE]
        yield

        q_blk = 2 * i_s + tok_io // SEL_BLOCK
        forced = (key_io == 0) | (key_io == q_blk) | (key_io == q_blk - 1)
        valid = key_io <= q_blk
        work = jnp.where(valid & jnp.logical_not(forced), imp, NEG)
        sel = yield from _select_blocks(work, key_io.astype(F32), 0, N_SEL - 3)
        sel = jnp.where(valid & forced, 1.0, sel)
        sel_bias_t = jnp.where(sel.T > 0.5, 0.0, NEG).astype(BF16)
        for h in range(HPG):
            qp_scr[slot_s, h * LANE:(h + 1) * LANE, 0:LANE] = sel_bias_t
            qp_scr[slot_s, h * LANE:(h + 1) * LANE, LANE:2 * LANE] = q_ref[0, h]

        cnt = _dot_nt(jnp.ones((8, LANE), BF16), sel.astype(BF16))[0:1]
        lane = lax.broadcasted_iota(jnp.int32, (1, LANE), 1)
        live = (cnt > 0.0) & (lane >= 2) & (lane < 2 * (i_s - N_NEAR))
        for w in range(LANE // 32):
            bits_ref[slot_s * (LANE // 32) + w] = jnp.sum(
                jnp.where(live & (lane // 32 == w), jnp.left_shift(1, lane % 32), 0))

    box = []

    def window_result():
        box.append((yield from window_branch()))

    _interleave(selection_half(), window_result(), near_tiles(), steps=(2, 1, 1))
    o_win = box[0]

    def group_body(k, carry):
        word = bits_ref[slot * (LANE // 32) + k // 4]
        used = jnp.right_shift(word, (8 * k) % 32) & 255

        @pl.when(used != 0)
        def _():
            offsets, vts = [], []
            s_all = _dot_nt(jnp.concatenate([sel_keys(4 * k + r) for r in range(4)], axis=0), qp_scr[slot])
            scores = [s_all[r * LANE:(r + 1) * LANE] for r in range(4)]
            for r in range(4):
                jp = 4 * k + r
                dead = jnp.where((jp == 0) | (jp >= i - N_NEAR), MASK_OFF, 0.0).astype(F32)
                offsets.append(sl * (LANE * (i - jp)).astype(F32) + dead)
                vts.append(vst_ref[0, jp])
            m, alpha, acc = _run(_softmax_group(scores, offsets, m_scr[...], BF16, vts))
            m_scr[...] = m
            acc_scr[...] = alpha * acc_scr[...] + acc
        return carry

    lax.fori_loop(0, jnp.maximum(i - N_NEAR + 3, 0) // 4, group_body, 0)
    o_sel = _normalise(acc_scr[...])

    o_cmp = oc_scr[slot]
    for h in range(HPG):
        cs = slice(h * LANE, (h + 1) * LANE)
        g_c = g_ref[0, 0, 3 * h:3 * h + 1, :]
        g_s = g_ref[0, 0, 3 * h + 1:3 * h + 2, :]
        g_w = g_ref[0, 0, 3 * h + 2:3 * h + 3, :]
        o_ref[0, h * HEAD_DIM:(h + 1) * HEAD_DIM, :] = g_c * o_cmp[:, cs] + g_s * o_sel[:, cs] + g_w * o_win[:, cs]


def _nsa_prompt(q, gates_t, kc, vct, k_ext, v_t, cov_t, onehot, sl_rows, n_seq, seq_len):
    n_qt = seq_len // LANE
    ncp = kc.shape[2]
    assert n_qt % 4 == 0
    kern = functools.partial(_nsa_prompt_kernel, ncp=ncp, n_qt=n_qt)
    cols = HPG * LANE
    last = n_qt - 1

    def sel_tile(j):
        return jnp.minimum(j, last)

    def att_tile(j):
        return jnp.maximum(j - 1, 0)

    return pl.pallas_call(
        kern,
        grid=(n_seq, N_KV, n_qt + 1),
        in_specs=[pl.BlockSpec((1, HPG, LANE, LANE), lambda b, g, j: (b, g, sel_tile(j), 0)),
                  pl.BlockSpec((1, 1, 16, LANE), lambda b, g, j: (b, g, 0, att_tile(j))),
                  pl.BlockSpec((1, 1, ncp, LANE), lambda b, g, j: (b, g, 0, 0)),
                  pl.BlockSpec((1, 1, VROWS, ncp), lambda b, g, j: (b, g, 0, 0)),
                  pl.BlockSpec((1, 1, n_qt, LANE, LANE), lambda b, g, j: (b, g, 0, 0, 0)),
                  pl.BlockSpec((1, n_qt, VROWS, LANE), lambda b, g, j: (b, 0, g, 0)),
                  pl.BlockSpec((1, 1, n_qt, LANE, LANE), lambda b, g, j: (b, N_KV + g, 0, 0, 0)),
                  pl.BlockSpec((1, n_qt, VROWS, LANE), lambda b, g, j: (b, 0, N_KV + g, 0)),
                  pl.BlockSpec((LANE, ncp), lambda b, g, j: (0, 0)),
                  pl.BlockSpec((n_qt, LANE, LANE), lambda b, g, j: (0, 0, 0)),
                  pl.BlockSpec((1, 1, cols), lambda b, g, j: (g, 0, 0))],
        out_specs=pl.BlockSpec((1, HPG * HEAD_DIM, LANE), lambda b, g, j: (b, g, att_tile(j))),
        out_shape=jax.ShapeDtypeStruct((n_seq, N_HEADS * HEAD_DIM, seq_len), F32),
        scratch_shapes=[pltpu.VMEM((2, cols, 2 * LANE), BF16),
                        pltpu.VMEM((2, HEAD_DIM, cols), F32),
                        pltpu.VMEM((1, cols), F32),
                        pltpu.VMEM((VROWS, cols), F32),
                        pltpu.SMEM((2 * (LANE // 32),), jnp.int32)],
        compiler_params=_params(3),
    )(q, gates_t, kc, vct, k_ext, v_t, k_ext, v_t, cov_t, onehot, sl_rows)


def _nsa_sample_kernel(tbl_ref, *refs, n_pages, n_e):
    del tbl_ref
    shared = refs[n_e * n_pages:]
    _interleave(*[_sample_sequence(e, refs[e * n_pages:(e + 1) * n_pages], *shared) for e in range(n_e)])


def _sample_sequence(e, pages, q_ref, g_ref, kc_ref, vct_ref, win_ref, new_ref, cov_ref, exp_ref, slope_ref,
                     o_ref, s_scr):
    n_pages = len(pages)
    gd = N_KV * HEAD_DIM
    past = n_pages * PAGE
    q_pos = float(past)
    qb = q_ref[e].astype(BF16)
    slope = slope_ref[...]
    row_grp = lax.broadcasted_iota(jnp.int32, (N_HEADS, LANE), 0) // HPG
    lane_f = lax.broadcasted_iota(jnp.int32, (N_HEADS, LANE), 1).astype(F32)
    new = new_ref[e]

    def own_rows(g_sel, pick):
        out = pick(0)
        for grp in range(1, N_KV):
            out = jnp.where(g_sel == grp, pick(grp), out)
        return out

    q_f = q_ref[e]
    row_grp_o = lax.broadcasted_iota(jnp.int32, (N_HEADS, HEAD_DIM), 0) // HPG
    q_pair = [jnp.concatenate([jnp.where(row_grp_o == 2 * pr, q_f, 0.0), jnp.where(row_grp_o == 2 * pr + 1, q_f, 0.0)],
                              axis=1).astype(BF16) for pr in range(N_KV // 2)]
    first_row2 = lax.broadcasted_iota(jnp.int32, (LANE, 2 * HEAD_DIM), 0) == 0

    def pair_scores(tile):
        return _dot_nt(q_pair[0], tile(0)) + _dot_nt(q_pair[1], tile(1))

    def own_halves(o_pairs):
        out = o_pairs[0][:, 0:HEAD_DIM]
        for grp in range(1, N_KV):
            half = o_pairs[grp // 2][:, (grp % 2) * HEAD_DIM:(grp % 2 + 1) * HEAD_DIM]
            out = jnp.where(row_grp_o == grp, half, out)
        return out

    def new_slab(slot, pr):
        col = slot * gd + pr * 2 * HEAD_DIM
        return jnp.where(first_row2, new[:, col:col + 2 * HEAD_DIM], 0.0).astype(BF16)

    def softmax_rows(s, valid):
        s = jnp.where(valid, s, NEG)
        m = jnp.max(s, axis=1, keepdims=True)
        p = jnp.where(valid, jnp.exp(s - m), 0.0)
        l = jnp.sum(p, axis=1, keepdims=True)
        return p * jnp.where(l > 0.0, 1.0 / l, 0.0)

    d_c = q_pos - (CMP_STRIDE * lane_f + (CMP_LEN - 1))
    s_c = own_rows(row_grp, lambda grp: _dot_nt(qb, kc_ref[e, grp][:, 0:HEAD_DIM])) - slope * d_c
    yield
    p_c = softmax_rows(s_c, d_c >= 0.0)
    p_cb = p_c.astype(BF16)
    o_c = own_rows(row_grp_o, lambda grp: _dot_nt(p_cb, vct_ref[e, grp, 0:HEAD_DIM, :]))
    yield

    p_grp = own_rows(row_grp, lambda grp: jnp.broadcast_to(
        jnp.sum(p_c[grp * HPG:(grp + 1) * HPG], axis=0, keepdims=True), (N_HEADS, LANE)))
    p_hi, p_lo = _split_bf16(jnp.concatenate([p_grp, jnp.zeros((LANE - N_HEADS, LANE), F32)], axis=0))
    cov_t = cov_ref[...]
    imp_t = _dot_nt(cov_t, p_hi) + _dot_nt(cov_t, p_lo)
    blk = lax.broadcasted_iota(jnp.int32, (LANE, LANE), 0)
    q_blk = past // SEL_BLOCK
    forced = (blk == 0) | (blk == q_blk) | (blk == q_blk - 1)
    valid = blk <= q_blk
    work = jnp.where(valid & jnp.logical_not(forced), imp_t, NEG)
    yield
    sel_t = yield from _select_blocks(work, blk.astype(F32), 0, N_SEL - 3)
    sel = jnp.where(valid & forced, 1.0, sel_t).T[0:N_HEADS]

    n_t = n_pages + 1
    for t in range(n_t):
        if t < n_pages:
            tile = lambda pr, t=t: pages[t][0, :, pr * LANE:(pr + 1) * LANE].astype(BF16)
        else:
            tile = lambda pr: new_slab(2, pr)
        s_scr[e, :, t * LANE:(t + 1) * LANE] = pair_scores(tile)
        if t % 4 == 3:
            yield
    width = n_t * LANE
    kpos = lax.broadcasted_iota(jnp.int32, (N_HEADS, width), 1).astype(F32)
    d_s = q_pos - kpos
    chosen = _dot(sel.astype(BF16), exp_ref[...])
    slope_w = jnp.concatenate([slope] * n_t, axis=1)
    p_s = softmax_rows(s_scr[e] - slope_w * d_s, (chosen > 0.5) & (d_s >= 0.0)).astype(BF16)
    yield
    o_sp = [jnp.zeros((N_HEADS, 2 * HEAD_DIM), F32) for _ in range(N_KV // 2)]
    for t in range(n_t):
        pt = p_s[:, t * LANE:(t + 1) * LANE]
        if t < n_pages:
            vtile = lambda pr, t=t: pages[t][0, :, gd + pr * LANE: gd + (pr + 1) * LANE].astype(BF16)
        else:
            vtile = lambda pr: new_slab(3, pr)
        o_sp = [o_sp[pr] + _dot(pt, vtile(pr)) for pr in range(N_KV // 2)]
        if t % 4 == 3:
            yield
    o_s = own_halves(o_sp)

    buf = win_ref.shape[1]
    n_w = buf // LANE + 1
    w_parts = []
    for t in range(n_w):
        if t < n_w - 1:
            tile = lambda pr, t=t: win_ref[e, t * LANE:(t + 1) * LANE, pr * LANE:(pr + 1) * LANE].astype(BF16)
        else:
            tile = lambda pr: new_slab(4, pr)
        w_parts.append(pair_scores(tile))
    s_w = jnp.concatenate(w_parts, axis=1)
    yield
    wpos = lax.broadcasted_iota(jnp.int32, (N_HEADS, n_w * LANE), 1).astype(F32) + float(past - buf)
    d_w = q_pos - wpos
    slope_ww = jnp.concatenate([slope] * n_w, axis=1)
    p_w = softmax_rows(s_w - slope_ww * d_w, (d_w >= 0.0) & (d_w <= float(WINDOW))).astype(BF16)
    yield
    o_wp = [jnp.zeros((N_HEADS, 2 * HEAD_DIM), F32) for _ in range(N_KV // 2)]
    for t in range(n_w):
        pt = p_w[:, t * LANE:(t + 1) * LANE]
        if t < n_w - 1:
            vtile = lambda pr, t=t: win_ref[e, t * LANE:(t + 1) * LANE,
                                            gd + pr * LANE: gd + (pr + 1) * LANE].astype(BF16)
        else:
            vtile = lambda pr: new_slab(5, pr)
        o_wp = [o_wp[pr] + _dot(pt, vtile(pr)) for pr in range(N_KV // 2)]
    o_w = own_halves(o_wp)

    gt = g_ref[e]
    o_ref[e] = gt[:, 0:1] * o_c + gt[:, 1:2] * o_s + gt[:, 2:3] * o_w


def _nsa_sample(q, gates, kc, vct, cache_pages, page_table, win, kv_new, cov, expand, slope16):
    n_seq, n_pages = page_table.shape
    gd = N_KV * HEAD_DIM
    buf = win.shape[1]
    n_e = SAMPLE_SEQS if n_seq % SAMPLE_SEQS == 0 else 1
    kern = functools.partial(_nsa_sample_kernel, n_pages=n_pages, n_e=n_e)

    def page_spec(e, p):
        return pl.BlockSpec((1, PAGE, 2 * gd), lambda b, tbl, e=e, p=p: (tbl[b * n_e + e, p], 0, 1))

    def const_spec(shape):
        nd = len(shape)
        return pl.BlockSpec(shape, lambda b, tbl: (0,) * nd)

    grid_spec = pltpu.PrefetchScalarGridSpec(
        num_scalar_prefetch=1,
        grid=(n_seq // n_e,),
        in_specs=[page_spec(e, p) for e in range(n_e) for p in range(n_pages)] + [
            pl.BlockSpec((n_e, N_HEADS, HEAD_DIM), lambda b, tbl: (b, 0, 0)),
            pl.BlockSpec((n_e, N_HEADS, 3), lambda b, tbl: (b, 0, 0)),
            pl.BlockSpec((n_e, N_KV, LANE, LANE), lambda b, tbl: (b, 0, 0, 0)),
            pl.BlockSpec((n_e, N_KV, VROWS, LANE), lambda b, tbl: (b, 0, 0, 0)),
            pl.BlockSpec((n_e, buf, 2 * gd), lambda b, tbl: (b, 0, 0)),
            pl.BlockSpec((n_e, 1, N_KV_PROJ * gd), lambda b, tbl: (b, 0, 0)),
            const_spec(cov.shape), const_spec(expand.shape), const_spec(slope16.shape)],
        out_specs=pl.BlockSpec((n_e, N_HEADS, HEAD_DIM), lambda b, tbl: (b, 0, 0)),
        scratch_shapes=[pltpu.VMEM((n_e, N_HEADS, (n_pages + 1) * LANE), F32)])
    return pl.pallas_call(
        kern,
        grid_spec=grid_spec,
        out_shape=jax.ShapeDtypeStruct((n_seq, N_HEADS, HEAD_DIM), F32),
        compiler_params=_params(1),
    )(page_table, *([cache_pages] * (n_e * n_pages)), q, gates, kc, vct, win, kv_new, cov, expand, slope16)


def _oproj_prompt_kernel(ot_ref, x_ref, gt_ref, wo_ref, o_ref):
    o = ot_ref[0].T.astype(BF16)
    o_ref[...] = x_ref[...] + gt_ref[0] * _dot(o, wo_ref[...])


def _oproj_sample_kernel(a_ref, x_ref, gt_ref, wo_ref, o_ref):
    o_ref[...] = x_ref[...] + gt_ref[0] * _dot(a_ref[...].astype(BF16), wo_ref[...])


def _out_proj(tok, attn, x, gate, wo):
    tm, nt = tok.tiles(512)
    if tok.per_row:
        kern, a_spec = _oproj_sample_kernel, tok.x_spec(tm)
    else:
        per_seq = tok.seq_len // tm
        kern = _oproj_prompt_kernel
        a_spec = pl.BlockSpec((1, D_MODEL, tm), lambda i: (i // per_seq, 0, i % per_seq))
    return pl.pallas_call(
        kern, grid=(nt,),
        in_specs=[a_spec, tok.x_spec(tm), tok.mod_spec(tm), _full_spec((D_MODEL, D_MODEL))],
        out_specs=tok.x_spec(tm),
        out_shape=jax.ShapeDtypeStruct((tok.n_tok, D_MODEL), F32),
        compiler_params=_params(1))(attn, x, tok.mod_arr(gate), wo)


def _final_kernel(x_ref, sh_ref, sc_ref, g_ref, o_ref):
    o_ref[...] = _ada_norm(x_ref[...], g_ref[...], sh_ref[0], sc_ref[0])


def _final(tok, x, shift, scale, g):
    tm, nt = tok.tiles(512)
    return pl.pallas_call(
        _final_kernel, grid=(nt,),
        in_specs=[tok.x_spec(tm), tok.mod_spec(tm), tok.mod_spec(tm), _full_spec((1, D_MODEL))],
        out_specs=tok.x_spec(tm),
        out_shape=jax.ShapeDtypeStruct((tok.n_tok, D_MODEL), F32),
        compiler_params=_params(1))(x, tok.mod_arr(shift), tok.mod_arr(scale), g.reshape(1, D_MODEL))


def _alibi_slopes():
    h = jnp.arange(1, N_HEADS + 1, dtype=F32)
    return jnp.exp2(-8.0 * h / N_HEADS)


def _cover(n_cmp, n_sel):
    c_start = jnp.arange(n_cmp)[:, None] * CMP_STRIDE
    s_start = jnp.arange(n_sel)[None, :] * SEL_BLOCK
    return ((c_start < s_start + SEL_BLOCK) & (c_start + CMP_LEN > s_start)).astype(BF16)


def _trunk(tok, x, mods, kv_mod, f_mod, wts, ctx):
    depth = wts['norm_g'].shape[0]
    n_a = depth // 2
    v_rows = []
    kv = None
    attn_ctx = None
    for l in range(depth):
        m = mods[l]
        if l == n_a:
            kv_out = _kv_proj(tok, x, kv_mod[:, 0], kv_mod[:, 1], wts['kv_norm_g'], wts['kv_w'], wts['kv_wvt'],
                              wts['alibi_k'])
            kv, attn_ctx = ctx['prepare'](kv_out)
        x = _ffn(tok, x, m[:, 0], m[:, 1], m[:, 2], wts['norm_g'][l, 0],
                 wts['ffn_w_gate'][l, 0], wts['ffn_w_up'][l, 0], wts['ffn_w_down'][l, 0])
        if l < n_a:
            x, v = _gmlp(tok, x, m[:, 3], m[:, 4], m[:, 5], wts['norm_g'][l, 1], wts['gmlp_w_uv'][l],
                         wts['gmlp_ln_g'][l], wts['gmlp_ln_b'][l], wts['gmlp_w_sp'][l], wts['gmlp_b_sp'][l],
                         wts['gmlp_w_out'][l])
            v_rows.append(v)
        else:
            j = l - n_a
            q, gates = _qg_proj(tok, x, m[:, 3], m[:, 4], wts['norm_g'][l, 1], wts['nsa_wq'][j],
                                wts['nsa_wg_t'][j], wts['nsa_wg'][j], wts['alibi_q'])
            attn = ctx['attend'](q, gates, attn_ctx)
            x = _out_proj(tok, attn, x, m[:, 5], wts['nsa_w_o'][j])
        x = _ffn(tok, x, m[:, 6], m[:, 7], m[:, 8], wts['norm_g'][l, 2],
                 wts['ffn_w_gate'][l, 1], wts['ffn_w_up'][l, 1], wts['ffn_w_down'][l, 1])
    y = _final(tok, x, f_mod[:, 0], f_mod[:, 1], wts['final_g'])
    return y, kv, v_rows


def kernel(x_prompt, x_sample, cache_kv, state_win_kv, page_table, c_prompt, c_sample, ada_w, ada_b, norm_g, ffn_w_gate, ffn_w_up, ffn_w_down, gmlp_w_uv, gmlp_ln_g, gmlp_ln_b, gmlp_w_sp, gmlp_b_sp, gmlp_w_out, nsa_w_qg, nsa_w_o, kv_norm_g, kv_ada_w, kv_ada_b, kv_w, cmp_w1, cmp_w2, cmp_pe, final_g, final_ada_w, final_ada_b):
    n_p, seq, _ = x_prompt.shape
    n_s, dec_seq, _ = x_sample.shape
    assert dec_seq == 1 and seq % (PAGES_PER_SEG * PAGE) == 0
    depth = ada_w.shape[0]
    n_b = nsa_w_qg.shape[0]
    gd = N_KV * HEAD_DIM
    n_pages = page_table.shape[1]
    assert n_pages == PAGES_PER_SEG
    past = n_pages * PAGE

    nq = N_HEADS * HEAD_DIM
    wg_cols = nsa_w_qg[:, :, nq:]
    wg_pad = jnp.pad(wg_cols.reshape(n_b, D_MODEL, N_KV, HPG * 3), ((0, 0), (0, 0), (0, 0), (0, 16 - HPG * 3)))
    wts = dict(
        norm_g=norm_g, kv_norm_g=kv_norm_g, final_g=final_g,
        ffn_w_gate=ffn_w_gate.astype(BF16), ffn_w_up=ffn_w_up.astype(BF16), ffn_w_down=ffn_w_down.astype(BF16),
        gmlp_w_uv=gmlp_w_uv.astype(BF16), gmlp_ln_g=gmlp_ln_g, gmlp_ln_b=gmlp_ln_b,
        gmlp_w_sp=gmlp_w_sp, gmlp_b_sp=gmlp_b_sp, gmlp_w_out=gmlp_w_out.astype(BF16),
        nsa_wq=nsa_w_qg[:, :, :nq].astype(BF16),
        nsa_wg_t=jnp.swapaxes(wg_pad.reshape(n_b, D_MODEL, N_KV * 16), 1, 2).astype(BF16),
        nsa_wg=jnp.pad(wg_cols, ((0, 0), (0, 0), (0, LANE - N_HEADS * 3))).astype(BF16),
        nsa_w_o=nsa_w_o.astype(BF16),
        kv_w=kv_w.astype(BF16),
        kv_wvt=jnp.concatenate([kv_w[:, 3 * gd:4 * gd], kv_w[:, 5 * gd:6 * gd]], axis=1).T.astype(BF16),
    )
    w1 = cmp_w1.astype(BF16)
    w2 = cmp_w2.astype(BF16)
    w2t = jnp.swapaxes(cmp_w2, 1, 2).astype(BF16)
    pe = jnp.broadcast_to(cmp_pe.astype(BF16).reshape(2, 1, CMP_LEN * HEAD_DIM), (2, 16, CMP_LEN * HEAD_DIM))
    slopes = _alibi_slopes()
    sl2 = slopes * LOG2E
    off_f = jnp.arange(LANE, dtype=F32)

    def split3(x):
        a = x.astype(BF16)
        b = (x - a.astype(F32)).astype(BF16)
        c = (x - a.astype(F32) - b.astype(F32)).astype(BF16)
        return [a, b, c]

    q_cols = split3(-sl2[:, None] * off_f[None, :]) + \
        [jnp.broadcast_to(c[:, None], (N_HEADS, LANE)) for c in split3(sl2)] + \
        [jnp.broadcast_to(c[:, None], (N_HEADS, LANE)) for c in split3(sl2 * CMP_STRIDE)]
    wts['alibi_q'] = jnp.pad(jnp.stack(q_cols, axis=-1), ((0, 0), (0, 0), (0, HEAD_DIM - 9)))
    ones, zeros, offs = jnp.ones((LANE,), BF16), jnp.zeros((LANE,), BF16), off_f.astype(BF16)
    wts['alibi_k'] = jnp.pad(jnp.stack([ones] * 3 + [offs] * 3 + [zeros] * 3, axis=-1), ((0, 0), (0, HEAD_DIM - 9)))
    alibi_kc = jnp.pad(jnp.stack([ones] * 3 + [zeros] * 3 + [offs] * 3, axis=-1), ((0, 0), (0, HEAD_DIM - 9)))

    n_c = n_p + n_s
    c_all = jnp.pad(jnp.concatenate([c_prompt, c_sample], axis=0), ((0, (-n_c) % 8), (0, 0)))
    mod_all = _mod_linear(c_all, ada_w, ada_b)
    kv_mod_all = _mod_linear(c_all, kv_ada_w[None], kv_ada_b[None])[0]
    f_mod_all = _mod_linear(c_all, final_ada_w[None], final_ada_b[None])[0]

    def rows(a, lo, hi, k):
        return a[..., lo:hi, :].reshape(a.shape[:-2] + (hi - lo, k, D_MODEL))

    tok_p = _Tok(n_p, seq, per_row=False)
    n_seg = seq // (PAGES_PER_SEG * PAGE)
    n_qt = seq // LANE
    ncp = n_seg * PAGES_PER_SEG * 8
    n_sel_p = seq // SEL_BLOCK
    assert n_sel_p <= LANE
    cov_t = jnp.pad(_cover(ncp, n_sel_p).T, ((0, LANE - n_sel_p), (0, 0)))
    sl_rows = jnp.repeat(sl2.reshape(N_KV, 1, HPG), LANE, axis=2)
    onehot = (jnp.arange(LANE)[None, None, :] ==
              (2 * jnp.arange(n_qt)[:, None, None] + jnp.arange(LANE)[None, :, None] // SEL_BLOCK)).astype(BF16)

    def prepare_p(kv_out):
        kv, k_nat, v_t = kv_out
        pages_per_seq = seq // PAGE
        base = jnp.arange(n_p * n_seg, dtype=jnp.int32)[:, None] * PAGES_PER_SEG
        table = jnp.minimum(base + jnp.arange(PAGES_PER_SEG + 1, dtype=jnp.int32)[None, :],
                            n_p * pages_per_seq - 1)
        kc, vct = _compress(kv.reshape(n_p * pages_per_seq, PAGE, N_KV_PROJ * gd), table, w1, w2, w2t, pe,
                            alibi_kc, n_p, n_seg)
        return kv, (kc, vct, k_nat, v_t)

    def attend_p(q, gates, c):
        kc, vct, k_nat, v_t = c
        return _nsa_prompt(q, gates, kc, vct, k_nat, v_t, cov_t, onehot, sl_rows, n_p, seq)

    y_p, kv_p, _ = _trunk(tok_p, x_prompt.reshape(n_p * seq, D_MODEL),
                          rows(mod_all, 0, n_p, 9), rows(kv_mod_all, 0, n_p, 2), rows(f_mod_all, 0, n_p, 2),
                          wts, dict(prepare=prepare_p, attend=attend_p))
    kv_p = kv_p.reshape(n_p, seq, N_KV_PROJ, N_KV, HEAD_DIM)
    kv_prompt = kv_p[:, :, :4]
    win_prompt = kv_p[:, seq - min(WINDOW, seq):, 4:6]

    tok_s = _Tok(n_s, 1, per_row=True)
    cache_pages = cache_kv.reshape(cache_kv.shape[0], PAGE, 4 * gd)
    buf = state_win_kv.shape[1]
    win_flat = state_win_kv.reshape(n_s, buf, 2 * gd)
    n_sel_s = (past + 1 + SEL_BLOCK - 1) // SEL_BLOCK
    cov_s = jnp.pad(_cover(LANE, n_sel_s).T, ((0, LANE - n_sel_s), (0, 0)))
    n_keys = (n_pages + 1) * LANE
    expand_s = (jnp.arange(LANE)[:, None] == (jnp.arange(n_keys)[None, :] // SEL_BLOCK)).astype(BF16)
    slope16 = jnp.broadcast_to(slopes[:, None], (N_HEADS, LANE))

    def prepare_s(kv):
        table = jnp.concatenate([page_table, page_table[:, -1:]], axis=1)
        kc, vct = _compress(cache_pages, table, w1, w2, w2t, pe, alibi_kc, n_s, 1)
        return kv, (kc, vct, kv)

    def attend_s(q, gates, c):
        kc, vct, kv = c
        o = _nsa_sample(q.reshape(n_s, N_HEADS, HEAD_DIM), gates[:, :N_HEADS * 3].reshape(n_s, N_HEADS, 3),
                        kc, vct, cache_pages, page_table, win_flat, kv.reshape(n_s, 1, N_KV_PROJ * gd),
                        cov_s, expand_s, slope16)
        return o.reshape(n_s, N_HEADS * HEAD_DIM)

    y_s, kv_s, v_rows = _trunk(tok_s, x_sample.reshape(n_s, D_MODEL),
                               rows(mod_all, n_p, n_c, 9), rows(kv_mod_all, n_p, n_c, 2),
                               rows(f_mod_all, n_p, n_c, 2), wts, dict(prepare=prepare_s, attend=attend_s))
    kv_s = kv_s.reshape(n_s, 1, N_KV_PROJ, N_KV, HEAD_DIM)
    kv_sample = kv_s[:, :, :4]
    win_sample = jnp.concatenate([state_win_kv, kv_s[:, :, 4:6]], axis=1)[:, 1:]
    gmlp_v_sample = jnp.stack([v.reshape(n_s, 1, D_V) for v in v_rows])

    return (y_p.reshape(n_p, seq, D_MODEL), y_s.reshape(n_s, 1, D_MODEL), kv_prompt, kv_sample,
            win_prompt, win_sample, gmlp_v_sample)
```

```python
import functools

import jax
import jax.numpy as jnp
from jax import lax
from jax.experimental import pallas as pl
from jax.experimental.pallas import tpu as pltpu

F32 = jnp.float32
BF16 = jnp.bfloat16

D_MODEL = 1024
D_FF = 2816
D_V = 3072
CHUNK = 128
N_GROUPS_A = 8
CG = D_V // N_GROUPS_A
N_HEADS = 16
HEAD_DIM = 64
N_KV = 4
HPG = N_HEADS // N_KV
CMP_LEN = 32
CMP_STRIDE = 16
CMP_HID = 256
SEL_BLOCK = 64
N_SEL = 16
WINDOW = 512
PAGE = 128
N_KV_PROJ = 6
EPS = 1e-6
NEG = -1e30
M_FLOOR = -1e29
PICKED = -3e38
LOG2E = 1.4426950408889634
MASK_OFF = 1e30

LANE = 128
VMEM_LIMIT = 56 * 1024 * 1024
FF_CHUNK = 256
UV_CHUNK = 512
GATE_ROWS = 16
VROWS = HEAD_DIM + 16
N_NEAR = 8
SAMPLE_SEQS = 4
PAGES_PER_SEG = 16
XROWS = PAGES_PER_SEG * 8 + 8


def _dot(a, b):
    return jnp.dot(a, b, preferred_element_type=F32)


def _dot_nt(a, b):
    return lax.dot_general(a, b, (((1,), (1,)), ((), ())), preferred_element_type=F32)


def _split_bf16(x):
    hi = x.astype(BF16)
    lo = (x - hi.astype(F32)).astype(BF16)
    return hi, lo


def _params(n_grid):
    return pltpu.CompilerParams(dimension_semantics=("arbitrary",) * n_grid,
                                vmem_limit_bytes=VMEM_LIMIT)


def _full_spec(shape):
    nd = len(shape)
    return pl.BlockSpec(shape, lambda *_: (0,) * nd, pipeline_mode=pl.Buffered(1))


def _ada_norm(x, g, shift, scale):
    ms = jnp.mean(x * x, axis=-1, keepdims=True)
    h = x * lax.rsqrt(ms + EPS) * g
    return h * (1.0 + scale) + shift


def _mod_kernel(c_ref, w_ref, b_ref, o_ref):
    c = c_ref[...]
    a = c * jax.nn.sigmoid(c)
    w = w_ref[0]
    a_hi, a_lo = _split_bf16(a)
    w_hi, w_lo = _split_bf16(w)
    o_ref[0] = _dot(a_hi, w_hi) + _dot(a_hi, w_lo) + _dot(a_lo, w_hi) + b_ref[0]


def _mod_linear(c, w, b):
    n_l, d, n = w.shape
    m = c.shape[0]
    tn = 1024
    return pl.pallas_call(
        _mod_kernel,
        grid=(n_l, n // tn),
        in_specs=[pl.BlockSpec((m, d), lambda l, j: (0, 0)),
                  pl.BlockSpec((1, d, tn), lambda l, j: (l, 0, j)),
                  pl.BlockSpec((1, 1, tn), lambda l, j: (l, 0, j))],
        out_specs=pl.BlockSpec((1, m, tn), lambda l, j: (l, 0, j)),
        out_shape=jax.ShapeDtypeStruct((n_l, m, n), F32),
        compiler_params=_params(2),
    )(c, w, b.reshape(n_l, 1, n))


class _Tok:
    def __init__(self, n_seq, seq_len, per_row):
        self.n_seq, self.seq_len = n_seq, seq_len
        self.n_tok = n_seq * seq_len
        self.per_row = per_row

    def tiles(self, tm):
        if self.per_row:
            return self.n_tok, 1
        assert self.seq_len % tm == 0
        return tm, self.n_tok // tm

    def x_spec(self, tm, width=D_MODEL):
        return pl.BlockSpec((tm, width), lambda i: (i, 0))

    def mod_spec(self, tm):
        if self.per_row:
            return pl.BlockSpec((1, tm, D_MODEL), lambda i: (0, 0, 0))
        per_seq = self.seq_len // tm
        return pl.BlockSpec((1, 1, D_MODEL), lambda i: (i // per_seq, 0, 0))

    def mod_arr(self, m):
        if self.per_row:
            return m.reshape(1, self.n_tok, D_MODEL)
        return m.reshape(self.n_seq, 1, D_MODEL)


def _ffn_kernel(x_ref, sh_ref, sc_ref, gt_ref, g_ref, wg_ref, wu_ref, wd_ref, *rest, final_norm):
    o_ref, a_scr = rest[-2:]
    x = x_ref[...]
    hb = _ada_norm(x, g_ref[...], sh_ref[0], sc_ref[0]).astype(BF16)
    for c in range(D_FF // FF_CHUNK):
        cs = slice(c * FF_CHUNK, (c + 1) * FF_CHUNK)
        gate = _dot(hb, wg_ref[:, cs])
        up = _dot(hb, wu_ref[:, cs])
        a_scr[:, cs] = (gate * jax.nn.sigmoid(gate) * up).astype(BF16)
    y = _dot(a_scr[...], wd_ref[...])
    x = x + (0.5 * gt_ref[0]) * y
    if final_norm:
        fsh_ref, fsc_ref, fg_ref = rest[:3]
        x = _ada_norm(x, fg_ref[...], fsh_ref[0], fsc_ref[0])
    o_ref[...] = x


def _ffn(tok, x, shift, scale, gate, g, wg, wu, wd, final=None):
    tm, nt = tok.tiles(512)
    extra_specs, extra_args = [], []
    if final is not None:
        extra_specs = [tok.mod_spec(tm), tok.mod_spec(tm), _full_spec((1, D_MODEL))]
        extra_args = [tok.mod_arr(final[0]), tok.mod_arr(final[1]), final[2].reshape(1, D_MODEL)]
    return pl.pallas_call(
        functools.partial(_ffn_kernel, final_norm=final is not None),
        grid=(nt,),
        in_specs=[tok.x_spec(tm), tok.mod_spec(tm), tok.mod_spec(tm), tok.mod_spec(tm),
                  _full_spec((1, D_MODEL)), _full_spec((D_MODEL, D_FF)),
                  _full_spec((D_MODEL, D_FF)), _full_spec((D_FF, D_MODEL))] + extra_specs,
        out_specs=tok.x_spec(tm),
        out_shape=jax.ShapeDtypeStruct((tok.n_tok, D_MODEL), F32),
        scratch_shapes=[pltpu.VMEM((tm, D_FF), BF16)],
        compiler_params=_params(1),
    )(x, tok.mod_arr(shift), tok.mod_arr(scale), tok.mod_arr(gate), g.reshape(1, D_MODEL), wg, wu, wd,
      *extra_args)


def _gmlp_uv(x_ref, sh_ref, sc_ref, g_ref, wuv_ref, lng_ref, lnb_ref, u_scr, v_scr):
    hb = _ada_norm(x_ref[...], g_ref[...], sh_ref[0], sc_ref[0]).astype(BF16)
    for c in range(D_V // UV_CHUNK):
        cs = slice(c * UV_CHUNK, (c + 1) * UV_CHUNK)
        cv = slice(D_V + c * UV_CHUNK, D_V + (c + 1) * UV_CHUNK)
        u_scr[:, cs] = jax.nn.gelu(_dot(hb, wuv_ref[:, cs]))
        v_scr[:, cs] = jax.nn.gelu(_dot(hb, wuv_ref[:, cv]))
    v = v_scr[...]
    mu = jnp.mean(v, axis=-1, keepdims=True)
    vc = v - mu
    var = jnp.mean(vc * vc, axis=-1, keepdims=True)
    return vc * lax.rsqrt(var + EPS) * lng_ref[...] + lnb_ref[...]


def _gmlp_prompt_kernel(x_ref, sh_ref, sc_ref, gt_ref, g_ref, wuv_ref, lng_ref, lnb_ref,
                        wsp_ref, bsp_ref, wout_ref, o_ref, u_scr, v_scr, a_scr):
    v_scr[...] = _gmlp_uv(x_ref, sh_ref, sc_ref, g_ref, wuv_ref, lng_ref, lnb_ref, u_scr, v_scr)
    row = lax.broadcasted_iota(jnp.int32, (CHUNK, CHUNK), 0)
    col = lax.broadcasted_iota(jnp.int32, (CHUNK, CHUNK), 1)
    tm = x_ref.shape[0]
    for grp in range(N_GROUPS_A):
        w = jnp.where(row >= col, wsp_ref[grp], 0.0).astype(BF16)
        cs = slice(grp * CG, (grp + 1) * CG)
        for n in range(tm // CHUNK):
            rs = slice(n * CHUNK, (n + 1) * CHUNK)
            s = _dot(w, v_scr[rs, cs].astype(BF16)) + bsp_ref[:, cs]
            a_scr[rs, cs] = (u_scr[rs, cs] * s).astype(BF16)
    y = _dot(a_scr[...], wout_ref[...])
    o_ref[...] = x_ref[...] + gt_ref[0] * y


def _gmlp_sample_kernel(x_ref, sh_ref, sc_ref, gt_ref, g_ref, wuv_ref, lng_ref, lnb_ref,
                        wrow_ref, brow_ref, wout_ref, o_ref, vout_ref, u_scr, v_scr):
    vn = _gmlp_uv(x_ref, sh_ref, sc_ref, g_ref, wuv_ref, lng_ref, lnb_ref, u_scr, v_scr)
    vout_ref[...] = vn
    s = vn * wrow_ref[...] + brow_ref[...]
    y = _dot((u_scr[...] * s).astype(BF16), wout_ref[...])
    o_ref[...] = x_ref[...] + gt_ref[0] * y


def _gmlp(tok, x, shift, scale, gate, g, wuv, ln_g, ln_b, w_sp, b_sp, wout):
    tm, nt = tok.tiles(512)
    common = [tok.x_spec(tm), tok.mod_spec(tm), tok.mod_spec(tm), tok.mod_spec(tm),
              _full_spec((1, D_MODEL)), _full_spec((D_MODEL, 2 * D_V)),
              _full_spec((1, D_V)), _full_spec((1, D_V))]
    args = [x, tok.mod_arr(shift), tok.mod_arr(scale), tok.mod_arr(gate), g.reshape(1, D_MODEL), wuv,
            ln_g.reshape(1, D_V), ln_b.reshape(1, D_V)]
    x_shape = jax.ShapeDtypeStruct((tok.n_tok, D_MODEL), F32)
    if tok.per_row:
        wrow = jnp.repeat(w_sp[:, 0, 0], CG).reshape(1, D_V)
        brow = jnp.repeat(b_sp[:, 0], CG).reshape(1, D_V)
        return pl.pallas_call(
            _gmlp_sample_kernel,
            grid=(nt,),
            in_specs=common + [_full_spec((1, D_V)), _full_spec((1, D_V)), _full_spec((D_V, D_MODEL))],
            out_specs=[tok.x_spec(tm), tok.x_spec(tm, D_V)],
            out_shape=[x_shape, jax.ShapeDtypeStruct((tok.n_tok, D_V), F32)],
            scratch_shapes=[pltpu.VMEM((tm, D_V), F32), pltpu.VMEM((tm, D_V), F32)],
            compiler_params=_params(1),
        )(*args, wrow, brow, wout)
    bias = jnp.repeat(b_sp.T, CG, axis=1)
    out = pl.pallas_call(
        _gmlp_prompt_kernel,
        grid=(nt,),
        in_specs=common + [_full_spec((N_GROUPS_A, CHUNK, CHUNK)), _full_spec((CHUNK, D_V)),
                           _full_spec((D_V, D_MODEL))],
        out_specs=tok.x_spec(tm),
        out_shape=x_shape,
        scratch_shapes=[pltpu.VMEM((tm, D_V), F32), pltpu.VMEM((tm, D_V), F32),
                        pltpu.VMEM((tm, D_V), BF16)],
        compiler_params=_params(1),
    )(*args, w_sp, bias, wout)
    return out, None


def _kv_prompt_kernel(x_ref, sh_ref, sc_ref, g_ref, w_ref, wvt_ref, ak_ref, kv_ref, k_ref, vt_ref):
    hb = _ada_norm(x_ref[...], g_ref[...], sh_ref[0], sc_ref[0]).astype(BF16)
    kv = _dot(hb, w_ref[...])
    kv_ref[...] = kv
    vt = _dot_nt(wvt_ref[...], hb).astype(BF16)
    tm = x_ref.shape[0]
    gd = N_KV * HEAD_DIM
    ones_rows = jnp.where(lax.broadcasted_iota(jnp.int32, (VROWS - HEAD_DIM, LANE), 0) == 0, 1.0, 0.0).astype(BF16)
    for c in range(tm // LANE):
        rs = slice(c * LANE, (c + 1) * LANE)
        for hd in range(2 * N_KV):
            vt_ref[0, c, hd * VROWS:hd * VROWS + HEAD_DIM, :] = vt[hd * HEAD_DIM:(hd + 1) * HEAD_DIM, rs]
            vt_ref[0, c, hd * VROWS + HEAD_DIM:(hd + 1) * VROWS, :] = ones_rows
        for j, slot in enumerate((2, 4)):
            for grp in range(N_KV):
                col = slot * gd + grp * HEAD_DIM
                k_ref[0, j * N_KV + grp, c, :, 0:HEAD_DIM] = kv[rs, col:col + HEAD_DIM].astype(BF16)
                k_ref[0, j * N_KV + grp, c, :, HEAD_DIM:LANE] = ak_ref[...]


def _kv_sample_kernel(x_ref, sh_ref, sc_ref, g_ref, w_ref, kv_ref):
    hb = _ada_norm(x_ref[...], g_ref[...], sh_ref[0], sc_ref[0]).astype(BF16)
    kv_ref[...] = _dot(hb, w_ref[...])


def _kv_proj(tok, x, shift, scale, g, w, wvt, alibi_k):
    tm, nt = tok.tiles(512)
    n_kv = N_KV_PROJ * N_KV * HEAD_DIM
    in_specs = [tok.x_spec(tm), tok.mod_spec(tm), tok.mod_spec(tm), _full_spec((1, D_MODEL)),
                _full_spec((D_MODEL, n_kv))]
    args = [x, tok.mod_arr(shift), tok.mod_arr(scale), g.reshape(1, D_MODEL), w]
    kv_shape = jax.ShapeDtypeStruct((tok.n_tok, n_kv), F32)
    if tok.per_row:
        return pl.pallas_call(
            _kv_sample_kernel, grid=(nt,), in_specs=in_specs, out_specs=tok.x_spec(tm, n_kv),
            out_shape=kv_shape, compiler_params=_params(1))(*args)
    per_seq = tok.seq_len // tm
    n_qt = tok.seq_len // LANE
    sub = tm // LANE
    return pl.pallas_call(
        _kv_prompt_kernel,
        grid=(nt,),
        in_specs=in_specs + [_full_spec((2 * N_KV * HEAD_DIM, D_MODEL)), _full_spec((LANE, HEAD_DIM))],
        out_specs=[tok.x_spec(tm, n_kv),
                   pl.BlockSpec((1, 2 * N_KV, sub, LANE, LANE),
                                lambda i: (i // per_seq, 0, i % per_seq, 0, 0)),
                   pl.BlockSpec((1, sub, 2 * N_KV * VROWS, LANE),
                                lambda i: (i // per_seq, i % per_seq, 0, 0))],
        out_shape=[kv_shape,
                   jax.ShapeDtypeStruct((tok.n_seq, 2 * N_KV, n_qt, LANE, LANE), BF16),
                   jax.ShapeDtypeStruct((tok.n_seq, n_qt, 2 * N_KV * VROWS, LANE), BF16)],
        compiler_params=_params(1),
    )(*args, wvt, alibi_k)


def _compress_kernel(tbl_ref, *refs):
    pages = refs[:PAGES_PER_SEG + 1]
    perm_ref, w1_ref, w2_ref, w2t_ref, pe_ref, akc_ref, kc_ref, vct_ref, x_scr, hb_scr = refs[PAGES_PER_SEG + 1:]
    del tbl_ref
    perm = perm_ref[...]
    n_heads = 2 * N_KV
    half = CMP_STRIDE * HEAD_DIM
    for p in range(PAGES_PER_SEG + 1):
        xp = _dot(perm, pages[p][0].astype(BF16))
        for hd in range(n_heads):
            cs = slice(hd * HEAD_DIM, (hd + 1) * HEAD_DIM)
            x_scr[hd, p * 8:(p + 1) * 8, :] = jnp.concatenate(
                [xp[s * 8:(s + 1) * 8, cs] for s in range(CMP_STRIDE)], axis=1)
    n_blk = PAGES_PER_SEG * 8

    def one_slot(slot):
        x = x_scr[slot * N_KV:(slot + 1) * N_KV].reshape(N_KV * XROWS, half).astype(BF16)
        ha = _dot(x, w1_ref[slot, 0:half, :])
        yield
        hb_scr[slot] = _dot(x, w1_ref[slot, half:2 * half, :])
        pe_hid = _dot(pe_ref[slot], w1_ref[slot])[0:1]
        yield
        for grp in range(N_KV):
            hid = ha[grp * XROWS:grp * XROWS + n_blk] + hb_scr[slot, pl.ds(grp * XROWS + 1, n_blk), :] + pe_hid
            act = jax.nn.gelu(hid).astype(BF16)
            if slot == 0:
                kc_ref[0, grp, :, 0:HEAD_DIM] = _dot(act, w2_ref[0]).astype(BF16)
                kc_ref[0, grp, :, HEAD_DIM:LANE] = akc_ref[...]
            else:
                vct_ref[0, grp, 0:HEAD_DIM, :] = _dot_nt(w2t_ref[1], act).astype(BF16)
                vct_ref[0, grp, HEAD_DIM:VROWS, :] = jnp.where(
                    lax.broadcasted_iota(jnp.int32, (VROWS - HEAD_DIM, n_blk), 0) == 0, 1.0, 0.0).astype(BF16)
            yield

    _interleave(one_slot(0), one_slot(1))


def _compress(src, table, w1, w2, w2t, pe, alibi_kc, n_seq, n_seg):
    n_blk = PAGES_PER_SEG * 8
    gd2 = 2 * N_KV * HEAD_DIM
    dst = jnp.arange(PAGE)
    src_row = (dst % 8) * CMP_STRIDE + dst // 8
    perm = (jnp.arange(PAGE)[None, :] == src_row[:, None]).astype(BF16)

    def page_spec(p):
        return pl.BlockSpec((1, PAGE, gd2), lambda i, tbl, p=p: (tbl[i, p], 0, 0))

    grid_spec = pltpu.PrefetchScalarGridSpec(
        num_scalar_prefetch=1,
        grid=(n_seq * n_seg,),
        in_specs=[page_spec(p) for p in range(PAGES_PER_SEG + 1)] + [
            pl.BlockSpec((PAGE, PAGE), lambda i, tbl: (0, 0)),
            pl.BlockSpec(w1.shape, lambda i, tbl: (0, 0, 0)),
            pl.BlockSpec(w2.shape, lambda i, tbl: (0, 0, 0)),
            pl.BlockSpec(w2t.shape, lambda i, tbl: (0, 0, 0)),
            pl.BlockSpec(pe.shape, lambda i, tbl: (0, 0, 0)),
            pl.BlockSpec((LANE, HEAD_DIM), lambda i, tbl: (0, 0))],
        out_specs=[pl.BlockSpec((1, N_KV, n_blk, LANE), lambda i, tbl: (i // n_seg, 0, i % n_seg, 0)),
                   pl.BlockSpec((1, N_KV, VROWS, n_blk), lambda i, tbl: (i // n_seg, 0, 0, i % n_seg))],
        scratch_shapes=[pltpu.VMEM((2 * N_KV, XROWS, CMP_STRIDE * HEAD_DIM), F32),
                        pltpu.VMEM((2, N_KV * XROWS, CMP_HID), F32)])
    return pl.pallas_call(
        _compress_kernel,
        grid_spec=grid_spec,
        out_shape=[jax.ShapeDtypeStruct((n_seq, N_KV, n_seg * n_blk, LANE), BF16),
                   jax.ShapeDtypeStruct((n_seq, N_KV, VROWS, n_seg * n_blk), BF16)],
        compiler_params=_params(1),
    )(table, *([src] * (PAGES_PER_SEG + 1)), perm, w1, w2, w2t, pe, alibi_kc)


def _qg_prompt_kernel(x_ref, sh_ref, sc_ref, g_ref, wq_ref, wgt_ref, aq_ref, q_ref, gt_ref):
    hb = _ada_norm(x_ref[...], g_ref[...], sh_ref[0], sc_ref[0]).astype(BF16)
    q = _dot(hb, wq_ref[...]) * (HEAD_DIM ** -0.5 * LOG2E)
    tm = x_ref.shape[0]
    for h in range(N_HEADS):
        q_ref[0, h, :, 0:HEAD_DIM] = q[:, h * HEAD_DIM:(h + 1) * HEAD_DIM].astype(BF16)
        for c in range(tm // LANE):
            q_ref[0, h, c * LANE:(c + 1) * LANE, HEAD_DIM:LANE] = aq_ref[h]
    gates = jax.nn.sigmoid(_dot_nt(wgt_ref[...], hb))
    for grp in range(N_KV):
        gt_ref[0, grp] = gates[grp * GATE_ROWS:(grp + 1) * GATE_ROWS, :]


def _qg_sample_kernel(x_ref, sh_ref, sc_ref, g_ref, wq_ref, wg_ref, q_ref, gt_ref):
    hb = _ada_norm(x_ref[...], g_ref[...], sh_ref[0], sc_ref[0]).astype(BF16)
    q_ref[...] = _dot(hb, wq_ref[...]) * (HEAD_DIM ** -0.5)
    gt_ref[...] = jax.nn.sigmoid(_dot(hb, wg_ref[...]))


def _qg_proj(tok, x, shift, scale, g, wq, wg_t, wg_nat, alibi_q):
    tm, nt = tok.tiles(512)
    in_specs = [tok.x_spec(tm), tok.mod_spec(tm), tok.mod_spec(tm), _full_spec((1, D_MODEL)),
                _full_spec((D_MODEL, D_MODEL))]
    args = [x, tok.mod_arr(shift), tok.mod_arr(scale), g.reshape(1, D_MODEL), wq]
    if tok.per_row:
        return pl.pallas_call(
            _qg_sample_kernel, grid=(nt,),
            in_specs=in_specs + [_full_spec((D_MODEL, LANE))],
            out_specs=[tok.x_spec(tm), tok.x_spec(tm, LANE)],
            out_shape=[jax.ShapeDtypeStruct((tok.n_tok, D_MODEL), F32),
                       jax.ShapeDtypeStruct((tok.n_tok, LANE), F32)],
            compiler_params=_params(1))(*args, wg_nat)
    per_seq = tok.seq_len // tm
    return pl.pallas_call(
        _qg_prompt_kernel, grid=(nt,),
        in_specs=in_specs + [_full_spec((N_KV * GATE_ROWS, D_MODEL)), _full_spec((N_HEADS, LANE, HEAD_DIM))],
        out_specs=[pl.BlockSpec((1, N_HEADS, tm, LANE), lambda i: (i // per_seq, 0, i % per_seq, 0)),
                   pl.BlockSpec((1, N_KV, GATE_ROWS, tm), lambda i: (i // per_seq, 0, 0, i % per_seq))],
        out_shape=[jax.ShapeDtypeStruct((tok.n_seq, N_HEADS, tok.seq_len, LANE), BF16),
                   jax.ShapeDtypeStruct((tok.n_seq, N_KV, GATE_ROWS, tok.seq_len), F32)],
        compiler_params=_params(1))(*args, wg_t, alibi_q)


def _select_blocks(work, idx_f, axis, rounds):
    for _ in range(rounds):
        mx = jnp.max(work, axis=axis, keepdims=True)
        first = jnp.min(jnp.where(work == mx, idx_f, 1e9), axis=axis, keepdims=True)
        first = jnp.where(mx > NEG / 2, first, -1.0)
        work = jnp.where(idx_f == first, PICKED, work)
        yield
    return jnp.where(work == PICKED, 1.0, 0.0)


def _softmax_group(scores, offsets, m_old, exp_dtype=F32, values=None):
    m_new = m_old
    for s, off in zip(scores, offsets):
        m_new = jnp.maximum(m_new, jnp.max(s, axis=0, keepdims=True) - off)
        yield
    alpha = jnp.exp2(m_old - m_new)
    probs = []
    acc = None
    for k, (s, off) in enumerate(zip(scores, offsets)):
        probs.append(jnp.exp2((s - (m_new + off)).astype(exp_dtype)))
        if values is not None and (k % 2 == 1 or k == len(scores) - 1):
            n = 2 if k % 2 == 1 else 1
            part = _dot(jnp.concatenate(values[k + 1 - n:k + 1], axis=1), jnp.concatenate(probs[-n:], axis=0))
            acc = part if acc is None else acc + part
        yield
    return m_new, alpha, (probs if values is None else acc)


def _run(gen):
    try:
        while True:
            next(gen)
    except StopIteration as stop:
        return stop.value


def _interleave(*gens, steps=None):
    live = [(g, 1 if steps is None else steps[k]) for k, g in enumerate(gens)]
    while live:
        for item in list(live):
            try:
                for _ in range(item[1]):
                    next(item[0])
            except StopIteration:
                live.remove(item)


def _normalise(acc):
    l = acc[HEAD_DIM:HEAD_DIM + 1]
    return acc[0:HEAD_DIM] * jnp.where(l > 0.0, 1.0 / l, 0.0)


def _nsa_prompt_kernel(q_ref, g_ref, kc_ref, vct_ref, ks_ref, vst_ref, kw_ref, vwt_ref, cov_ref, hot_ref, sl_ref,
                       o_ref, qp_scr, oc_scr, m_scr, acc_scr, bits_ref, *, ncp, n_qt):
    j = pl.program_id(2)
    cols = HPG * LANE
    key_io = lax.broadcasted_iota(jnp.int32, (LANE, LANE), 0)
    tok_io = lax.broadcasted_iota(jnp.int32, (LANE, LANE), 1)
    sl = sl_ref[0]
    m_init = jnp.full((1, cols), M_FLOOR, F32)

    def tile4(a):
        return jnp.concatenate([a] * HPG, axis=1)

    @pl.when(j == 0)
    def _():
        qp_scr[1] = jnp.zeros(qp_scr.shape[1:], BF16)
        oc_scr[1] = jnp.zeros(oc_scr.shape[1:], F32)
        for w in range(LANE // 32):
            bits_ref[LANE // 32 + w] = 0

    i = jnp.maximum(j - 1, 0)
    slot = (j + 1) % 2

    def sel_keys(jp):
        return jnp.concatenate([hot_ref[jp], ks_ref[0, 0, jp]], axis=1)

    fi_a = i.astype(F32)
    qp = qp_scr[slot]
    rel = tile4((tok_io - key_io).astype(F32))

    def window_branch():
        qx = qp[:, LANE:2 * LANE]

        n_wt = WINDOW // LANE
        scores, offsets, vts = [], [], []
        tiles = [jnp.maximum(i - n_wt + kt_i, 0) for kt_i in range(n_wt + 1)]
        s_all = _dot_nt(jnp.concatenate([kw_ref[0, 0, t] for t in tiles], axis=0), qx)
        yield
        for kt_i in range(n_wt + 1):
            jt = i - n_wt + kt_i
            jt_c = tiles[kt_i]
            s = s_all[kt_i * LANE:(kt_i + 1) * LANE]
            if kt_i == 0:
                s = jnp.where(rel <= 0.0, s, NEG)
            elif kt_i == n_wt:
                s = jnp.where(rel >= 0.0, s, NEG)
            scores.append(s)
            offsets.append(sl * float(LANE * (n_wt - kt_i)) + jnp.where(jt >= 0, 0.0, MASK_OFF).astype(F32))
            vts.append(vwt_ref[0, jt_c])
        _, _, acc = yield from _softmax_group(scores, offsets, m_init, BF16, vts)
        return _normalise(acc)

    def near_tiles():
        fi = fi_a
        tiles = [0, i] + [jnp.maximum(i - r, 0) for r in range(1, N_NEAR + 1)]
        s_all = _dot_nt(jnp.concatenate([sel_keys(t) for t in tiles], axis=0), qp)
        yield
        scores = [s_all[k * LANE:(k + 1) * LANE] for k in range(len(tiles))]
        scores[1] = jnp.where(rel >= 0.0, scores[1], NEG)
        offsets = [sl * (LANE * fi) + jnp.where(i > N_NEAR, 0.0, MASK_OFF).astype(F32), jnp.zeros((1, cols), F32)]
        for r in range(1, N_NEAR + 1):
            offsets.append(sl * float(LANE * r) + jnp.where(i - r >= 0, 0.0, MASK_OFF).astype(F32))
        vts = [vst_ref[0, t] for t in tiles]
        m, _, acc = yield from _softmax_group(scores, offsets, m_init, BF16, vts)
        m_scr[...] = m
        acc_scr[...] = acc

    def selection_half():
        i_s = jnp.minimum(j, n_qt - 1)
        slot_s = j % 2
        fi = i_s.astype(F32)
        qx = q_ref[0].reshape(cols, LANE)

        n_chunks = ncp // LANE
        rel_c = tile4((tok_io - CMP_STRIDE * key_io).astype(F32))
        scores, offsets = [], []
        s_all = _dot_nt(kc_ref[0, 0], qx)
        yield
        for c in range(n_chunks):
            base = float(CMP_STRIDE * LANE * c + CMP_LEN - 1)
            s_c = s_all[c * LANE:(c + 1) * LANE]
            scores.append(jnp.where(rel_c >= base - LANE * fi, s_c, NEG))
            offsets.append(sl * (LANE * fi - base))
        _, _, probs = yield from _softmax_group(scores, offsets, m_init)
        p_hi, p_lo = _split_bf16(jnp.concatenate(probs, axis=0))
        acc_c = _dot(vct_ref[0, 0], p_hi)
        oc_scr[slot_s] = _normalise(acc_c)
        yield
        l_c = acc_c[HEAD_DIM:HEAD_DIM + 1]
        cov = cov_ref[...]
        imp4 = (_dot(cov, p_hi) + _dot(cov, p_lo)) * jnp.where(l_c > 0.0, 1.0 / l_c, 0.0)
        imp = imp4[:, 0:LANE]
        for h in range(1, HPG):
            imp = imp + imp4[:, h * LANE:(h + 1) * LANE]
        yield

        q_blk = 2 * i_s + tok_io // SEL_BLOCK
        forced = (key_io == 0) | (key_io == q_blk) | (key_io == q_blk - 1)
        valid = key_io <= q_blk
        work = jnp.where(valid & jnp.logical_not(forced), imp, NEG)
        sel = yield from _select_blocks(work, key_io.astype(F32), 0, N_SEL - 3)
        sel = jnp.where(valid & forced, 1.0, sel)
        sel_bias_t = jnp.where(sel.T > 0.5, 0.0, NEG).astype(BF16)
        for h in range(HPG):
            qp_scr[slot_s, h * LANE:(h + 1) * LANE, 0:LANE] = sel_bias_t
            qp_scr[slot_s, h * LANE:(h + 1) * LANE, LANE:2 * LANE] = q_ref[0, h]

        cnt = _dot_nt(jnp.ones((8, LANE), BF16), sel.astype(BF16))[0:1]
        lane = lax.broadcasted_iota(jnp.int32, (1, LANE), 1)
        live = (cnt > 0.0) & (lane >= 2) & (lane < 2 * (i_s - N_NEAR))
        for w in range(LANE // 32):
            bits_ref[slot_s * (LANE // 32) + w] = jnp.sum(
                jnp.where(live & (lane // 32 == w), jnp.left_shift(1, lane % 32), 0))

    box = []

    def window_result():
        box.append((yield from window_branch()))

    _interleave(selection_half(), window_result(), near_tiles(), steps=(2, 1, 1))
    o_win = box[0]

    def group_body(k, carry):
        word = bits_ref[slot * (LANE // 32) + k // 4]
        used = jnp.right_shift(word, (8 * k) % 32) & 255

        @pl.when(used != 0)
        def _():
            offsets, vts = [], []
            s_all = _dot_nt(jnp.concatenate([sel_keys(4 * k + r) for r in range(4)], axis=0), qp_scr[slot])
            scores = [s_all[r * LANE:(r + 1) * LANE] for r in range(4)]
            for r in range(4):
                jp = 4 * k + r
                dead = jnp.where((jp == 0) | (jp >= i - N_NEAR), MASK_OFF, 0.0).astype(F32)
                offsets.append(sl * (LANE * (i - jp)).astype(F32) + dead)
                vts.append(vst_ref[0, jp])
            m, alpha, acc = _run(_softmax_group(scores, offsets, m_scr[...], BF16, vts))
            m_scr[...] = m
            acc_scr[...] = alpha * acc_scr[...] + acc
        return carry

    lax.fori_loop(0, jnp.maximum(i - N_NEAR + 3, 0) // 4, group_body, 0)
    o_sel = _normalise(acc_scr[...])

    o_cmp = oc_scr[slot]
    for h in range(HPG):
        cs = slice(h * LANE, (h + 1) * LANE)
        g_c = g_ref[0, 0, 3 * h:3 * h + 1, :]
        g_s = g_ref[0, 0, 3 * h + 1:3 * h + 2, :]
        g_w = g_ref[0, 0, 3 * h + 2:3 * h + 3, :]
        o_ref[0, h * HEAD_DIM:(h + 1) * HEAD_DIM, :] = g_c * o_cmp[:, cs] + g_s * o_sel[:, cs] + g_w * o_win[:, cs]


def _nsa_prompt(q, gates_t, kc, vct, k_ext, v_t, cov_t, onehot, sl_rows, n_seq, seq_len):
    n_qt = seq_len // LANE
    ncp = kc.shape[2]
    assert n_qt % 4 == 0
    kern = functools.partial(_nsa_prompt_kernel, ncp=ncp, n_qt=n_qt)
    cols = HPG * LANE
    last = n_qt - 1

    def sel_tile(j):
        return jnp.minimum(j, last)

    def att_tile(j):
        return jnp.maximum(j - 1, 0)

    return pl.pallas_call(
        kern,
        grid=(n_seq, N_KV, n_qt + 1),
        in_specs=[pl.BlockSpec((1, HPG, LANE, LANE), lambda b, g, j: (b, g, sel_tile(j), 0)),
                  pl.BlockSpec((1, 1, GATE_ROWS, LANE), lambda b, g, j: (b, g, 0, att_tile(j))),
                  pl.BlockSpec((1, 1, ncp, LANE), lambda b, g, j: (b, g, 0, 0)),
                  pl.BlockSpec((1, 1, VROWS, ncp), lambda b, g, j: (b, g, 0, 0)),
                  pl.BlockSpec((1, 1, n_qt, LANE, LANE), lambda b, g, j: (b, g, 0, 0, 0)),
                  pl.BlockSpec((1, n_qt, VROWS, LANE), lambda b, g, j: (b, 0, g, 0)),
                  pl.BlockSpec((1, 1, n_qt, LANE, LANE), lambda b, g, j: (b, N_KV + g, 0, 0, 0)),
                  pl.BlockSpec((1, n_qt, VROWS, LANE), lambda b, g, j: (b, 0, N_KV + g, 0)),
                  pl.BlockSpec((LANE, ncp), lambda b, g, j: (0, 0)),
                  pl.BlockSpec((n_qt, LANE, LANE), lambda b, g, j: (0, 0, 0)),
                  pl.BlockSpec((1, 1, cols), lambda b, g, j: (g, 0, 0))],
        out_specs=pl.BlockSpec((1, HPG * HEAD_DIM, LANE), lambda b, g, j: (b, g, att_tile(j))),
        out_shape=jax.ShapeDtypeStruct((n_seq, N_HEADS * HEAD_DIM, seq_len), F32),
        scratch_shapes=[pltpu.VMEM((2, cols, 2 * LANE), BF16),
                        pltpu.VMEM((2, HEAD_DIM, cols), F32),
                        pltpu.VMEM((1, cols), F32),
                        pltpu.VMEM((VROWS, cols), F32),
                        pltpu.SMEM((2 * (LANE // 32),), jnp.int32)],
        compiler_params=_params(3),
    )(q, gates_t, kc, vct, k_ext, v_t, k_ext, v_t, cov_t, onehot, sl_rows)


def _nsa_sample_kernel(tbl_ref, *refs, n_pages, n_e):
    del tbl_ref
    shared = refs[n_e * n_pages:]
    _interleave(*[_sample_sequence(e, refs[e * n_pages:(e + 1) * n_pages], *shared) for e in range(n_e)])


def _sample_sequence(e, pages, q_ref, g_ref, kc_ref, vct_ref, win_ref, new_ref, cov_ref, exp_ref, slope_ref,
                     o_ref, s_scr):
    n_pages = len(pages)
    gd = N_KV * HEAD_DIM
    past = n_pages * PAGE
    q_pos = float(past)
    qb = q_ref[e].astype(BF16)
    slope = slope_ref[...]
    row_grp = lax.broadcasted_iota(jnp.int32, (N_HEADS, LANE), 0) // HPG
    lane_f = lax.broadcasted_iota(jnp.int32, (N_HEADS, LANE), 1).astype(F32)
    new = new_ref[e]

    def own_rows(g_sel, pick):
        out = pick(0)
        for grp in range(1, N_KV):
            out = jnp.where(g_sel == grp, pick(grp), out)
        return out

    q_f = q_ref[e]
    row_grp_o = lax.broadcasted_iota(jnp.int32, (N_HEADS, HEAD_DIM), 0) // HPG
    q_pair = [jnp.concatenate([jnp.where(row_grp_o == 2 * pr, q_f, 0.0), jnp.where(row_grp_o == 2 * pr + 1, q_f, 0.0)],
                              axis=1).astype(BF16) for pr in range(N_KV // 2)]
    first_row2 = lax.broadcasted_iota(jnp.int32, (LANE, 2 * HEAD_DIM), 0) == 0

    def pair_scores(tile):
        return _dot_nt(q_pair[0], tile(0)) + _dot_nt(q_pair[1], tile(1))

    def own_halves(o_pairs):
        out = o_pairs[0][:, 0:HEAD_DIM]
        for grp in range(1, N_KV):
            half = o_pairs[grp // 2][:, (grp % 2) * HEAD_DIM:(grp % 2 + 1) * HEAD_DIM]
            out = jnp.where(row_grp_o == grp, half, out)
        return out

    def new_slab(slot, pr):
        col = slot * gd + pr * 2 * HEAD_DIM
        return jnp.where(first_row2, new[:, col:col + 2 * HEAD_DIM], 0.0).astype(BF16)

    def softmax_rows(s, valid):
        s = jnp.where(valid, s, NEG)
        m = jnp.max(s, axis=1, keepdims=True)
        p = jnp.where(valid, jnp.exp(s - m), 0.0)
        l = jnp.sum(p, axis=1, keepdims=True)
        return p * jnp.where(l > 0.0, 1.0 / l, 0.0)

    d_c = q_pos - (CMP_STRIDE * lane_f + (CMP_LEN - 1))
    s_c = own_rows(row_grp, lambda grp: _dot_nt(qb, kc_ref[e, grp][:, 0:HEAD_DIM])) - slope * d_c
    yield
    p_c = softmax_rows(s_c, d_c >= 0.0)
    p_cb = p_c.astype(BF16)
    o_c = own_rows(row_grp_o, lambda grp: _dot_nt(p_cb, vct_ref[e, grp, 0:HEAD_DIM, :]))
    yield

    p_grp = own_rows(row_grp, lambda grp: jnp.broadcast_to(
        jnp.sum(p_c[grp * HPG:(grp + 1) * HPG], axis=0, keepdims=True), (N_HEADS, LANE)))
    p_hi, p_lo = _split_bf16(jnp.concatenate([p_grp, jnp.zeros((LANE - N_HEADS, LANE), F32)], axis=0))
    cov_t = cov_ref[...]
    imp_t = _dot_nt(cov_t, p_hi) + _dot_nt(cov_t, p_lo)
    blk = lax.broadcasted_iota(jnp.int32, (LANE, LANE), 0)
    q_blk = past // SEL_BLOCK
    forced = (blk == 0) | (blk == q_blk) | (blk == q_blk - 1)
    valid = blk <= q_blk
    work = jnp.where(valid & jnp.logical_not(forced), imp_t, NEG)
    yield
    sel_t = yield from _select_blocks(work, blk.astype(F32), 0, N_SEL - 3)
    sel = jnp.where(valid & forced, 1.0, sel_t).T[0:N_HEADS]

    n_t = n_pages + 1
    for t in range(n_t):
        if t < n_pages:
            tile = lambda pr, t=t: pages[t][0, :, pr * LANE:(pr + 1) * LANE].astype(BF16)
        else:
            tile = lambda pr: new_slab(2, pr)
        s_scr[e, :, t * LANE:(t + 1) * LANE] = pair_scores(tile)
        if t % 4 == 3:
            yield
    width = n_t * LANE
    kpos = lax.broadcasted_iota(jnp.int32, (N_HEADS, width), 1).astype(F32)
    d_s = q_pos - kpos
    chosen = _dot(sel.astype(BF16), exp_ref[...])
    slope_w = jnp.concatenate([slope] * n_t, axis=1)
    p_s = softmax_rows(s_scr[e] - slope_w * d_s, (chosen > 0.5) & (d_s >= 0.0)).astype(BF16)
    yield
    o_sp = [jnp.zeros((N_HEADS, 2 * HEAD_DIM), F32) for _ in range(N_KV // 2)]
    for t in range(n_t):
        pt = p_s[:, t * LANE:(t + 1) * LANE]
        if t < n_pages:
            vtile = lambda pr, t=t: pages[t][0, :, gd + pr * LANE: gd + (pr + 1) * LANE].astype(BF16)
        else:
            vtile = lambda pr: new_slab(3, pr)
        o_sp = [o_sp[pr] + _dot(pt, vtile(pr)) for pr in range(N_KV // 2)]
        if t % 4 == 3:
            yield
    o_s = own_halves(o_sp)

    buf = win_ref.shape[1]
    n_w = buf // LANE + 1
    w_parts = []
    for t in range(n_w):
        if t < n_w - 1:
            tile = lambda pr, t=t: win_ref[e, t * LANE:(t + 1) * LANE, pr * LANE:(pr + 1) * LANE].astype(BF16)
        else:
            tile = lambda pr: new_slab(4, pr)
        w_parts.append(pair_scores(tile))
    s_w = jnp.concatenate(w_parts, axis=1)
    yield
    wpos = lax.broadcasted_iota(jnp.int32, (N_HEADS, n_w * LANE), 1).astype(F32) + float(past - buf)
    d_w = q_pos - wpos
    slope_ww = jnp.concatenate([slope] * n_w, axis=1)
    p_w = softmax_rows(s_w - slope_ww * d_w, (d_w >= 0.0) & (d_w <= float(WINDOW))).astype(BF16)
    yield
    o_wp = [jnp.zeros((N_HEADS, 2 * HEAD_DIM), F32) for _ in range(N_KV // 2)]
    for t in range(n_w):
        pt = p_w[:, t * LANE:(t + 1) * LANE]
        if t < n_w - 1:
            vtile = lambda pr, t=t: win_ref[e, t * LANE:(t + 1) * LANE,
                                            gd + pr * LANE: gd + (pr + 1) * LANE].astype(BF16)
        else:
            vtile = lambda pr: new_slab(5, pr)
        o_wp = [o_wp[pr] + _dot(pt, vtile(pr)) for pr in range(N_KV // 2)]
    o_w = own_halves(o_wp)

    gt = g_ref[e]
    o_ref[e] = gt[:, 0:1] * o_c + gt[:, 1:2] * o_s + gt[:, 2:3] * o_w


def _nsa_sample(q, gates, kc, vct, cache_pages, page_table, win, kv_new, cov, expand, slope16):
    n_seq, n_pages = page_table.shape
    gd = N_KV * HEAD_DIM
    buf = win.shape[1]
    n_e = SAMPLE_SEQS if n_seq % SAMPLE_SEQS == 0 else 1
    kern = functools.partial(_nsa_sample_kernel, n_pages=n_pages, n_e=n_e)

    def page_spec(e, p):
        return pl.BlockSpec((1, PAGE, 2 * gd), lambda b, tbl, e=e, p=p: (tbl[b * n_e + e, p], 0, 1))

    def const_spec(shape):
        nd = len(shape)
        return pl.BlockSpec(shape, lambda b, tbl: (0,) * nd)

    grid_spec = pltpu.PrefetchScalarGridSpec(
        num_scalar_prefetch=1,
        grid=(n_seq // n_e,),
        in_specs=[page_spec(e, p) for e in range(n_e) for p in range(n_pages)] + [
            pl.BlockSpec((n_e, N_HEADS, HEAD_DIM), lambda b, tbl: (b, 0, 0)),
            pl.BlockSpec((n_e, N_HEADS, 3), lambda b, tbl: (b, 0, 0)),
            pl.BlockSpec((n_e, N_KV, LANE, LANE), lambda b, tbl: (b, 0, 0, 0)),
            pl.BlockSpec((n_e, N_KV, VROWS, LANE), lambda b, tbl: (b, 0, 0, 0)),
            pl.BlockSpec((n_e, buf, 2 * gd), lambda b, tbl: (b, 0, 0)),
            pl.BlockSpec((n_e, 1, N_KV_PROJ * gd), lambda b, tbl: (b, 0, 0)),
            const_spec(cov.shape), const_spec(expand.shape), const_spec(slope16.shape)],
        out_specs=pl.BlockSpec((n_e, N_HEADS, HEAD_DIM), lambda b, tbl: (b, 0, 0)),
        scratch_shapes=[pltpu.VMEM((n_e, N_HEADS, (n_pages + 1) * LANE), F32)])
    return pl.pallas_call(
        kern,
        grid_spec=grid_spec,
        out_shape=jax.ShapeDtypeStruct((n_seq, N_HEADS, HEAD_DIM), F32),
        compiler_params=_params(1),
    )(page_table, *([cache_pages] * (n_e * n_pages)), q, gates, kc, vct, win, kv_new, cov, expand, slope16)


def _oproj_prompt_kernel(ot_ref, x_ref, gt_ref, wo_ref, o_ref):
    o = ot_ref[0].T.astype(BF16)
    o_ref[...] = x_ref[...] + gt_ref[0] * _dot(o, wo_ref[...])


def _oproj_sample_kernel(a_ref, x_ref, gt_ref, wo_ref, o_ref):
    o_ref[...] = x_ref[...] + gt_ref[0] * _dot(a_ref[...].astype(BF16), wo_ref[...])


def _out_proj(tok, attn, x, gate, wo):
    tm, nt = tok.tiles(512)
    if tok.per_row:
        kern, a_spec = _oproj_sample_kernel, tok.x_spec(tm)
    else:
        per_seq = tok.seq_len // tm
        kern = _oproj_prompt_kernel
        a_spec = pl.BlockSpec((1, D_MODEL, tm), lambda i: (i // per_seq, 0, i % per_seq))
    return pl.pallas_call(
        kern, grid=(nt,),
        in_specs=[a_spec, tok.x_spec(tm), tok.mod_spec(tm), _full_spec((D_MODEL, D_MODEL))],
        out_specs=tok.x_spec(tm),
        out_shape=jax.ShapeDtypeStruct((tok.n_tok, D_MODEL), F32),
        compiler_params=_params(1))(attn, x, tok.mod_arr(gate), wo)


def _alibi_slopes():
    h = jnp.arange(1, N_HEADS + 1, dtype=F32)
    return jnp.exp2(-8.0 * h / N_HEADS)


def _cover(n_cmp, n_sel):
    c_start = jnp.arange(n_cmp)[:, None] * CMP_STRIDE
    s_start = jnp.arange(n_sel)[None, :] * SEL_BLOCK
    return ((c_start < s_start + SEL_BLOCK) & (c_start + CMP_LEN > s_start)).astype(BF16)


def _trunk(tok, x, mods, kv_mod, f_mod, wts, ctx):
    depth = wts['norm_g'].shape[0]
    n_a = depth // 2
    v_rows = []
    kv = None
    attn_ctx = None
    for l in range(depth):
        m = mods[l]
        if l == n_a:
            kv_out = _kv_proj(tok, x, kv_mod[:, 0], kv_mod[:, 1], wts['kv_norm_g'], wts['kv_w'], wts['kv_wvt'],
                              wts['alibi_k'])
            kv, attn_ctx = ctx['prepare'](kv_out)
        x = _ffn(tok, x, m[:, 0], m[:, 1], m[:, 2], wts['norm_g'][l, 0],
                 wts['ffn_w_gate'][l, 0], wts['ffn_w_up'][l, 0], wts['ffn_w_down'][l, 0])
        if l < n_a:
            x, v = _gmlp(tok, x, m[:, 3], m[:, 4], m[:, 5], wts['norm_g'][l, 1], wts['gmlp_w_uv'][l],
                         wts['gmlp_ln_g'][l], wts['gmlp_ln_b'][l], wts['gmlp_w_sp'][l], wts['gmlp_b_sp'][l],
                         wts['gmlp_w_out'][l])
            v_rows.append(v)
        else:
            j = l - n_a
            q, gates = _qg_proj(tok, x, m[:, 3], m[:, 4], wts['norm_g'][l, 1], wts['nsa_wq'][j],
                                wts['nsa_wg_t'][j], wts['nsa_wg'][j], wts['alibi_q'])
            attn = ctx['attend'](q, gates, attn_ctx)
            x = _out_proj(tok, attn, x, m[:, 5], wts['nsa_w_o'][j])
        final = (f_mod[:, 0], f_mod[:, 1], wts['final_g']) if l == depth - 1 else None
        x = _ffn(tok, x, m[:, 6], m[:, 7], m[:, 8], wts['norm_g'][l, 2],
                 wts['ffn_w_gate'][l, 1], wts['ffn_w_up'][l, 1], wts['ffn_w_down'][l, 1], final)
    return x, kv, v_rows


def kernel(x_prompt, x_sample, cache_kv, state_win_kv, page_table, c_prompt, c_sample, ada_w, ada_b, norm_g, ffn_w_gate, ffn_w_up, ffn_w_down, gmlp_w_uv, gmlp_ln_g, gmlp_ln_b, gmlp_w_sp, gmlp_b_sp, gmlp_w_out, nsa_w_qg, nsa_w_o, kv_norm_g, kv_ada_w, kv_ada_b, kv_w, cmp_w1, cmp_w2, cmp_pe, final_g, final_ada_w, final_ada_b):
    n_p, seq, _ = x_prompt.shape
    n_s, dec_seq, _ = x_sample.shape
    assert dec_seq == 1 and seq % (PAGES_PER_SEG * PAGE) == 0
    depth = ada_w.shape[0]
    n_b = nsa_w_qg.shape[0]
    gd = N_KV * HEAD_DIM
    n_pages = page_table.shape[1]
    assert n_pages == PAGES_PER_SEG
    past = n_pages * PAGE

    nq = N_HEADS * HEAD_DIM
    wg_cols = nsa_w_qg[:, :, nq:]
    wg_pad = jnp.pad(wg_cols.reshape(n_b, D_MODEL, N_KV, HPG * 3), ((0, 0), (0, 0), (0, 0), (0, GATE_ROWS - HPG * 3)))
    wts = dict(
        norm_g=norm_g, kv_norm_g=kv_norm_g, final_g=final_g,
        ffn_w_gate=ffn_w_gate.astype(BF16), ffn_w_up=ffn_w_up.astype(BF16), ffn_w_down=ffn_w_down.astype(BF16),
        gmlp_w_uv=gmlp_w_uv.astype(BF16), gmlp_ln_g=gmlp_ln_g, gmlp_ln_b=gmlp_ln_b,
        gmlp_w_sp=gmlp_w_sp, gmlp_b_sp=gmlp_b_sp, gmlp_w_out=gmlp_w_out.astype(BF16),
        nsa_wq=nsa_w_qg[:, :, :nq].astype(BF16),
        nsa_wg_t=jnp.swapaxes(wg_pad.reshape(n_b, D_MODEL, N_KV * GATE_ROWS), 1, 2).astype(BF16),
        nsa_wg=jnp.pad(wg_cols, ((0, 0), (0, 0), (0, LANE - N_HEADS * 3))).astype(BF16),
        nsa_w_o=nsa_w_o.astype(BF16),
        kv_w=kv_w.astype(BF16),
        kv_wvt=jnp.concatenate([kv_w[:, 3 * gd:4 * gd], kv_w[:, 5 * gd:6 * gd]], axis=1).T.astype(BF16),
    )
    w1 = cmp_w1.astype(BF16)
    w2 = cmp_w2.astype(BF16)
    w2t = jnp.swapaxes(cmp_w2, 1, 2).astype(BF16)
    pe = jnp.broadcast_to(cmp_pe.astype(BF16).reshape(2, 1, CMP_LEN * HEAD_DIM), (2, 16, CMP_LEN * HEAD_DIM))
    slopes = _alibi_slopes()
    sl2 = slopes * LOG2E
    off_f = jnp.arange(LANE, dtype=F32)

    def split3(x):
        a = x.astype(BF16)
        b = (x - a.astype(F32)).astype(BF16)
        c = (x - a.astype(F32) - b.astype(F32)).astype(BF16)
        return [a, b, c]

    q_cols = split3(-sl2[:, None] * off_f[None, :]) + \
        [jnp.broadcast_to(c[:, None], (N_HEADS, LANE)) for c in split3(sl2)] + \
        [jnp.broadcast_to(c[:, None], (N_HEADS, LANE)) for c in split3(sl2 * CMP_STRIDE)]
    wts['alibi_q'] = jnp.pad(jnp.stack(q_cols, axis=-1), ((0, 0), (0, 0), (0, HEAD_DIM - 9)))
    ones, zeros, offs = jnp.ones((LANE,), BF16), jnp.zeros((LANE,), BF16), off_f.astype(BF16)
    wts['alibi_k'] = jnp.pad(jnp.stack([ones] * 3 + [offs] * 3 + [zeros] * 3, axis=-1), ((0, 0), (0, HEAD_DIM - 9)))
    alibi_kc = jnp.pad(jnp.stack([ones] * 3 + [zeros] * 3 + [offs] * 3, axis=-1), ((0, 0), (0, HEAD_DIM - 9)))

    n_c = n_p + n_s
    c_all = jnp.pad(jnp.concatenate([c_prompt, c_sample], axis=0), ((0, (-n_c) % 8), (0, 0)))
    mod_all = _mod_linear(c_all, ada_w, ada_b)
    kv_mod_all = _mod_linear(c_all, kv_ada_w[None], kv_ada_b[None])[0]
    f_mod_all = _mod_linear(c_all, final_ada_w[None], final_ada_b[None])[0]

    def rows(a, lo, hi, k):
        return a[..., lo:hi, :].reshape(a.shape[:-2] + (hi - lo, k, D_MODEL))

    tok_p = _Tok(n_p, seq, per_row=False)
    n_seg = seq // (PAGES_PER_SEG * PAGE)
    n_qt = seq // LANE
    ncp = n_seg * PAGES_PER_SEG * 8
    n_sel_p = seq // SEL_BLOCK
    assert n_sel_p <= LANE
    cov_t = jnp.pad(_cover(ncp, n_sel_p).T, ((0, LANE - n_sel_p), (0, 0)))
    sl_rows = jnp.repeat(sl2.reshape(N_KV, 1, HPG), LANE, axis=2)
    onehot = (jnp.arange(LANE)[None, None, :] ==
              (2 * jnp.arange(n_qt)[:, None, None] + jnp.arange(LANE)[None, :, None] // SEL_BLOCK)).astype(BF16)

    def prepare_p(kv_out):
        kv, k_nat, v_t = kv_out
        pages_per_seq = seq // PAGE
        base = jnp.arange(n_p * n_seg, dtype=jnp.int32)[:, None] * PAGES_PER_SEG
        table = jnp.minimum(base + jnp.arange(PAGES_PER_SEG + 1, dtype=jnp.int32)[None, :],
                            n_p * pages_per_seq - 1)
        kc, vct = _compress(kv.reshape(n_p * pages_per_seq, PAGE, N_KV_PROJ * gd), table, w1, w2, w2t, pe,
                            alibi_kc, n_p, n_seg)
        return kv, (kc, vct, k_nat, v_t)

    def attend_p(q, gates, c):
        kc, vct, k_nat, v_t = c
        return _nsa_prompt(q, gates, kc, vct, k_nat, v_t, cov_t, onehot, sl_rows, n_p, seq)

    y_p, kv_p, _ = _trunk(tok_p, x_prompt.reshape(n_p * seq, D_MODEL),
                          rows(mod_all, 0, n_p, 9), rows(kv_mod_all, 0, n_p, 2), rows(f_mod_all, 0, n_p, 2),
                          wts, dict(prepare=prepare_p, attend=attend_p))
    kv_p = kv_p.reshape(n_p, seq, N_KV_PROJ * gd)
    n_win = min(WINDOW, seq)
    kv_prompt = kv_p[:, :, :4 * gd].reshape(n_p, seq, 4, N_KV, HEAD_DIM)
    win_prompt = kv_p[:, seq - n_win:, 4 * gd:].reshape(n_p, n_win, 2, N_KV, HEAD_DIM)

    tok_s = _Tok(n_s, 1, per_row=True)
    cache_pages = cache_kv.reshape(cache_kv.shape[0], PAGE, 4 * gd)
    buf = state_win_kv.shape[1]
    win_flat = state_win_kv.reshape(n_s, buf, 2 * gd)
    n_sel_s = (past + 1 + SEL_BLOCK - 1) // SEL_BLOCK
    cov_s = jnp.pad(_cover(LANE, n_sel_s).T, ((0, LANE - n_sel_s), (0, 0)))
    n_keys = (n_pages + 1) * LANE
    expand_s = (jnp.arange(LANE)[:, None] == (jnp.arange(n_keys)[None, :] // SEL_BLOCK)).astype(BF16)
    slope16 = jnp.broadcast_to(slopes[:, None], (N_HEADS, LANE))

    def prepare_s(kv):
        table = jnp.concatenate([page_table, page_table[:, -1:]], axis=1)
        kc, vct = _compress(cache_pages, table, w1, w2, w2t, pe, alibi_kc, n_s, 1)
        return kv, (kc, vct, kv)

    def attend_s(q, gates, c):
        kc, vct, kv = c
        o = _nsa_sample(q.reshape(n_s, N_HEADS, HEAD_DIM), gates[:, :N_HEADS * 3].reshape(n_s, N_HEADS, 3),
                        kc, vct, cache_pages, page_table, win_flat, kv.reshape(n_s, 1, N_KV_PROJ * gd),
                        cov_s, expand_s, slope16)
        return o.reshape(n_s, N_HEADS * HEAD_DIM)

    y_s, kv_s, v_rows = _trunk(tok_s, x_sample.reshape(n_s, D_MODEL),
                               rows(mod_all, n_p, n_c, 9), rows(kv_mod_all, n_p, n_c, 2),
                               rows(f_mod_all, n_p, n_c, 2), wts, dict(prepare=prepare_s, attend=attend_s))
    kv_s = kv_s.reshape(n_s, 1, N_KV_PROJ, N_KV, HEAD_DIM)
    kv_sample = kv_s[:, :, :4]
    win_sample = jnp.concatenate([state_win_kv[:, 1:], kv_s[:, :, 4:6]], axis=1)
    gmlp_v_sample = jnp.stack([v.reshape(n_s, 1, D_V) for v in v_rows])

    return (y_p.reshape(n_p, seq, D_MODEL), y_s.reshape(n_s, 1, D_MODEL), kv_prompt, kv_sample,
            win_prompt, win_sample, gmlp_v_sample)
```

```python
import functools

import jax
import jax.numpy as jnp
from jax import lax
from jax.experimental import pallas as pl
from jax.experimental.pallas import tpu as pltpu

F32 = jnp.float32
BF16 = jnp.bfloat16

D_MODEL = 1024
D_FF = 2816
D_V = 3072
CHUNK = 128
N_GROUPS_A = 8
CG = D_V // N_GROUPS_A
N_HEADS = 16
HEAD_DIM = 64
N_KV = 4
HPG = N_HEADS // N_KV
CMP_LEN = 32
CMP_STRIDE = 16
CMP_HID = 256
SEL_BLOCK = 64
N_SEL = 16
WINDOW = 512
PAGE = 128
N_KV_PROJ = 6
EPS = 1e-6
NEG = -1e30
M_FLOOR = -1e29
PICKED = -3e38
LOG2E = 1.4426950408889634
MASK_OFF = 1e30

LANE = 128
VMEM_LIMIT = 56 * 1024 * 1024
FF_CHUNK = 256
UV_CHUNK = 512
VROWS = HEAD_DIM + 16
N_NEAR = 8
SAMPLE_SEQS = 4
PAGES_PER_SEG = 16
XROWS = PAGES_PER_SEG * 8 + 8


def _dot(a, b):
    return jnp.dot(a, b, preferred_element_type=F32)


def _dot_nt(a, b):
    return lax.dot_general(a, b, (((1,), (1,)), ((), ())), preferred_element_type=F32)


def _split_bf16(x):
    hi = x.astype(BF16)
    lo = (x - hi.astype(F32)).astype(BF16)
    return hi, lo


def _params(n_grid):
    return pltpu.CompilerParams(dimension_semantics=("arbitrary",) * n_grid,
                                vmem_limit_bytes=VMEM_LIMIT)


def _full_spec(shape):
    nd = len(shape)
    return pl.BlockSpec(shape, lambda *_: (0,) * nd, pipeline_mode=pl.Buffered(1))


def _ada_norm(x, g, shift, scale):
    ms = jnp.mean(x * x, axis=-1, keepdims=True)
    h = x * lax.rsqrt(ms + EPS) * g
    return h * (1.0 + scale) + shift


def _mod_kernel(c_ref, w_ref, b_ref, o_ref):
    c = c_ref[...]
    a = c * jax.nn.sigmoid(c)
    w = w_ref[0]
    a_hi, a_lo = _split_bf16(a)
    w_hi, w_lo = _split_bf16(w)
    o_ref[0] = _dot(a_hi, w_hi) + _dot(a_hi, w_lo) + _dot(a_lo, w_hi) + b_ref[0]


def _mod_linear(c, w, b):
    n_l, d, n = w.shape
    m = c.shape[0]
    tn = 1024
    return pl.pallas_call(
        _mod_kernel,
        grid=(n_l, n // tn),
        in_specs=[pl.BlockSpec((m, d), lambda l, j: (0, 0)),
                  pl.BlockSpec((1, d, tn), lambda l, j: (l, 0, j)),
                  pl.BlockSpec((1, 1, tn), lambda l, j: (l, 0, j))],
        out_specs=pl.BlockSpec((1, m, tn), lambda l, j: (l, 0, j)),
        out_shape=jax.ShapeDtypeStruct((n_l, m, n), F32),
        compiler_params=_params(2),
    )(c, w, b.reshape(n_l, 1, n))


class _Tok:
    def __init__(self, n_seq, seq_len, per_row):
        self.n_seq, self.seq_len = n_seq, seq_len
        self.n_tok = n_seq * seq_len
        self.per_row = per_row

    def tiles(self, tm):
        if self.per_row:
            return self.n_tok, 1
        assert self.seq_len % tm == 0
        return tm, self.n_tok // tm

    def x_spec(self, tm, width=D_MODEL):
        return pl.BlockSpec((tm, width), lambda i: (i, 0))

    def mod_spec(self, tm):
        if self.per_row:
            return pl.BlockSpec((1, tm, D_MODEL), lambda i: (0, 0, 0))
        per_seq = self.seq_len // tm
        return pl.BlockSpec((1, 1, D_MODEL), lambda i: (i // per_seq, 0, 0))

    def mod_arr(self, m):
        if self.per_row:
            return m.reshape(1, self.n_tok, D_MODEL)
        return m.reshape(self.n_seq, 1, D_MODEL)


def _ffn_kernel(x_ref, sh_ref, sc_ref, gt_ref, g_ref, wg_ref, wu_ref, wd_ref, o_ref, a_scr):
    x = x_ref[...]
    hb = _ada_norm(x, g_ref[...], sh_ref[0], sc_ref[0]).astype(BF16)
    for c in range(D_FF // FF_CHUNK):
        cs = slice(c * FF_CHUNK, (c + 1) * FF_CHUNK)
        gate = _dot(hb, wg_ref[:, cs].astype(BF16))
        up = _dot(hb, wu_ref[:, cs].astype(BF16))
        a_scr[:, cs] = (gate * jax.nn.sigmoid(gate) * up).astype(BF16)
    y = _dot(a_scr[...], wd_ref[...].astype(BF16))
    o_ref[...] = x + (0.5 * gt_ref[0]) * y


def _ffn(tok, x, shift, scale, gate, g, wg, wu, wd):
    tm, nt = tok.tiles(512)
    return pl.pallas_call(
        _ffn_kernel,
        grid=(nt,),
        in_specs=[tok.x_spec(tm), tok.mod_spec(tm), tok.mod_spec(tm), tok.mod_spec(tm),
                  _full_spec((1, D_MODEL)), _full_spec((D_MODEL, D_FF)),
                  _full_spec((D_MODEL, D_FF)), _full_spec((D_FF, D_MODEL))],
        out_specs=tok.x_spec(tm),
        out_shape=jax.ShapeDtypeStruct((tok.n_tok, D_MODEL), F32),
        scratch_shapes=[pltpu.VMEM((tm, D_FF), BF16)],
        compiler_params=_params(1),
    )(x, tok.mod_arr(shift), tok.mod_arr(scale), tok.mod_arr(gate), g.reshape(1, D_MODEL), wg, wu, wd)


def _gmlp_uv(x_ref, sh_ref, sc_ref, g_ref, wuv_ref, lng_ref, lnb_ref, u_scr, v_scr):
    hb = _ada_norm(x_ref[...], g_ref[...], sh_ref[0], sc_ref[0]).astype(BF16)
    for c in range(D_V // UV_CHUNK):
        cs = slice(c * UV_CHUNK, (c + 1) * UV_CHUNK)
        cv = slice(D_V + c * UV_CHUNK, D_V + (c + 1) * UV_CHUNK)
        u_scr[:, cs] = jax.nn.gelu(_dot(hb, wuv_ref[:, cs]))
        v_scr[:, cs] = jax.nn.gelu(_dot(hb, wuv_ref[:, cv]))
    v = v_scr[...]
    mu = jnp.mean(v, axis=-1, keepdims=True)
    vc = v - mu
    var = jnp.mean(vc * vc, axis=-1, keepdims=True)
    return vc * lax.rsqrt(var + EPS) * lng_ref[...] + lnb_ref[...]


def _gmlp_prompt_kernel(x_ref, sh_ref, sc_ref, gt_ref, g_ref, wuv_ref, lng_ref, lnb_ref,
                        wsp_ref, bsp_ref, wout_ref, o_ref, u_scr, v_scr, a_scr):
    v_scr[...] = _gmlp_uv(x_ref, sh_ref, sc_ref, g_ref, wuv_ref, lng_ref, lnb_ref, u_scr, v_scr)
    row = lax.broadcasted_iota(jnp.int32, (CHUNK, CHUNK), 0)
    col = lax.broadcasted_iota(jnp.int32, (CHUNK, CHUNK), 1)
    tm = x_ref.shape[0]
    for grp in range(N_GROUPS_A):
        w = jnp.where(row >= col, wsp_ref[grp], 0.0).astype(BF16)
        cs = slice(grp * CG, (grp + 1) * CG)
        for n in range(tm // CHUNK):
            rs = slice(n * CHUNK, (n + 1) * CHUNK)
            s = _dot(w, v_scr[rs, cs].astype(BF16)) + bsp_ref[:, cs]
            a_scr[rs, cs] = (u_scr[rs, cs] * s).astype(BF16)
    y = _dot(a_scr[...], wout_ref[...])
    o_ref[...] = x_ref[...] + gt_ref[0] * y


def _gmlp_sample_kernel(x_ref, sh_ref, sc_ref, gt_ref, g_ref, wuv_ref, lng_ref, lnb_ref,
                        wrow_ref, brow_ref, wout_ref, o_ref, vout_ref, u_scr, v_scr):
    vn = _gmlp_uv(x_ref, sh_ref, sc_ref, g_ref, wuv_ref, lng_ref, lnb_ref, u_scr, v_scr)
    vout_ref[...] = vn
    s = vn * wrow_ref[...] + brow_ref[...]
    y = _dot((u_scr[...] * s).astype(BF16), wout_ref[...])
    o_ref[...] = x_ref[...] + gt_ref[0] * y


def _gmlp(tok, x, shift, scale, gate, g, wuv, ln_g, ln_b, w_sp, b_sp, wout):
    tm, nt = tok.tiles(512)
    common = [tok.x_spec(tm), tok.mod_spec(tm), tok.mod_spec(tm), tok.mod_spec(tm),
              _full_spec((1, D_MODEL)), _full_spec((D_MODEL, 2 * D_V)),
              _full_spec((1, D_V)), _full_spec((1, D_V))]
    args = [x, tok.mod_arr(shift), tok.mod_arr(scale), tok.mod_arr(gate), g.reshape(1, D_MODEL), wuv,
            ln_g.reshape(1, D_V), ln_b.reshape(1, D_V)]
    x_shape = jax.ShapeDtypeStruct((tok.n_tok, D_MODEL), F32)
    if tok.per_row:
        wrow = jnp.repeat(w_sp[:, 0, 0], CG).reshape(1, D_V)
        brow = jnp.repeat(b_sp[:, 0], CG).reshape(1, D_V)
        return pl.pallas_call(
            _gmlp_sample_kernel,
            grid=(nt,),
            in_specs=common + [_full_spec((1, D_V)), _full_spec((1, D_V)), _full_spec((D_V, D_MODEL))],
            out_specs=[tok.x_spec(tm), tok.x_spec(tm, D_V)],
            out_shape=[x_shape, jax.ShapeDtypeStruct((tok.n_tok, D_V), F32)],
            scratch_shapes=[pltpu.VMEM((tm, D_V), F32), pltpu.VMEM((tm, D_V), F32)],
            compiler_params=_params(1),
        )(*args, wrow, brow, wout)
    bias = jnp.repeat(b_sp.T, CG, axis=1)
    out = pl.pallas_call(
        _gmlp_prompt_kernel,
        grid=(nt,),
        in_specs=common + [_full_spec((N_GROUPS_A, CHUNK, CHUNK)), _full_spec((CHUNK, D_V)),
                           _full_spec((D_V, D_MODEL))],
        out_specs=tok.x_spec(tm),
        out_shape=x_shape,
        scratch_shapes=[pltpu.VMEM((tm, D_V), F32), pltpu.VMEM((tm, D_V), F32),
                        pltpu.VMEM((tm, D_V), BF16)],
        compiler_params=_params(1),
    )(*args, w_sp, bias, wout)
    return out, None


def _kv_prompt_kernel(x_ref, sh_ref, sc_ref, g_ref, w_ref, wvt_ref, ak_ref, kv_ref, k_ref, vt_ref):
    hb = _ada_norm(x_ref[...], g_ref[...], sh_ref[0], sc_ref[0]).astype(BF16)
    kv = _dot(hb, w_ref[...])
    kv_ref[...] = kv
    vt = _dot_nt(wvt_ref[...], hb).astype(BF16)
    tm = x_ref.shape[0]
    gd = N_KV * HEAD_DIM
    ones_rows = jnp.where(lax.broadcasted_iota(jnp.int32, (VROWS - HEAD_DIM, LANE), 0) == 0, 1.0, 0.0).astype(BF16)
    for c in range(tm // LANE):
        rs = slice(c * LANE, (c + 1) * LANE)
        for hd in range(2 * N_KV):
            vt_ref[0, c, hd * VROWS:hd * VROWS + HEAD_DIM, :] = vt[hd * HEAD_DIM:(hd + 1) * HEAD_DIM, rs]
            vt_ref[0, c, hd * VROWS + HEAD_DIM:(hd + 1) * VROWS, :] = ones_rows
        for j, slot in enumerate((2, 4)):
            for grp in range(N_KV):
                col = slot * gd + grp * HEAD_DIM
                k_ref[0, j * N_KV + grp, c, :, 0:HEAD_DIM] = kv[rs, col:col + HEAD_DIM].astype(BF16)
                k_ref[0, j * N_KV + grp, c, :, HEAD_DIM:LANE] = ak_ref[...]


def _kv_sample_kernel(x_ref, sh_ref, sc_ref, g_ref, w_ref, kv_ref):
    hb = _ada_norm(x_ref[...], g_ref[...], sh_ref[0], sc_ref[0]).astype(BF16)
    kv_ref[...] = _dot(hb, w_ref[...])


def _kv_proj(tok, x, shift, scale, g, w, wvt, alibi_k):
    tm, nt = tok.tiles(512)
    n_kv = N_KV_PROJ * N_KV * HEAD_DIM
    in_specs = [tok.x_spec(tm), tok.mod_spec(tm), tok.mod_spec(tm), _full_spec((1, D_MODEL)),
                _full_spec((D_MODEL, n_kv))]
    args = [x, tok.mod_arr(shift), tok.mod_arr(scale), g.reshape(1, D_MODEL), w]
    kv_shape = jax.ShapeDtypeStruct((tok.n_tok, n_kv), F32)
    if tok.per_row:
        return pl.pallas_call(
            _kv_sample_kernel, grid=(nt,), in_specs=in_specs, out_specs=tok.x_spec(tm, n_kv),
            out_shape=kv_shape, compiler_params=_params(1))(*args)
    per_seq = tok.seq_len // tm
    n_qt = tok.seq_len // LANE
    sub = tm // LANE
    return pl.pallas_call(
        _kv_prompt_kernel,
        grid=(nt,),
        in_specs=in_specs + [_full_spec((2 * N_KV * HEAD_DIM, D_MODEL)), _full_spec((LANE, HEAD_DIM))],
        out_specs=[tok.x_spec(tm, n_kv),
                   pl.BlockSpec((1, 2 * N_KV, sub, LANE, LANE),
                                lambda i: (i // per_seq, 0, i % per_seq, 0, 0)),
                   pl.BlockSpec((1, sub, 2 * N_KV * VROWS, LANE),
                                lambda i: (i // per_seq, i % per_seq, 0, 0))],
        out_shape=[kv_shape,
                   jax.ShapeDtypeStruct((tok.n_seq, 2 * N_KV, n_qt, LANE, LANE), BF16),
                   jax.ShapeDtypeStruct((tok.n_seq, n_qt, 2 * N_KV * VROWS, LANE), BF16)],
        compiler_params=_params(1),
    )(*args, wvt, alibi_k)


def _compress_kernel(tbl_ref, *refs):
    pages = refs[:PAGES_PER_SEG + 1]
    perm_ref, w1_ref, w2_ref, w2t_ref, pe_ref, akc_ref, kc_ref, vct_ref, x_scr, hb_scr = refs[PAGES_PER_SEG + 1:]
    del tbl_ref
    perm = perm_ref[...]
    n_heads = 2 * N_KV
    half = CMP_STRIDE * HEAD_DIM
    for p in range(PAGES_PER_SEG + 1):
        xp = _dot(perm, pages[p][0].astype(BF16))
        for hd in range(n_heads):
            cs = slice(hd * HEAD_DIM, (hd + 1) * HEAD_DIM)
            x_scr[hd, p * 8:(p + 1) * 8, :] = jnp.concatenate(
                [xp[s * 8:(s + 1) * 8, cs] for s in range(CMP_STRIDE)], axis=1)
    n_blk = PAGES_PER_SEG * 8

    def one_slot(slot):
        x = x_scr[slot * N_KV:(slot + 1) * N_KV].reshape(N_KV * XROWS, half).astype(BF16)
        ha = _dot(x, w1_ref[slot, 0:half, :])
        yield
        hb_scr[slot] = _dot(x, w1_ref[slot, half:2 * half, :])
        pe_hid = _dot(pe_ref[slot], w1_ref[slot])[0:1]
        yield
        for grp in range(N_KV):
            hid = ha[grp * XROWS:grp * XROWS + n_blk] + hb_scr[slot, pl.ds(grp * XROWS + 1, n_blk), :] + pe_hid
            act = jax.nn.gelu(hid).astype(BF16)
            if slot == 0:
                kc_ref[0, grp, :, 0:HEAD_DIM] = _dot(act, w2_ref[0]).astype(BF16)
                kc_ref[0, grp, :, HEAD_DIM:LANE] = akc_ref[...]
            else:
                vct_ref[0, grp, 0:HEAD_DIM, :] = _dot_nt(w2t_ref[1], act).astype(BF16)
                vct_ref[0, grp, HEAD_DIM:VROWS, :] = jnp.where(
                    lax.broadcasted_iota(jnp.int32, (VROWS - HEAD_DIM, n_blk), 0) == 0, 1.0, 0.0).astype(BF16)
            yield

    _interleave(one_slot(0), one_slot(1))


def _compress(src, table, w1, w2, w2t, pe, alibi_kc, n_seq, n_seg):
    n_blk = PAGES_PER_SEG * 8
    gd2 = 2 * N_KV * HEAD_DIM
    dst = jnp.arange(PAGE)
    src_row = (dst % 8) * CMP_STRIDE + dst // 8
    perm = (jnp.arange(PAGE)[None, :] == src_row[:, None]).astype(BF16)

    def page_spec(p):
        return pl.BlockSpec((1, PAGE, gd2), lambda i, tbl, p=p: (tbl[i, p], 0, 0))

    grid_spec = pltpu.PrefetchScalarGridSpec(
        num_scalar_prefetch=1,
        grid=(n_seq * n_seg,),
        in_specs=[page_spec(p) for p in range(PAGES_PER_SEG + 1)] + [
            pl.BlockSpec((PAGE, PAGE), lambda i, tbl: (0, 0)),
            pl.BlockSpec(w1.shape, lambda i, tbl: (0, 0, 0)),
            pl.BlockSpec(w2.shape, lambda i, tbl: (0, 0, 0)),
            pl.BlockSpec(w2t.shape, lambda i, tbl: (0, 0, 0)),
            pl.BlockSpec(pe.shape, lambda i, tbl: (0, 0, 0)),
            pl.BlockSpec((LANE, HEAD_DIM), lambda i, tbl: (0, 0))],
        out_specs=[pl.BlockSpec((1, N_KV, n_blk, LANE), lambda i, tbl: (i // n_seg, 0, i % n_seg, 0)),
                   pl.BlockSpec((1, N_KV, VROWS, n_blk), lambda i, tbl: (i // n_seg, 0, 0, i % n_seg))],
        scratch_shapes=[pltpu.VMEM((2 * N_KV, XROWS, CMP_STRIDE * HEAD_DIM), F32),
                        pltpu.VMEM((2, N_KV * XROWS, CMP_HID), F32)])
    return pl.pallas_call(
        _compress_kernel,
        grid_spec=grid_spec,
        out_shape=[jax.ShapeDtypeStruct((n_seq, N_KV, n_seg * n_blk, LANE), BF16),
                   jax.ShapeDtypeStruct((n_seq, N_KV, VROWS, n_seg * n_blk), BF16)],
        compiler_params=_params(1),
    )(table, *([src] * (PAGES_PER_SEG + 1)), perm, w1, w2, w2t, pe, alibi_kc)


def _qg_prompt_kernel(x_ref, sh_ref, sc_ref, g_ref, wq_ref, wgt_ref, aq_ref, q_ref, gt_ref):
    hb = _ada_norm(x_ref[...], g_ref[...], sh_ref[0], sc_ref[0]).astype(BF16)
    q = _dot(hb, wq_ref[...]) * (HEAD_DIM ** -0.5 * LOG2E)
    tm = x_ref.shape[0]
    for h in range(N_HEADS):
        q_ref[0, h, :, 0:HEAD_DIM] = q[:, h * HEAD_DIM:(h + 1) * HEAD_DIM].astype(BF16)
        for c in range(tm // LANE):
            q_ref[0, h, c * LANE:(c + 1) * LANE, HEAD_DIM:LANE] = aq_ref[h]
    gates = jax.nn.sigmoid(_dot_nt(wgt_ref[...], hb))
    for grp in range(N_KV):
        gt_ref[0, grp] = gates[grp * 16:(grp + 1) * 16, :]


def _qg_sample_kernel(x_ref, sh_ref, sc_ref, g_ref, wq_ref, wg_ref, q_ref, gt_ref):
    hb = _ada_norm(x_ref[...], g_ref[...], sh_ref[0], sc_ref[0]).astype(BF16)
    q_ref[...] = _dot(hb, wq_ref[...]) * (HEAD_DIM ** -0.5)
    gt_ref[...] = jax.nn.sigmoid(_dot(hb, wg_ref[...]))


def _qg_proj(tok, x, shift, scale, g, wq, wg_t, wg_nat, alibi_q):
    tm, nt = tok.tiles(512)
    in_specs = [tok.x_spec(tm), tok.mod_spec(tm), tok.mod_spec(tm), _full_spec((1, D_MODEL)),
                _full_spec((D_MODEL, D_MODEL))]
    args = [x, tok.mod_arr(shift), tok.mod_arr(scale), g.reshape(1, D_MODEL), wq]
    if tok.per_row:
        return pl.pallas_call(
            _qg_sample_kernel, grid=(nt,),
            in_specs=in_specs + [_full_spec((D_MODEL, LANE))],
            out_specs=[tok.x_spec(tm), tok.x_spec(tm, LANE)],
            out_shape=[jax.ShapeDtypeStruct((tok.n_tok, D_MODEL), F32),
                       jax.ShapeDtypeStruct((tok.n_tok, LANE), F32)],
            compiler_params=_params(1))(*args, wg_nat)
    per_seq = tok.seq_len // tm
    return pl.pallas_call(
        _qg_prompt_kernel, grid=(nt,),
        in_specs=in_specs + [_full_spec((N_KV * 16, D_MODEL)), _full_spec((N_HEADS, LANE, HEAD_DIM))],
        out_specs=[pl.BlockSpec((1, N_HEADS, tm, LANE), lambda i: (i // per_seq, 0, i % per_seq, 0)),
                   pl.BlockSpec((1, N_KV, 16, tm), lambda i: (i // per_seq, 0, 0, i % per_seq))],
        out_shape=[jax.ShapeDtypeStruct((tok.n_seq, N_HEADS, tok.seq_len, LANE), BF16),
                   jax.ShapeDtypeStruct((tok.n_seq, N_KV, 16, tok.seq_len), F32)],
        compiler_params=_params(1))(*args, wg_t, alibi_q)


def _select_blocks(work, idx_f, axis, rounds):
    for _ in range(rounds):
        mx = jnp.max(work, axis=axis, keepdims=True)
        first = jnp.min(jnp.where(work == mx, idx_f, 1e9), axis=axis, keepdims=True)
        first = jnp.where(mx > NEG / 2, first, -1.0)
        work = jnp.where(idx_f == first, PICKED, work)
        yield
    return jnp.where(work == PICKED, 1.0, 0.0)


def _softmax_group(scores, offsets, m_old, exp_dtype=F32, values=None):
    m_new = m_old
    for s, off in zip(scores, offsets):
        m_new = jnp.maximum(m_new, jnp.max(s, axis=0, keepdims=True) - off)
        yield
    alpha = jnp.exp2(m_old - m_new)
    probs = []
    acc = None
    for k, (s, off) in enumerate(zip(scores, offsets)):
        probs.append(jnp.exp2((s - (m_new + off)).astype(exp_dtype)))
        if values is not None and (k % 2 == 1 or k == len(scores) - 1):
            n = 2 if k % 2 == 1 else 1
            part = _dot(jnp.concatenate(values[k + 1 - n:k + 1], axis=1), jnp.concatenate(probs[-n:], axis=0))
            acc = part if acc is None else acc + part
        yield
    return m_new, alpha, (probs if values is None else acc)


def _run(gen):
    try:
        while True:
            next(gen)
    except StopIteration as stop:
        return stop.value


def _interleave(*gens, steps=None):
    live = [(g, 1 if steps is None else steps[k]) for k, g in enumerate(gens)]
    while live:
        for item in list(live):
            try:
                for _ in range(item[1]):
                    next(item[0])
            except StopIteration:
                live.remove(item)


def _normalise(acc):
    l = acc[HEAD_DIM:HEAD_DIM + 1]
    return acc[0:HEAD_DIM] * jnp.where(l > 0.0, 1.0 / l, 0.0)


def _nsa_prompt_kernel(q_ref, g_ref, kc_ref, vct_ref, ks_ref, vst_ref, kw_ref, vwt_ref, cov_ref, hot_ref, sl_ref,
                       o_ref, qp_scr, oc_scr, m_scr, acc_scr, bits_ref, *, ncp, n_qt):
    j = pl.program_id(2)
    cols = HPG * LANE
    key_io = lax.broadcasted_iota(jnp.int32, (LANE, LANE), 0)
    tok_io = lax.broadcasted_iota(jnp.int32, (LANE, LANE), 1)
    sl = sl_ref[0]
    m_init = jnp.full((1, cols), M_FLOOR, F32)

    def tile4(a):
        return jnp.concatenate([a] * HPG, axis=1)

    @pl.when(j == 0)
    def _():
        qp_scr[1] = jnp.zeros(qp_scr.shape[1:], BF16)
        oc_scr[1] = jnp.zeros(oc_scr.shape[1:], F32)
        for w in range(LANE // 32):
            bits_ref[LANE // 32 + w] = 0

    i = jnp.maximum(j - 1, 0)
    slot = (j + 1) % 2

    def sel_keys(jp):
        return jnp.concatenate([hot_ref[jp], ks_ref[0, 0, jp]], axis=1)

    fi_a = i.astype(F32)
    qp = qp_scr[slot]
    rel = tile4((tok_io - key_io).astype(F32))

    def window_branch():
        qx = qp[:, LANE:2 * LANE]

        n_wt = WINDOW // LANE
        scores, offsets, vts = [], [], []
        tiles = [jnp.maximum(i - n_wt + kt_i, 0) for kt_i in range(n_wt + 1)]
        s_all = _dot_nt(jnp.concatenate([kw_ref[0, 0, t] for t in tiles], axis=0), qx)
        yield
        for kt_i in range(n_wt + 1):
            jt = i - n_wt + kt_i
            jt_c = tiles[kt_i]
            s = s_all[kt_i * LANE:(kt_i + 1) * LANE]
            if kt_i == 0:
                s = jnp.where(rel <= 0.0, s, NEG)
            elif kt_i == n_wt:
                s = jnp.where(rel >= 0.0, s, NEG)
            scores.append(s)
            offsets.append(sl * float(LANE * (n_wt - kt_i)) + jnp.where(jt >= 0, 0.0, MASK_OFF).astype(F32))
            vts.append(vwt_ref[0, jt_c])
        _, _, acc = yield from _softmax_group(scores, offsets, m_init, BF16, vts)
        return _normalise(acc)

    def near_tiles():
        fi = fi_a
        tiles = [0, i] + [jnp.maximum(i - r, 0) for r in range(1, N_NEAR + 1)]
        s_all = _dot_nt(jnp.concatenate([sel_keys(t) for t in tiles], axis=0), qp)
        yield
        scores = [s_all[k * LANE:(k + 1) * LANE] for k in range(len(tiles))]
        scores[1] = jnp.where(rel >= 0.0, scores[1], NEG)
        offsets = [sl * (LANE * fi) + jnp.where(i > N_NEAR, 0.0, MASK_OFF).astype(F32), jnp.zeros((1, cols), F32)]
        for r in range(1, N_NEAR + 1):
            offsets.append(sl * float(LANE * r) + jnp.where(i - r >= 0, 0.0, MASK_OFF).astype(F32))
        vts = [vst_ref[0, t] for t in tiles]
        m, _, acc = yield from _softmax_group(scores, offsets, m_init, BF16, vts)
        m_scr[...] = m
        acc_scr[...] = acc

    def selection_half():
        i_s = jnp.minimum(j, n_qt - 1)
        slot_s = j % 2
        fi = i_s.astype(F32)
        qx = q_ref[0].reshape(cols, LANE)

        n_chunks = ncp // LANE
        rel_c = tile4((tok_io - CMP_STRIDE * key_io).astype(F32))
        scores, offsets = [], []
        s_all = _dot_nt(kc_ref[0, 0], qx)
        yield
        for c in range(n_chunks):
            base = float(CMP_STRIDE * LANE * c + CMP_LEN - 1)
            s_c = s_all[c * LANE:(c + 1) * LANE]
            scores.append(jnp.where(rel_c >= base - LANE * fi, s_c, NEG))
            offsets.append(sl * (LANE * fi - base))
        _, _, probs = yield from _softmax_group(scores, offsets, m_init)
        p_hi, p_lo = _split_bf16(jnp.concatenate(probs, axis=0))
        acc_c = _dot(vct_ref[0, 0], p_hi)
        oc_scr[slot_s] = _normalise(acc_c)
        yield
        l_c = acc_c[HEAD_DIM:HEAD_DIM + 1]
        cov = cov_ref[...]
        imp4 = (_dot(cov, p_hi) + _dot(cov, p_lo)) * jnp.where(l_c > 0.0, 1.0 / l_c, 0.0)
        imp = imp4[:, 0:LANE]
        for h in range(1, HPG):
            imp = imp + imp4[:, h * LANE:(h + 1) * LANE]
        yield

        q_blk = 2 * i_s + tok_io // SEL_BLOCK
        forced = (key_io == 0) | (key_io == q_blk) | (key_io == q_blk - 1)
        valid = key_io <= q_blk
        work = jnp.where(valid & jnp.logical_not(forced), imp, NEG)
        sel = yield from _select_blocks(work, key_io.astype(F32), 0, N_SEL - 3)
        sel = jnp.where(valid & forced, 1.0, sel)
        sel_bias_t = jnp.where(sel.T > 0.5, 0.0, NEG).astype(BF16)
        for h in range(HPG):
            qp_scr[slot_s, h * LANE:(h + 1) * LANE, 0:LANE] = sel_bias_t
            qp_scr[slot_s, h * LANE:(h + 1) * LANE, LANE:2 * LANE] = q_ref[0, h]

        cnt = _dot_nt(jnp.ones((8, LANE), BF16), sel.astype(BF16))[0:1]
        lane = lax.broadcasted_iota(jnp.int32, (1, LANE), 1)
        live = (cnt > 0.0) & (lane >= 2) & (lane < 2 * (i_s - N_NEAR))
        for w in range(LANE // 32):
            bits_ref[slot_s * (LANE // 32) + w] = jnp.sum(
                jnp.where(live & (lane // 32 == w), jnp.left_shift(1, lane % 32), 0))

    box = []

    def window_result():
        box.append((yield from window_branch()))

    _interleave(selection_half(), window_result(), near_tiles(), steps=(2, 1, 1))
    o_win = box[0]

    def group_body(k, carry):
        word = bits_ref[slot * (LANE // 32) + k // 4]
        used = jnp.right_shift(word, (8 * k) % 32) & 255

        @pl.when(used != 0)
        def _():
            offsets, vts = [], []
            s_all = _dot_nt(jnp.concatenate([sel_keys(4 * k + r) for r in range(4)], axis=0), qp_scr[slot])
            scores = [s_all[r * LANE:(r + 1) * LANE] for r in range(4)]
            for r in range(4):
                jp = 4 * k + r
                dead = jnp.where((jp == 0) | (jp >= i - N_NEAR), MASK_OFF, 0.0).astype(F32)
                offsets.append(sl * (LANE * (i - jp)).astype(F32) + dead)
                vts.append(vst_ref[0, jp])
            m, alpha, acc = _run(_softmax_group(scores, offsets, m_scr[...], BF16, vts))
            m_scr[...] = m
            acc_scr[...] = alpha * acc_scr[...] + acc
        return carry

    lax.fori_loop(0, jnp.maximum(i - N_NEAR + 3, 0) // 4, group_body, 0)
    o_sel = _normalise(acc_scr[...])

    o_cmp = oc_scr[slot]
    for h in range(HPG):
        cs = slice(h * LANE, (h + 1) * LANE)
        g_c = g_ref[0, 0, 3 * h:3 * h + 1, :]
        g_s = g_ref[0, 0, 3 * h + 1:3 * h + 2, :]
        g_w = g_ref[0, 0, 3 * h + 2:3 * h + 3, :]
        o_ref[0, h * HEAD_DIM:(h + 1) * HEAD_DIM, :] = g_c * o_cmp[:, cs] + g_s * o_sel[:, cs] + g_w * o_win[:, cs]


def _nsa_prompt(q, gates_t, kc, vct, k_ext, v_t, cov_t, onehot, sl_rows, n_seq, seq_len):
    n_qt = seq_len // LANE
    ncp = kc.shape[2]
    assert n_qt % 4 == 0
    kern = functools.partial(_nsa_prompt_kernel, ncp=ncp, n_qt=n_qt)
    cols = HPG * LANE
    last = n_qt - 1

    def sel_tile(j):
        return jnp.minimum(j, last)

    def att_tile(j):
        return jnp.maximum(j - 1, 0)

    return pl.pallas_call(
        kern,
        grid=(n_seq, N_KV, n_qt + 1),
        in_specs=[pl.BlockSpec((1, HPG, LANE, LANE), lambda b, g, j: (b, g, sel_tile(j), 0)),
                  pl.BlockSpec((1, 1, 16, LANE), lambda b, g, j: (b, g, 0, att_tile(j))),
                  pl.BlockSpec((1, 1, ncp, LANE), lambda b, g, j: (b, g, 0, 0)),
                  pl.BlockSpec((1, 1, VROWS, ncp), lambda b, g, j: (b, g, 0, 0)),
                  pl.BlockSpec((1, 1, n_qt, LANE, LANE), lambda b, g, j: (b, g, 0, 0, 0)),
                  pl.BlockSpec((1, n_qt, VROWS, LANE), lambda b, g, j: (b, 0, g, 0)),
                  pl.BlockSpec((1, 1, n_qt, LANE, LANE), lambda b, g, j: (b, N_KV + g, 0, 0, 0)),
                  pl.BlockSpec((1, n_qt, VROWS, LANE), lambda b, g, j: (b, 0, N_KV + g, 0)),
                  pl.BlockSpec((LANE, ncp), lambda b, g, j: (0, 0)),
                  pl.BlockSpec((n_qt, LANE, LANE), lambda b, g, j: (0, 0, 0)),
                  pl.BlockSpec((1, 1, cols), lambda b, g, j: (g, 0, 0))],
        out_specs=pl.BlockSpec((1, HPG * HEAD_DIM, LANE), lambda b, g, j: (b, g, att_tile(j))),
        out_shape=jax.ShapeDtypeStruct((n_seq, N_HEADS * HEAD_DIM, seq_len), F32),
        scratch_shapes=[pltpu.VMEM((2, cols, 2 * LANE), BF16),
                        pltpu.VMEM((2, HEAD_DIM, cols), F32),
                        pltpu.VMEM((1, cols), F32),
                        pltpu.VMEM((VROWS, cols), F32),
                        pltpu.SMEM((2 * (LANE // 32),), jnp.int32)],
        compiler_params=_params(3),
    )(q, gates_t, kc, vct, k_ext, v_t, k_ext, v_t, cov_t, onehot, sl_rows)


def _nsa_sample_kernel(tbl_ref, *refs, n_pages, n_e):
    del tbl_ref
    shared = refs[n_e * n_pages:]
    _interleave(*[_sample_sequence(e, refs[e * n_pages:(e + 1) * n_pages], *shared) for e in range(n_e)])


def _sample_sequence(e, pages, q_ref, g_ref, kc_ref, vct_ref, win_ref, new_ref, cov_ref, exp_ref, slope_ref,
                     o_ref, s_scr):
    n_pages = len(pages)
    gd = N_KV * HEAD_DIM
    past = n_pages * PAGE
    q_pos = float(past)
    qb = q_ref[e].astype(BF16)
    slope = slope_ref[...]
    row_grp = lax.broadcasted_iota(jnp.int32, (N_HEADS, LANE), 0) // HPG
    lane_f = lax.broadcasted_iota(jnp.int32, (N_HEADS, LANE), 1).astype(F32)
    new = new_ref[e]

    def own_rows(g_sel, pick):
        out = pick(0)
        for grp in range(1, N_KV):
            out = jnp.where(g_sel == grp, pick(grp), out)
        return out

    q_f = q_ref[e]
    row_grp_o = lax.broadcasted_iota(jnp.int32, (N_HEADS, HEAD_DIM), 0) // HPG
    q_pair = [jnp.concatenate([jnp.where(row_grp_o == 2 * pr, q_f, 0.0), jnp.where(row_grp_o == 2 * pr + 1, q_f, 0.0)],
                              axis=1).astype(BF16) for pr in range(N_KV // 2)]
    first_row2 = lax.broadcasted_iota(jnp.int32, (LANE, 2 * HEAD_DIM), 0) == 0

    def pair_scores(tile):
        return _dot_nt(q_pair[0], tile(0)) + _dot_nt(q_pair[1], tile(1))

    def own_halves(o_pairs):
        out = o_pairs[0][:, 0:HEAD_DIM]
        for grp in range(1, N_KV):
            half = o_pairs[grp // 2][:, (grp % 2) * HEAD_DIM:(grp % 2 + 1) * HEAD_DIM]
            out = jnp.where(row_grp_o == grp, half, out)
        return out

    def new_slab(slot, pr):
        col = slot * gd + pr * 2 * HEAD_DIM
        return jnp.where(first_row2, new[:, col:col + 2 * HEAD_DIM], 0.0).astype(BF16)

    def softmax_rows(s, valid):
        s = jnp.where(valid, s, NEG)
        m = jnp.max(s, axis=1, keepdims=True)
        p = jnp.where(valid, jnp.exp(s - m), 0.0)
        l = jnp.sum(p, axis=1, keepdims=True)
        return p * jnp.where(l > 0.0, 1.0 / l, 0.0)

    d_c = q_pos - (CMP_STRIDE * lane_f + (CMP_LEN - 1))
    s_c = own_rows(row_grp, lambda grp: _dot_nt(qb, kc_ref[e, grp][:, 0:HEAD_DIM])) - slope * d_c
    yield
    p_c = softmax_rows(s_c, d_c >= 0.0)
    p_cb = p_c.astype(BF16)
    o_c = own_rows(row_grp_o, lambda grp: _dot_nt(p_cb, vct_ref[e, grp, 0:HEAD_DIM, :]))
    yield

    p_grp = own_rows(row_grp, lambda grp: jnp.broadcast_to(
        jnp.sum(p_c[grp * HPG:(grp + 1) * HPG], axis=0, keepdims=True), (N_HEADS, LANE)))
    p_hi, p_lo = _split_bf16(jnp.concatenate([p_grp, jnp.zeros((LANE - N_HEADS, LANE), F32)], axis=0))
    cov_t = cov_ref[...]
    imp_t = _dot_nt(cov_t, p_hi) + _dot_nt(cov_t, p_lo)
    blk = lax.broadcasted_iota(jnp.int32, (LANE, LANE), 0)
    q_blk = past // SEL_BLOCK
    forced = (blk == 0) | (blk == q_blk) | (blk == q_blk - 1)
    valid = blk <= q_blk
    work = jnp.where(valid & jnp.logical_not(forced), imp_t, NEG)
    yield
    sel_t = yield from _select_blocks(work, blk.astype(F32), 0, N_SEL - 3)
    sel = jnp.where(valid & forced, 1.0, sel_t).T[0:N_HEADS]

    n_t = n_pages + 1
    for t in range(n_t):
        if t < n_pages:
            tile = lambda pr, t=t: pages[t][0, :, pr * LANE:(pr + 1) * LANE].astype(BF16)
        else:
            tile = lambda pr: new_slab(2, pr)
        s_scr[e, :, t * LANE:(t + 1) * LANE] = pair_scores(tile)
        if t % 4 == 3:
            yield
    width = n_t * LANE
    kpos = lax.broadcasted_iota(jnp.int32, (N_HEADS, width), 1).astype(F32)
    d_s = q_pos - kpos
    chosen = _dot(sel.astype(BF16), exp_ref[...])
    slope_w = jnp.concatenate([slope] * n_t, axis=1)
    p_s = softmax_rows(s_scr[e] - slope_w * d_s, (chosen > 0.5) & (d_s >= 0.0)).astype(BF16)
    yield
    o_sp = [jnp.zeros((N_HEADS, 2 * HEAD_DIM), F32) for _ in range(N_KV // 2)]
    for t in range(n_t):
        pt = p_s[:, t * LANE:(t + 1) * LANE]
        if t < n_pages:
            vtile = lambda pr, t=t: pages[t][0, :, gd + pr * LANE: gd + (pr + 1) * LANE].astype(BF16)
        else:
            vtile = lambda pr: new_slab(3, pr)
        o_sp = [o_sp[pr] + _dot(pt, vtile(pr)) for pr in range(N_KV // 2)]
        if t % 4 == 3:
            yield
    o_s = own_halves(o_sp)

    buf = win_ref.shape[1]
    n_w = buf // LANE + 1
    w_parts = []
    for t in range(n_w):
        if t < n_w - 1:
            tile = lambda pr, t=t: win_ref[e, t * LANE:(t + 1) * LANE, pr * LANE:(pr + 1) * LANE].astype(BF16)
        else:
            tile = lambda pr: new_slab(4, pr)
        w_parts.append(pair_scores(tile))
    s_w = jnp.concatenate(w_parts, axis=1)
    yield
    wpos = lax.broadcasted_iota(jnp.int32, (N_HEADS, n_w * LANE), 1).astype(F32) + float(past - buf)
    d_w = q_pos - wpos
    slope_ww = jnp.concatenate([slope] * n_w, axis=1)
    p_w = softmax_rows(s_w - slope_ww * d_w, (d_w >= 0.0) & (d_w <= float(WINDOW))).astype(BF16)
    yield
    o_wp = [jnp.zeros((N_HEADS, 2 * HEAD_DIM), F32) for _ in range(N_KV // 2)]
    for t in range(n_w):
        pt = p_w[:, t * LANE:(t + 1) * LANE]
        if t < n_w - 1:
            vtile = lambda pr, t=t: win_ref[e, t * LANE:(t + 1) * LANE,
                                            gd + pr * LANE: gd + (pr + 1) * LANE].astype(BF16)
        else:
            vtile = lambda pr: new_slab(5, pr)
        o_wp = [o_wp[pr] + _dot(pt, vtile(pr)) for pr in range(N_KV // 2)]
    o_w = own_halves(o_wp)

    gt = g_ref[e]
    o_ref[e] = gt[:, 0:1] * o_c + gt[:, 1:2] * o_s + gt[:, 2:3] * o_w


def _nsa_sample(q, gates, kc, vct, cache_pages, page_table, win, kv_new, cov, expand, slope16):
    n_seq, n_pages = page_table.shape
    gd = N_KV * HEAD_DIM
    buf = win.shape[1]
    n_e = SAMPLE_SEQS if n_seq % SAMPLE_SEQS == 0 else 1
    kern = functools.partial(_nsa_sample_kernel, n_pages=n_pages, n_e=n_e)

    def page_spec(e, p):
        return pl.BlockSpec((1, PAGE, 2 * gd), lambda b, tbl, e=e, p=p: (tbl[b * n_e + e, p], 0, 1))

    def const_spec(shape):
        nd = len(shape)
        return pl.BlockSpec(shape, lambda b, tbl: (0,) * nd)

    grid_spec = pltpu.PrefetchScalarGridSpec(
        num_scalar_prefetch=1,
        grid=(n_seq // n_e,),
        in_specs=[page_spec(e, p) for e in range(n_e) for p in range(n_pages)] + [
            pl.BlockSpec((n_e, N_HEADS, HEAD_DIM), lambda b, tbl: (b, 0, 0)),
            pl.BlockSpec((n_e, N_HEADS, 3), lambda b, tbl: (b, 0, 0)),
            pl.BlockSpec((n_e, N_KV, LANE, LANE), lambda b, tbl: (b, 0, 0, 0)),
            pl.BlockSpec((n_e, N_KV, VROWS, LANE), lambda b, tbl: (b, 0, 0, 0)),
            pl.BlockSpec((n_e, buf, 2 * gd), lambda b, tbl: (b, 0, 0)),
            pl.BlockSpec((n_e, 1, N_KV_PROJ * gd), lambda b, tbl: (b, 0, 0)),
            const_spec(cov.shape), const_spec(expand.shape), const_spec(slope16.shape)],
        out_specs=pl.BlockSpec((n_e, N_HEADS, HEAD_DIM), lambda b, tbl: (b, 0, 0)),
        scratch_shapes=[pltpu.VMEM((n_e, N_HEADS, (n_pages + 1) * LANE), F32)])
    return pl.pallas_call(
        kern,
        grid_spec=grid_spec,
        out_shape=jax.ShapeDtypeStruct((n_seq, N_HEADS, HEAD_DIM), F32),
        compiler_params=_params(1),
    )(page_table, *([cache_pages] * (n_e * n_pages)), q, gates, kc, vct, win, kv_new, cov, expand, slope16)


def _oproj_prompt_kernel(ot_ref, x_ref, gt_ref, wo_ref, o_ref):
    o = ot_ref[0].T.astype(BF16)
    o_ref[...] = x_ref[...] + gt_ref[0] * _dot(o, wo_ref[...])


def _oproj_sample_kernel(a_ref, x_ref, gt_ref, wo_ref, o_ref):
    o_ref[...] = x_ref[...] + gt_ref[0] * _dot(a_ref[...].astype(BF16), wo_ref[...])


def _out_proj(tok, attn, x, gate, wo):
    tm, nt = tok.tiles(512)
    if tok.per_row:
        kern, a_spec = _oproj_sample_kernel, tok.x_spec(tm)
    else:
        per_seq = tok.seq_len // tm
        kern = _oproj_prompt_kernel
        a_spec = pl.BlockSpec((1, D_MODEL, tm), lambda i: (i // per_seq, 0, i % per_seq))
    return pl.pallas_call(
        kern, grid=(nt,),
        in_specs=[a_spec, tok.x_spec(tm), tok.mod_spec(tm), _full_spec((D_MODEL, D_MODEL))],
        out_specs=tok.x_spec(tm),
        out_shape=jax.ShapeDtypeStruct((tok.n_tok, D_MODEL), F32),
        compiler_params=_params(1))(attn, x, tok.mod_arr(gate), wo)


def _final_kernel(x_ref, sh_ref, sc_ref, g_ref, o_ref):
    o_ref[...] = _ada_norm(x_ref[...], g_ref[...], sh_ref[0], sc_ref[0])


def _final(tok, x, shift, scale, g):
    tm, nt = tok.tiles(512)
    return pl.pallas_call(
        _final_kernel, grid=(nt,),
        in_specs=[tok.x_spec(tm), tok.mod_spec(tm), tok.mod_spec(tm), _full_spec((1, D_MODEL))],
        out_specs=tok.x_spec(tm),
        out_shape=jax.ShapeDtypeStruct((tok.n_tok, D_MODEL), F32),
        compiler_params=_params(1))(x, tok.mod_arr(shift), tok.mod_arr(scale), g.reshape(1, D_MODEL))


def _alibi_slopes():
    h = jnp.arange(1, N_HEADS + 1, dtype=F32)
    return jnp.exp2(-8.0 * h / N_HEADS)


def _cover(n_cmp, n_sel):
    c_start = jnp.arange(n_cmp)[:, None] * CMP_STRIDE
    s_start = jnp.arange(n_sel)[None, :] * SEL_BLOCK
    return ((c_start < s_start + SEL_BLOCK) & (c_start + CMP_LEN > s_start)).astype(BF16)


def _trunk(tok, x, mods, kv_mod, f_mod, wts, ctx):
    depth = wts['norm_g'].shape[0]
    n_a = depth // 2
    v_rows = []
    kv = None
    attn_ctx = None
    for l in range(depth):
        m = mods[l]
        if l == n_a:
            kv_out = _kv_proj(tok, x, kv_mod[:, 0], kv_mod[:, 1], wts['kv_norm_g'], wts['kv_w'], wts['kv_wvt'],
                              wts['alibi_k'])
            kv, attn_ctx = ctx['prepare'](kv_out)
        x = _ffn(tok, x, m[:, 0], m[:, 1], m[:, 2], wts['norm_g'][l, 0],
                 wts['ffn_w_gate'][l, 0], wts['ffn_w_up'][l, 0], wts['ffn_w_down'][l, 0])
        if l < n_a:
            x, v = _gmlp(tok, x, m[:, 3], m[:, 4], m[:, 5], wts['norm_g'][l, 1], wts['gmlp_w_uv'][l],
                         wts['gmlp_ln_g'][l], wts['gmlp_ln_b'][l], wts['gmlp_w_sp'][l], wts['gmlp_b_sp'][l],
                         wts['gmlp_w_out'][l])
            v_rows.append(v)
        else:
            j = l - n_a
            q, gates = _qg_proj(tok, x, m[:, 3], m[:, 4], wts['norm_g'][l, 1], wts['nsa_wq'][j],
                                wts['nsa_wg_t'][j], wts['nsa_wg'][j], wts['alibi_q'])
            attn = ctx['attend'](q, gates, attn_ctx)
            x = _out_proj(tok, attn, x, m[:, 5], wts['nsa_w_o'][j])
        x = _ffn(tok, x, m[:, 6], m[:, 7], m[:, 8], wts['norm_g'][l, 2],
                 wts['ffn_w_gate'][l, 1], wts['ffn_w_up'][l, 1], wts['ffn_w_down'][l, 1])
    y = _final(tok, x, f_mod[:, 0], f_mod[:, 1], wts['final_g'])
    return y, kv, v_rows


def kernel(x_prompt, x_sample, cache_kv, state_win_kv, page_table, c_prompt, c_sample, ada_w, ada_b, norm_g, ffn_w_gate, ffn_w_up, ffn_w_down, gmlp_w_uv, gmlp_ln_g, gmlp_ln_b, gmlp_w_sp, gmlp_b_sp, gmlp_w_out, nsa_w_qg, nsa_w_o, kv_norm_g, kv_ada_w, kv_ada_b, kv_w, cmp_w1, cmp_w2, cmp_pe, final_g, final_ada_w, final_ada_b):
    n_p, seq, _ = x_prompt.shape
    n_s, dec_seq, _ = x_sample.shape
    assert dec_seq == 1 and seq % (PAGES_PER_SEG * PAGE) == 0
    depth = ada_w.shape[0]
    n_b = nsa_w_qg.shape[0]
    gd = N_KV * HEAD_DIM
    n_pages = page_table.shape[1]
    assert n_pages == PAGES_PER_SEG
    past = n_pages * PAGE

    nq = N_HEADS * HEAD_DIM
    wg_cols = nsa_w_qg[:, :, nq:]
    wg_pad = jnp.pad(wg_cols.reshape(n_b, D_MODEL, N_KV, HPG * 3), ((0, 0), (0, 0), (0, 0), (0, 16 - HPG * 3)))
    wts = dict(
        norm_g=norm_g, kv_norm_g=kv_norm_g, final_g=final_g,
        ffn_w_gate=ffn_w_gate, ffn_w_up=ffn_w_up, ffn_w_down=ffn_w_down,
        gmlp_w_uv=gmlp_w_uv.astype(BF16), gmlp_ln_g=gmlp_ln_g, gmlp_ln_b=gmlp_ln_b,
        gmlp_w_sp=gmlp_w_sp, gmlp_b_sp=gmlp_b_sp, gmlp_w_out=gmlp_w_out.astype(BF16),
        nsa_wq=nsa_w_qg[:, :, :nq].astype(BF16),
        nsa_wg_t=jnp.swapaxes(wg_pad.reshape(n_b, D_MODEL, N_KV * 16), 1, 2).astype(BF16),
        nsa_wg=jnp.pad(wg_cols, ((0, 0), (0, 0), (0, LANE - N_HEADS * 3))).astype(BF16),
        nsa_w_o=nsa_w_o.astype(BF16),
        kv_w=kv_w.astype(BF16),
        kv_wvt=jnp.concatenate([kv_w[:, 3 * gd:4 * gd], kv_w[:, 5 * gd:6 * gd]], axis=1).T.astype(BF16),
    )
    w1 = cmp_w1.astype(BF16)
    w2 = cmp_w2.astype(BF16)
    w2t = jnp.swapaxes(cmp_w2, 1, 2).astype(BF16)
    pe = jnp.broadcast_to(cmp_pe.astype(BF16).reshape(2, 1, CMP_LEN * HEAD_DIM), (2, 16, CMP_LEN * HEAD_DIM))
    slopes = _alibi_slopes()
    sl2 = slopes * LOG2E
    off_f = jnp.arange(LANE, dtype=F32)

    def split3(x):
        a = x.astype(BF16)
        b = (x - a.astype(F32)).astype(BF16)
        c = (x - a.astype(F32) - b.astype(F32)).astype(BF16)
        return [a, b, c]

    q_cols = split3(-sl2[:, None] * off_f[None, :]) + \
        [jnp.broadcast_to(c[:, None], (N_HEADS, LANE)) for c in split3(sl2)] + \
        [jnp.broadcast_to(c[:, None], (N_HEADS, LANE)) for c in split3(sl2 * CMP_STRIDE)]
    wts['alibi_q'] = jnp.pad(jnp.stack(q_cols, axis=-1), ((0, 0), (0, 0), (0, HEAD_DIM - 9)))
    ones, zeros, offs = jnp.ones((LANE,), BF16), jnp.zeros((LANE,), BF16), off_f.astype(BF16)
    wts['alibi_k'] = jnp.pad(jnp.stack([ones] * 3 + [offs] * 3 + [zeros] * 3, axis=-1), ((0, 0), (0, HEAD_DIM - 9)))
    alibi_kc = jnp.pad(jnp.stack([ones] * 3 + [zeros] * 3 + [offs] * 3, axis=-1), ((0, 0), (0, HEAD_DIM - 9)))

    n_c = n_p + n_s
    c_all = jnp.pad(jnp.concatenate([c_prompt, c_sample], axis=0), ((0, (-n_c) % 8), (0, 0)))
    mod_all = _mod_linear(c_all, ada_w, ada_b)
    kv_mod_all = _mod_linear(c_all, kv_ada_w[None], kv_ada_b[None])[0]
    f_mod_all = _mod_linear(c_all, final_ada_w[None], final_ada_b[None])[0]

    def rows(a, lo, hi, k):
        return a[..., lo:hi, :].reshape(a.shape[:-2] + (hi - lo, k, D_MODEL))

    tok_p = _Tok(n_p, seq, per_row=False)
    n_seg = seq // (PAGES_PER_SEG * PAGE)
    n_qt = seq // LANE
    ncp = n_seg * PAGES_PER_SEG * 8
    n_sel_p = seq // SEL_BLOCK
    assert n_sel_p <= LANE
    cov_t = jnp.pad(_cover(ncp, n_sel_p).T, ((0, LANE - n_sel_p), (0, 0)))
    sl_rows = jnp.repeat(sl2.reshape(N_KV, 1, HPG), LANE, axis=2)
    onehot = (jnp.arange(LANE)[None, None, :] ==
              (2 * jnp.arange(n_qt)[:, None, None] + jnp.arange(LANE)[None, :, None] // SEL_BLOCK)).astype(BF16)

    def prepare_p(kv_out):
        kv, k_nat, v_t = kv_out
        pages_per_seq = seq // PAGE
        base = jnp.arange(n_p * n_seg, dtype=jnp.int32)[:, None] * PAGES_PER_SEG
        table = jnp.minimum(base + jnp.arange(PAGES_PER_SEG + 1, dtype=jnp.int32)[None, :],
                            n_p * pages_per_seq - 1)
        kc, vct = _compress(kv.reshape(n_p * pages_per_seq, PAGE, N_KV_PROJ * gd), table, w1, w2, w2t, pe,
                            alibi_kc, n_p, n_seg)
        return kv, (kc, vct, k_nat, v_t)

    def attend_p(q, gates, c):
        kc, vct, k_nat, v_t = c
        return _nsa_prompt(q, gates, kc, vct, k_nat, v_t, cov_t, onehot, sl_rows, n_p, seq)

    y_p, kv_p, _ = _trunk(tok_p, x_prompt.reshape(n_p * seq, D_MODEL),
                          rows(mod_all, 0, n_p, 9), rows(kv_mod_all, 0, n_p, 2), rows(f_mod_all, 0, n_p, 2),
                          wts, dict(prepare=prepare_p, attend=attend_p))
    kv_p = kv_p.reshape(n_p, seq, N_KV_PROJ * gd)
    n_win = min(WINDOW, seq)
    kv_prompt = kv_p[:, :, :4 * gd].reshape(n_p, seq, 4, N_KV, HEAD_DIM)
    win_prompt = kv_p[:, seq - n_win:, 4 * gd:].reshape(n_p, n_win, 2, N_KV, HEAD_DIM)

    tok_s = _Tok(n_s, 1, per_row=True)
    cache_pages = cache_kv.reshape(cache_kv.shape[0], PAGE, 4 * gd)
    buf = state_win_kv.shape[1]
    win_flat = state_win_kv.reshape(n_s, buf, 2 * gd)
    n_sel_s = (past + 1 + SEL_BLOCK - 1) // SEL_BLOCK
    cov_s = jnp.pad(_cover(LANE, n_sel_s).T, ((0, LANE - n_sel_s), (0, 0)))
    n_keys = (n_pages + 1) * LANE
    expand_s = (jnp.arange(LANE)[:, None] == (jnp.arange(n_keys)[None, :] // SEL_BLOCK)).astype(BF16)
    slope16 = jnp.broadcast_to(slopes[:, None], (N_HEADS, LANE))

    def prepare_s(kv):
        table = jnp.concatenate([page_table, page_table[:, -1:]], axis=1)
        kc, vct = _compress(cache_pages, table, w1, w2, w2t, pe, alibi_kc, n_s, 1)
        return kv, (kc, vct, kv)

    def attend_s(q, gates, c):
        kc, vct, kv = c
        o = _nsa_sample(q.reshape(n_s, N_HEADS, HEAD_DIM), gates[:, :N_HEADS * 3].reshape(n_s, N_HEADS, 3),
                        kc, vct, cache_pages, page_table, win_flat, kv.reshape(n_s, 1, N_KV_PROJ * gd),
                        cov_s, expand_s, slope16)
        return o.reshape(n_s, N_HEADS * HEAD_DIM)

    y_s, kv_s, v_rows = _trunk(tok_s, x_sample.reshape(n_s, D_MODEL),
                               rows(mod_all, n_p, n_c, 9), rows(kv_mod_all, n_p, n_c, 2),
                               rows(f_mod_all, n_p, n_c, 2), wts, dict(prepare=prepare_s, attend=attend_s))
    kv_s = kv_s.reshape(n_s, 1, N_KV_PROJ, N_KV, HEAD_DIM)
    kv_sample = kv_s[:, :, :4]
    win_sample = jnp.concatenate([state_win_kv[:, 1:], kv_s[:, :, 4:6]], axis=1)
    gmlp_v_sample = jnp.stack([v.reshape(n_s, 1, D_V) for v in v_rows])

    return (y_p.reshape(n_p, seq, D_MODEL), y_s.reshape(n_s, 1, D_MODEL), kv_prompt, kv_sample,
            win_prompt, win_sample, gmlp_v_sample)
```
